```python
import math
import jax, jax.numpy as jnp
from jax import lax
import numpy as np

D_MODEL = 1024
BATCH = 8
SEQ = 2048
DEPTH = 2
DEC_BATCH = 128
DEC_SEQ = 1
PAST_LEN = 16384
PAGE_SIZE = 128

D_S5 = D_MODEL // 2
S5_CH_PER_GROUP = 16
S5_GROUPS = D_S5 // S5_CH_PER_GROUP
S5_STATE = 64
D_POOL = D_MODEL // 4
POOL_WINDOWS = (2, 4, 8, 16)
POOL_GROUP = D_POOL // len(POOL_WINDOWS)
POOL_BUF = max(POOL_WINDOWS) - 1
D_CONV = D_MODEL // 4
CONV_K = 31
CONV_BUF = CONV_K - 1
N_BRANCH = 3
D_IN = D_S5 + D_POOL + 2 * D_CONV + N_BRANCH * D_MODEL
D_FF = 2752
N_EXPERTS = 8
TOP_K = 2
N_DENSE = (DEPTH + 1) // 2
N_MOE = DEPTH // 2
ALPHA = (2.0 * DEPTH) ** 0.25
BETA = (8.0 * DEPTH) ** -0.25
LN_EPS = 1e-5

kernel_name = 'hybrid_s5_pool_conv_moe_decode_step'


def layer_norm(x, g, b):
    xf = x.astype(jnp.float32)
    mu = jnp.mean(xf, axis=-1, keepdims=True)
    var = jnp.mean(jnp.square(xf - mu), axis=-1, keepdims=True)
    return ((xf - mu) * lax.rsqrt(var + LN_EPS) * g.astype(jnp.float32) + b.astype(jnp.float32)).astype(x.dtype)


def _ssm_combine(e1, e2):
    a1r, a1i, b1r, b1i = e1
    a2r, a2i, b2r, b2i = e2
    return (a2r * a1r - a2i * a1i,
            a2r * a1i + a2i * a1r,
            a2r * b1r - a2i * b1i + b2r,
            a2r * b1i + a2i * b1r + b2i)


def s5_branch(u, h0_re, h0_im, p):
    B, T, _ = u.shape
    f32 = jnp.float32
    a_re = p['s5_a_re'].astype(f32)
    a_im = p['s5_a_im'].astype(f32)
    dt = jnp.exp(p['s5_log_dt'].astype(f32))[:, None]
    mag = jnp.exp(a_re * dt)
    lr = mag * jnp.cos(a_im * dt)
    li = mag * jnp.sin(a_im * dt)
    den = a_re * a_re + a_im * a_im
    nr = lr - 1.0
    cr = (nr * a_re + li * a_im) / den
    ci = (li * a_re - nr * a_im) / den
    b_re = p['s5_b_re'].astype(f32)
    b_im = p['s5_b_im'].astype(f32)
    bb_re = cr[..., None] * b_re - ci[..., None] * b_im
    bb_im = cr[..., None] * b_im + ci[..., None] * b_re
    uf = u.astype(f32).reshape(B, T, S5_GROUPS, S5_CH_PER_GROUP)
    bu_re = jnp.einsum('btgh,gph->btgp', uf, bb_re)
    bu_im = jnp.einsum('btgh,gph->btgp', uf, bb_im)
    h0_re = h0_re.astype(f32)
    h0_im = h0_im.astype(f32)
    bu_re = bu_re.at[:, 0].add(lr * h0_re - li * h0_im)
    bu_im = bu_im.at[:, 0].add(lr * h0_im + li * h0_re)
    a_sr = jnp.broadcast_to(lr, (1, T, S5_GROUPS, S5_STATE))
    a_si = jnp.broadcast_to(li, (1, T, S5_GROUPS, S5_STATE))
    _, _, h_re, h_im = lax.associative_scan(_ssm_combine, (a_sr, a_si, bu_re, bu_im), axis=1)
    y = (jnp.einsum('btgp,ghp->btgh', h_re, p['s5_c_re'].astype(f32))
         - jnp.einsum('btgp,ghp->btgh', h_im, p['s5_c_im'].astype(f32))
         + p['s5_d'].astype(f32) * uf)
    y = jax.nn.gelu(y.reshape(B, T, D_S5))
    y = y * jax.nn.sigmoid(y @ p['s5_w_glu'].astype(f32) + p['s5_b_glu'].astype(f32))
    return y.astype(u.dtype), h_re[:, -1], h_im[:, -1]


def pool_branch(u, buf, start_pos, pool_w, pool_scale):
    B, T, _ = u.shape
    cat = jnp.concatenate([buf.astype(u.dtype), u], axis=1)
    cs = jnp.cumsum(cat.astype(jnp.float32), axis=1)
    cs = jnp.pad(cs, ((0, 0), (1, 0), (0, 0)))
    end = cs[:, POOL_BUF + 1:]
    pos = start_pos + jnp.arange(T)
    outs = []
    for gi, w in enumerate(POOL_WINDOWS):
        sl = slice(gi * POOL_GROUP, (gi + 1) * POOL_GROUP)
        s = end[..., sl] - cs[:, POOL_BUF + 1 - w:POOL_BUF + 1 - w + T, sl]
        cnt = jnp.minimum(pos + 1, w).astype(jnp.float32)[None, :, None]
        pooled = s / cnt - u[..., sl].astype(jnp.float32)
        outs.append(jnp.einsum('btc,cd->btd', pooled, pool_w[gi].astype(jnp.float32)))
    y = jnp.concatenate(outs, axis=-1) * pool_scale.astype(jnp.float32)
    return y.astype(u.dtype), cat[:, -POOL_BUF:]


def conv_branch(u, buf, conv_w, conv_b, ln_g, ln_b):
    v = u[..., :D_CONV] * jax.nn.sigmoid(u[..., D_CONV:])
    cat = jnp.concatenate([buf.astype(v.dtype), v], axis=1)
    w = conv_w.astype(v.dtype)[:, None, :]
    y = lax.conv_general_dilated(cat, w, window_strides=(1,), padding='VALID',
                                 dimension_numbers=('NWC', 'WIO', 'NWC'),
                                 feature_group_count=D_CONV)
    y = jax.nn.silu(layer_norm(y + conv_b, ln_g, ln_b))
    return y, cat[:, -CONV_BUF:]


def mixer_block(x, start_pos, h0_re, h0_im, pool_buf, conv_buf, p):
    B, T, _ = x.shape
    proj = x @ p['w_in'] + p['b_in']
    i1 = D_S5
    i2 = i1 + D_POOL
    i3 = i2 + 2 * D_CONV
    y_s5, h_re, h_im = s5_branch(proj[..., :i1], h0_re, h0_im, p)
    y_pool, new_pool = pool_branch(proj[..., i1:i2], pool_buf, start_pos, p['pool_w'], p['pool_scale'])
    y_conv, new_conv = conv_branch(proj[..., i2:i3], conv_buf, p['conv_w'], p['conv_b'],
                                   p['conv_ln_g'], p['conv_ln_b'])
    gates = jax.nn.sigmoid(proj[..., i3:].astype(jnp.float32)).reshape(B, T, N_BRANCH, D_MODEL)
    merged = (gates[..., 0, :] * (y_s5 @ p['proj_s5'])
              + gates[..., 1, :] * (y_pool @ p['proj_pool'])
              + gates[..., 2, :] * (y_conv @ p['proj_conv'])).astype(x.dtype)
    out = merged @ p['w_out'] + p['b_out']
    return out, h_re, h_im, new_pool, new_conv


def swiglu(x, w_gate, w_up, w_down):
    return (jax.nn.silu(x @ w_gate) * (x @ w_up)) @ w_down


def moe_swiglu(x, router, w_gate, w_up, w_down):
    B, T, D = x.shape
    xt = x.reshape(B * T, D)
    probs = jax.nn.softmax((xt @ router).astype(jnp.float32), axis=-1)
    top_v, top_i = lax.top_k(probs, TOP_K)
    top_v = top_v / jnp.sum(top_v, axis=-1, keepdims=True)
    comb = jnp.sum(jax.nn.one_hot(top_i, N_EXPERTS, dtype=jnp.float32) * top_v[..., None], axis=1)
    out = jnp.zeros((B * T, D), jnp.float32)
    for e in range(N_EXPERTS):
        ye = swiglu(xt, w_gate[e], w_up[e], w_down[e]).astype(jnp.float32)
        out = out + comb[:, e:e + 1] * ye
    return out.astype(x.dtype).reshape(B, T, D)


def trunk(x, start_pos, s5_re0, s5_im0, pool0, conv0, lw, ffn_w, moe_w):
    new_re, new_im, new_pool, new_conv = [], [], [], []
    for l in range(DEPTH):
        p = {k: v[l] for k, v in lw.items()}
        mix, h_re, h_im, pb, cb = mixer_block(x, start_pos, s5_re0[l], s5_im0[l], pool0[l], conv0[l], p)
        x = layer_norm(ALPHA * x + mix, p['ln1_g'], p['ln1_b'])
        if l % 2 == 0:
            i = l // 2
            f = swiglu(x, ffn_w[0][i], ffn_w[1][i], ffn_w[2][i])
        else:
            i = l // 2
            f = moe_swiglu(x, moe_w[0][i], moe_w[1][i], moe_w[2][i], moe_w[3][i])
        x = layer_norm(ALPHA * x + f, p['ln2_g'], p['ln2_b'])
        new_re.append(h_re)
        new_im.append(h_im)
        new_pool.append(pb)
        new_conv.append(cb)
    return x, jnp.stack(new_re), jnp.stack(new_im), jnp.stack(new_pool), jnp.stack(new_conv)


def setup_inputs(seed: int = 0) -> dict:
    key = jax.random.key(seed)
    ks = iter(jax.random.split(key, 48))
    f32 = jnp.float32

    def nrm(shape, scale):
        return jax.random.normal(next(ks), shape, f32) * scale

    d = {}
    d['x_prompt'] = nrm((BATCH, SEQ, D_MODEL), 1.0)
    d['x_sample'] = nrm((DEC_BATCH, DEC_SEQ, D_MODEL), 1.0)
    d['state_s5_re'] = nrm((DEPTH, DEC_BATCH, S5_GROUPS, S5_STATE), 0.3)
    d['state_s5_im'] = nrm((DEPTH, DEC_BATCH, S5_GROUPS, S5_STATE), 0.3)
    d['cache_pool'] = nrm((DEPTH, DEC_BATCH, POOL_BUF, D_POOL), 1.0)
    d['cache_conv'] = nrm((DEPTH, DEC_BATCH, CONV_BUF, D_CONV), 0.5)
    d['w_in'] = nrm((DEPTH, D_MODEL, D_IN), D_MODEL ** -0.5)
    d['b_in'] = nrm((DEPTH, D_IN), 0.02)
    d['s5_a_re'] = -0.5 + nrm((DEPTH, S5_GROUPS, S5_STATE), 0.01)
    d['s5_a_im'] = math.pi * jnp.arange(S5_STATE, dtype=f32) + nrm((DEPTH, S5_GROUPS, S5_STATE), 0.01)
    d['s5_log_dt'] = jax.random.uniform(next(ks), (DEPTH, S5_GROUPS), f32,
                                        minval=math.log(1e-3), maxval=math.log(1e-1))
    d['s5_b_re'] = nrm((DEPTH, S5_GROUPS, S5_STATE, S5_CH_PER_GROUP), (2.0 * S5_CH_PER_GROUP) ** -0.5)
    d['s5_b_im'] = nrm((DEPTH, S5_GROUPS, S5_STATE, S5_CH_PER_GROUP), (2.0 * S5_CH_PER_GROUP) ** -0.5)
    d['s5_c_re'] = nrm((DEPTH, S5_GROUPS, S5_CH_PER_GROUP, S5_STATE), S5_STATE ** -0.5)
    d['s5_c_im'] = nrm((DEPTH, S5_GROUPS, S5_CH_PER_GROUP, S5_STATE), S5_STATE ** -0.5)
    d['s5_d'] = nrm((DEPTH, S5_GROUPS, S5_CH_PER_GROUP), 1.0)
    d['s5_w_glu'] = nrm((DEPTH, D_S5, D_S5), D_S5 ** -0.5)
    d['s5_b_glu'] = nrm((DEPTH, D_S5), 0.02)
    d['pool_w'] = nrm((DEPTH, len(POOL_WINDOWS), POOL_GROUP, POOL_GROUP), POOL_GROUP ** -0.5)
    d['pool_scale'] = 1.0 + nrm((DEPTH, D_POOL), 0.02)
    d['conv_w'] = nrm((DEPTH, CONV_K, D_CONV), CONV_K ** -0.5)
    d['conv_b'] = nrm((DEPTH, D_CONV), 0.02)
    d['conv_ln_g'] = 1.0 + nrm((DEPTH, D_CONV), 0.02)
    d['conv_ln_b'] = nrm((DEPTH, D_CONV), 0.02)
    d['proj_s5'] = nrm((DEPTH, D_S5, D_MODEL), BETA * D_S5 ** -0.5)
    d['proj_pool'] = nrm((DEPTH, D_POOL, D_MODEL), BETA * D_POOL ** -0.5)
    d['proj_conv'] = nrm((DEPTH, D_CONV, D_MODEL), BETA * D_CONV ** -0.5)
    d['w_out'] = nrm((DEPTH, D_MODEL, D_MODEL), BETA * D_MODEL ** -0.5)
    d['b_out'] = nrm((DEPTH, D_MODEL), 0.02)
    d['ln1_g'] = 1.0 + nrm((DEPTH, D_MODEL), 0.02)
    d['ln1_b'] = nrm((DEPTH, D_MODEL), 0.02)
    d['ln2_g'] = 1.0 + nrm((DEPTH, D_MODEL), 0.02)
    d['ln2_b'] = nrm((DEPTH, D_MODEL), 0.02)
    d['ffn_w_gate'] = nrm((N_DENSE, D_MODEL, D_FF), D_MODEL ** -0.5)
    d['ffn_w_up'] = nrm((N_DENSE, D_MODEL, D_FF), D_MODEL ** -0.5)
    d['ffn_w_down'] = nrm((N_DENSE, D_FF, D_MODEL), BETA * D_FF ** -0.5)
    d['moe_router'] = nrm((N_MOE, D_MODEL, N_EXPERTS), D_MODEL ** -0.5)
    d['moe_w_gate'] = nrm((N_MOE, N_EXPERTS, D_MODEL, D_FF), D_MODEL ** -0.5)
    d['moe_w_up'] = nrm((N_MOE, N_EXPERTS, D_MODEL, D_FF), D_MODEL ** -0.5)
    d['moe_w_down'] = nrm((N_MOE, N_EXPERTS, D_FF, D_MODEL), BETA * D_FF ** -0.5)
    return d


def reference(x_prompt, x_sample, state_s5_re, state_s5_im, cache_pool, cache_conv,
              w_in, b_in, s5_a_re, s5_a_im, s5_log_dt, s5_b_re, s5_b_im, s5_c_re, s5_c_im,
              s5_d, s5_w_glu, s5_b_glu, pool_w, pool_scale, conv_w, conv_b, conv_ln_g, conv_ln_b,
              proj_s5, proj_pool, proj_conv, w_out, b_out, ln1_g, ln1_b, ln2_g, ln2_b,
              ffn_w_gate, ffn_w_up, ffn_w_down, moe_router, moe_w_gate, moe_w_up, moe_w_down):
    lw = dict(w_in=w_in, b_in=b_in, s5_a_re=s5_a_re, s5_a_im=s5_a_im, s5_log_dt=s5_log_dt,
              s5_b_re=s5_b_re, s5_b_im=s5_b_im, s5_c_re=s5_c_re, s5_c_im=s5_c_im, s5_d=s5_d,
              s5_w_glu=s5_w_glu, s5_b_glu=s5_b_glu, pool_w=pool_w, pool_scale=pool_scale,
              conv_w=conv_w, conv_b=conv_b, conv_ln_g=conv_ln_g, conv_ln_b=conv_ln_b,
              proj_s5=proj_s5, proj_pool=proj_pool, proj_conv=proj_conv, w_out=w_out, b_out=b_out,
              ln1_g=ln1_g, ln1_b=ln1_b, ln2_g=ln2_g, ln2_b=ln2_b)
    ffn_w = (ffn_w_gate, ffn_w_up, ffn_w_down)
    moe_w = (moe_router, moe_w_gate, moe_w_up, moe_w_down)
    B = x_prompt.shape[0]
    z_re = jnp.zeros((DEPTH, B, S5_GROUPS, S5_STATE), jnp.float32)
    z_im = jnp.zeros((DEPTH, B, S5_GROUPS, S5_STATE), jnp.float32)
    z_pool = jnp.zeros((DEPTH, B, POOL_BUF, D_POOL), x_prompt.dtype)
    z_conv = jnp.zeros((DEPTH, B, CONV_BUF, D_CONV), x_prompt.dtype)
    y_prompt, s5_re_p, s5_im_p, pool_p, conv_p = trunk(
        x_prompt, 0, z_re, z_im, z_pool, z_conv, lw, ffn_w, moe_w)
    y_sample, s5_re_s, s5_im_s, pool_s, conv_s = trunk(
        x_sample, PAST_LEN, state_s5_re, state_s5_im, cache_pool, cache_conv, lw, ffn_w, moe_w)
    return (y_prompt, y_sample, s5_re_p, s5_im_p, pool_p, conv_p, s5_re_s, s5_im_s, pool_s, conv_s)
```

```python
import functools
import math

import jax
import jax.numpy as jnp
from jax import lax
from jax.experimental import pallas as pl
from jax.experimental.pallas import tpu as pltpu

F32 = jnp.float32
BF16 = jnp.bfloat16

LN_EPS = 1e-5
POOL_WINDOWS = (2, 4, 8, 16)
PAST_LEN = 16384
TOP_K = 2
POOL_HIST = 16
CONV_HIST = 32
ROUTER_LANES = 128
VMEM_LIMIT = 56 * 1024 * 1024


def _sigmoid(x):
    return 0.5 * jnp.tanh(0.5 * x) + 0.5


def _silu(x):
    return x * _sigmoid(x)


def _layer_norm(x, g, b):
    mu = jnp.mean(x, axis=-1, keepdims=True)
    xc = x - mu
    var = jnp.mean(xc * xc, axis=-1, keepdims=True)
    return xc * lax.rsqrt(var + LN_EPS) * g + b


def _dot(a, b):
    return jnp.dot(a, b, preferred_element_type=F32)


def _params(*sem):
    return pltpu.CompilerParams(dimension_semantics=sem, vmem_limit_bytes=VMEM_LIMIT)


def _const_spec(shape):
    return pl.BlockSpec(shape, lambda *_: (0,) * len(shape))


def _s5_disc_kernel(are_ref, aim_ref, ldt_ref, bre_ref, bim_ref, lam_ref, bbre_ref, bbim_ref):
    a_re = are_ref[...]
    a_im = aim_ref[...]
    dt = jnp.exp(ldt_ref[...])
    mag = jnp.exp(a_re * dt)
    lr = mag * jnp.cos(a_im * dt)
    li = mag * jnp.sin(a_im * dt)
    den = a_re * a_re + a_im * a_im
    nr = lr - 1.0
    cr = (nr * a_re + li * a_im) / den
    ci = (li * a_re - nr * a_im) / den
    lam_ref[0:1, :] = lr
    lam_ref[1:2, :] = li
    b_re = bre_ref[...]
    b_im = bim_ref[...]
    bbre_ref[...] = cr * b_re - ci * b_im
    bbim_ref[...] = cr * b_im + ci * b_re


def _s5_discretise(a_re, a_im, log_dt, b_re, b_im):
    n_g, n_p = a_re.shape
    n_h = b_re.shape[-1]
    n_s = n_g * n_p
    ldt = jnp.broadcast_to(log_dt[:, None], (n_g, n_p)).reshape(1, n_s)
    b_re_t = b_re.reshape(n_s, n_h).T
    b_im_t = b_im.reshape(n_s, n_h).T
    return pl.pallas_call(
        _s5_disc_kernel,
        out_shape=(jax.ShapeDtypeStruct((2, n_s), F32),
                   jax.ShapeDtypeStruct((n_h, n_s), F32),
                   jax.ShapeDtypeStruct((n_h, n_s), F32)),
        name="s5_discretise",
    )(a_re.reshape(1, n_s), a_im.reshape(1, n_s), ldt, b_re_t, b_im_t)


def _block_diag(blocks):
    n_g, r, c = blocks.shape
    eye = jnp.eye(n_g, dtype=blocks.dtype)
    return (eye[:, None, :, None] * blocks[:, :, None, :]).reshape(n_g * r, n_g * c)


def _inproj_kernel(x_ref, w_ref, b_ref, o_ref, *, d_lin, d_conv):
    u = _dot(x_ref[...].astype(BF16), w_ref[...]) + b_ref[...]
    o_ref[:, :d_lin] = u[:, :d_lin]
    o_ref[:, d_lin:] = u[:, d_lin:d_lin + d_conv] * _sigmoid(u[:, d_lin + d_conv:])


def _inproj(x, w, b, *, d_lin, d_conv, tm):
    n_g, n_r, d = x.shape
    d_out = d_lin + d_conv
    out = pl.pallas_call(
        functools.partial(_inproj_kernel, d_lin=d_lin, d_conv=d_conv),
        grid=(n_g, n_r // tm),
        in_specs=[pl.BlockSpec((None, tm, d), lambda g, r: (g, r, 0)),
                  _const_spec(w.shape), _const_spec(b.shape)],
        out_specs=pl.BlockSpec((tm, d_out), lambda g, r: (r, g)),
        out_shape=jax.ShapeDtypeStruct((n_r, n_g * d_out), F32),
        compiler_params=_params("parallel", "parallel"),
        name="in_projection",
    )(x, w, b)
    return out.reshape(n_r * n_g, d_out)


def _mixer_kernel(u_ref, h0_ref, pc_ref, cc_ref, wb_ref, wc_ref, lam_ref, d_ref, wglu_ref, bglu_ref,
                  wpool_ref, pscale_ref, cw_ref, cb_ref, clg_ref, clb_ref,
                  y_ref, hT_ref, pnew_ref, cnew_ref,
                  s_scr, h_scr, ph_scr, ch_scr, *, nb, tc, start_pos, d_s5, d_pool, d_conv, n_taps):
    c = pl.program_id(0)
    last = pl.num_programs(0) - 1
    rows = tc * nb
    n_state = lam_ref.shape[1]
    o_pool = d_s5
    o_conv = d_s5 + d_pool

    @pl.when(c == 0)
    def _():
        h_scr[...] = h0_ref[...]
        ph_scr[0:POOL_HIST * nb, :] = pc_ref[...]
        ch_scr[0:CONV_HIST * nb, :] = cc_ref[...]

    u_s5 = u_ref[:, 0:d_s5]
    s_scr[...] = _dot(u_s5.astype(BF16), wb_ref[...])
    if tc == 1:
        lr = lam_ref[0:1, :]
        li = lam_ref[1:2, :]
        h_re = h_scr[:, 0:n_state]
        h_im = h_scr[:, n_state:]
        n_re = lr * h_re - li * h_im + s_scr[:, 0:n_state]
        n_im = lr * h_im + li * h_re + s_scr[:, n_state:]
        s_scr[:, 0:n_state] = n_re
        s_scr[:, n_state:] = n_im
        h_scr[:, 0:n_state] = n_re
        h_scr[:, n_state:] = n_im
    else:
        assert nb == 8
        half = n_state // 2
        for hf in range(2):
            re0 = hf * half
            im0 = n_state + hf * half
            lr = jnp.broadcast_to(lam_ref[0:1, re0:re0 + half], (nb, half))
            li = jnp.broadcast_to(lam_ref[1:2, re0:re0 + half], (nb, half))

            def step(t, carry, re0=re0, im0=im0, lr=lr, li=li):
                h_re, h_im = carry
                r0 = pl.multiple_of(t * nb, nb)
                n_re = lr * h_re - li * h_im + s_scr[pl.ds(r0, nb), re0:re0 + half]
                n_im = lr * h_im + li * h_re + s_scr[pl.ds(r0, nb), im0:im0 + half]
                s_scr[pl.ds(r0, nb), re0:re0 + half] = n_re
                s_scr[pl.ds(r0, nb), im0:im0 + half] = n_im
                return n_re, n_im

            h_re, h_im = lax.fori_loop(0, tc, step, (h_scr[:, re0:re0 + half], h_scr[:, im0:im0 + half]),
                                       unroll=4)
            h_scr[:, re0:re0 + half] = h_re
            h_scr[:, im0:im0 + half] = h_im

    y = _dot(s_scr[...].astype(BF16), wc_ref[...]) + d_ref[...] * u_s5
    y = jax.nn.gelu(y)
    y = y * _sigmoid(_dot(y.astype(BF16), wglu_ref[...]) + bglu_ref[...])
    y_ref[:, 0:d_s5] = y.astype(y_ref.dtype)

    @pl.when(c == last)
    def _():
        hT_ref[...] = h_scr[...]

    u_pool = u_ref[:, o_pool:o_pool + d_pool]
    ph_scr[pl.ds(POOL_HIST * nb, rows), :] = u_pool
    a = ph_scr[...]
    row_t = lax.broadcasted_iota(jnp.int32, (rows, d_pool), 0) >> (nb.bit_length() - 1)
    pos1 = start_pos + c * tc + row_t + 1
    lane = lax.broadcasted_iota(jnp.int32, (rows, d_pool), 1)
    group = d_pool // len(POOL_WINDOWS)
    w = 1
    s_sel = None
    cnt_sel = None
    for gi, win in enumerate(POOL_WINDOWS):
        while w < win:
            a = a[w * nb:, :] + a[:a.shape[0] - w * nb, :]
            w *= 2
        assert w == win
        i0 = (POOL_HIST + 1 - win) * nb
        s_w = a[i0:i0 + rows, :]
        cnt_w = jnp.minimum(pos1, win).astype(F32)
        if s_sel is None:
            s_sel, cnt_sel = s_w, cnt_w
        else:
            in_later = lane >= gi * group
            s_sel = jnp.where(in_later, s_w, s_sel)
            cnt_sel = jnp.where(in_later, cnt_w, cnt_sel)
    pooled = s_sel / cnt_sel - u_pool
    y_pool = _dot(pooled.astype(BF16), wpool_ref[...]) * pscale_ref[...]
    y_ref[:, o_pool:o_pool + d_pool] = y_pool.astype(y_ref.dtype)
    ph_scr[0:POOL_HIST * nb, :] = ph_scr[pl.ds(tc * nb, POOL_HIST * nb), :]

    @pl.when(c == last)
    def _():
        pnew_ref[...] = ph_scr[nb:POOL_HIST * nb, :]

    ch_scr[pl.ds(CONV_HIST * nb, rows), :] = u_ref[:, o_conv:o_conv + d_conv]
    rb = min(rows, 64)
    tap0 = CONV_HIST - (n_taps - 1)

    def conv_block(i, _):
        base = pl.multiple_of(i * rb, rb)
        acc = jnp.zeros((rb, d_conv), F32)
        for k in range(n_taps):
            acc = acc + cw_ref[k:k + 1, :] * ch_scr[pl.ds(base + (tap0 + k) * nb, rb), :]
        yc = _silu(_layer_norm(acc + cb_ref[...], clg_ref[...], clb_ref[...]))
        y_ref[pl.ds(base, rb), o_conv:o_conv + d_conv] = yc.astype(y_ref.dtype)
        return 0

    lax.fori_loop(0, rows // rb, conv_block, 0)
    ch_scr[0:CONV_HIST * nb, :] = ch_scr[pl.ds(tc * nb, CONV_HIST * nb), :]

    @pl.when(c == last)
    def _():
        cnew_ref[...] = ch_scr[(CONV_HIST - (n_taps - 1)) * nb:CONV_HIST * nb, :]


def _mixers(u, h0, pool_cache, conv_cache, mw, *, nb, tc, start_pos):
    n_rows, d_u = u.shape
    n_t = n_rows // nb
    rows = tc * nb
    d_s5 = mw["wglu"].shape[0]
    d_pool = mw["wpool"].shape[0]
    d_conv = mw["conv_w"].shape[1]
    n_taps = mw["n_taps"]
    d_y = d_s5 + d_pool + d_conv
    n_state2 = h0.shape[1]
    weights = [mw["wb"], mw["wc"], mw["lam"], mw["d"], mw["wglu"], mw["bglu"], mw["wpool"], mw["pscale"],
               mw["conv_w"], mw["conv_b"], mw["conv_ln_g"], mw["conv_ln_b"]]
    kern = functools.partial(_mixer_kernel, nb=nb, tc=tc, start_pos=start_pos, d_s5=d_s5, d_pool=d_pool,
                             d_conv=d_conv, n_taps=n_taps)
    return pl.pallas_call(
        kern,
        grid=(n_t // tc,),
        in_specs=[pl.BlockSpec((rows, d_u), lambda c: (c, 0)),
                  _const_spec(h0.shape), _const_spec(pool_cache.shape), _const_spec(conv_cache.shape)]
                 + [_const_spec(w.shape) for w in weights],
        out_specs=(pl.BlockSpec((rows, d_y), lambda c: (c, 0)),
                   _const_spec(h0.shape),
                   _const_spec(((POOL_HIST - 1) * nb, d_pool)),
                   _const_spec(((n_taps - 1) * nb, d_conv))),
        out_shape=(jax.ShapeDtypeStruct((n_rows, d_y), BF16),
                   jax.ShapeDtypeStruct(h0.shape, F32),
                   jax.ShapeDtypeStruct(((POOL_HIST - 1) * nb, d_pool), F32),
                   jax.ShapeDtypeStruct(((n_taps - 1) * nb, d_conv), F32)),
        scratch_shapes=[pltpu.VMEM((rows, n_state2), F32),
                        pltpu.VMEM((nb, n_state2), F32),
                        pltpu.VMEM(((POOL_HIST + tc) * nb, d_pool), F32),
                        pltpu.VMEM(((CONV_HIST + tc) * nb, d_conv), F32)],
        compiler_params=_params("arbitrary"),
        name="token_mixers",
    )(u, h0, pool_cache, conv_cache, *weights)


def _merge_kernel(*refs, alpha, d_s5, d_pool, d_conv, with_router, n_exp):
    if with_router:
        (x_ref, y_ref, wg_ref, bg_ref, ps5_ref, ppool_ref, pconv_ref, wo_ref, bo_ref, g_ref, b_ref, rt_ref,
         o_ref, ob_ref, ri_ref, rw_ref) = refs
    else:
        (x_ref, y_ref, wg_ref, bg_ref, ps5_ref, ppool_ref, pconv_ref, wo_ref, bo_ref, g_ref, b_ref,
         o_ref) = refs
    x = x_ref[...]
    xb = x.astype(BF16)
    d = x.shape[-1]
    merged = None
    col = 0
    for j, (width, p_ref) in enumerate(((d_s5, ps5_ref), (d_pool, ppool_ref), (d_conv, pconv_ref))):
        gate = _sigmoid(_dot(xb, wg_ref[:, j * d:(j + 1) * d]) + bg_ref[:, j * d:(j + 1) * d])
        term = gate * _dot(y_ref[:, col:col + width], p_ref[...])
        merged = term if merged is None else merged + term
        col += width
    out = _dot(merged.astype(BF16), wo_ref[...]) + bo_ref[...]
    x1 = _layer_norm(alpha * x + out, g_ref[...], b_ref[...])
    o_ref[...] = x1
    if with_router:
        x_hi = x1.astype(BF16)
        ob_ref[...] = x_hi
        x_lo = (x1 - x_hi.astype(F32)).astype(BF16)
        r = _dot(x_hi, rt_ref[...]) + _dot(x_lo, rt_ref[...])
        logits = r + pltpu.roll(r, ROUTER_LANES // 2, axis=1)
        lane = lax.broadcasted_iota(jnp.int32, logits.shape, 1).astype(F32)
        valid = lane < n_exp
        neg = jnp.float32(-jnp.inf)
        l1 = jnp.where(valid, logits, neg)
        m1 = jnp.max(l1, axis=-1, keepdims=True)
        i1 = jnp.min(jnp.where(l1 == m1, lane, float(ROUTER_LANES)), axis=-1, keepdims=True)
        l2 = jnp.where(lane == i1, neg, l1)
        m2 = jnp.max(l2, axis=-1, keepdims=True)
        i2 = jnp.min(jnp.where(l2 == m2, lane, float(ROUTER_LANES)), axis=-1, keepdims=True)
        e2 = jnp.exp(m2 - m1)
        w1 = 1.0 / (1.0 + e2)
        w2 = e2 * w1
        ri_ref[...] = jnp.where(lane == 0.0, i1, jnp.where(lane == 1.0, i2, 0.0)).astype(jnp.int32)
        rw_ref[...] = jnp.where(lane == 0.0, w1, jnp.where(lane == 1.0, w2, 0.0))


def _merge(x, y, lw, *, alpha, tm, router=None, n_exp=0):
    n_g, n_r, d = x.shape
    d_y = y.shape[1]
    y2 = y.reshape(n_r, n_g * d_y)
    d_s5, d_pool, d_conv = lw["proj_s5"].shape[0], lw["proj_pool"].shape[0], lw["proj_conv"].shape[0]
    weights = [lw["w_gates"], lw["b_gates"], lw["proj_s5"], lw["proj_pool"], lw["proj_conv"], lw["w_out"],
               lw["b_out"], lw["ln1_g"], lw["ln1_b"]]
    tok_spec = pl.BlockSpec((None, tm, d), lambda g, r: (g, r, 0))
    out_specs = [tok_spec]
    out_shape = [jax.ShapeDtypeStruct((n_g, n_r, d), F32)]
    if router is not None:
        weights.append(router)
        lane_spec = pl.BlockSpec((None, tm, ROUTER_LANES), lambda g, r: (g, r, 0))
        out_specs += [tok_spec, lane_spec, lane_spec]
        out_shape += [jax.ShapeDtypeStruct((n_g, n_r, d), BF16),
                      jax.ShapeDtypeStruct((n_g, n_r, ROUTER_LANES), jnp.int32),
                      jax.ShapeDtypeStruct((n_g, n_r, ROUTER_LANES), F32)]
    kern = functools.partial(_merge_kernel, alpha=alpha, d_s5=d_s5, d_pool=d_pool, d_conv=d_conv,
                             with_router=router is not None, n_exp=n_exp)
    return pl.pallas_call(
        kern,
        grid=(n_g, n_r // tm),
        in_specs=[tok_spec, pl.BlockSpec((tm, d_y), lambda g, r: (r, g))] + [_const_spec(w.shape) for w in weights],
        out_specs=tuple(out_specs),
        out_shape=tuple(out_shape),
        compiler_params=_params("parallel", "parallel"),
        name="merge_outproj_ln1",
    )(x, y2, *weights)


def _swiglu_tile(xb, wg_ref, wu_ref, wd_ref, ff_chunk):
    d_ff = wg_ref.shape[-1]
    acc = None
    for c0 in range(0, d_ff, ff_chunk):
        g = _dot(xb, wg_ref[:, c0:c0 + ff_chunk])
        u = _dot(xb, wu_ref[:, c0:c0 + ff_chunk])
        part = _dot((_silu(g) * u).astype(BF16), wd_ref[c0:c0 + ff_chunk, :])
        acc = part if acc is None else acc + part
    return acc


def _ffn_kernel(x_ref, wg_ref, wu_ref, wd_ref, g_ref, b_ref, o_ref, *, alpha, ff_chunk):
    x = x_ref[...]
    f = _swiglu_tile(x.astype(BF16), wg_ref, wu_ref, wd_ref, ff_chunk)
    o_ref[...] = _layer_norm(alpha * x + f, g_ref[...], b_ref[...])


def _ffn_dense(x, wg, wu, wd, ln_g, ln_b, *, alpha, tm, ff_chunk):
    n, d = x.shape
    weights = [wg, wu, wd, ln_g, ln_b]
    return pl.pallas_call(
        functools.partial(_ffn_kernel, alpha=alpha, ff_chunk=ff_chunk),
        grid=(n // tm,),
        in_specs=[pl.BlockSpec((tm, d), lambda i: (i, 0))] + [_const_spec(w.shape) for w in weights],
        out_specs=pl.BlockSpec((tm, d), lambda i: (i, 0)),
        out_shape=jax.ShapeDtypeStruct((n, d), F32),
        compiler_params=_params("parallel"),
        name="ffn_dense_ln2",
    )(x, *weights)


def _moe_kernel(te_ref, nt_ref, xs_ref, ws_ref, wg_ref, wu_ref, wd_ref, o_ref, *, ff_chunk):
    i = pl.program_id(0)

    @pl.when(i < nt_ref[0])
    def _():
        o_ref[...] = ws_ref[...] * _swiglu_tile(xs_ref[...], wg_ref, wu_ref, wd_ref, ff_chunk)

    @pl.when(i >= nt_ref[0])
    def _():
        o_ref[...] = jnp.zeros(o_ref.shape, o_ref.dtype)


def _moe_grouped(tile_expert, n_tiles_used, xs, ws, wg, wu, wd, *, tm, ff_chunk):
    n_slots, d = xs.shape
    d_ff = wg.shape[-1]
    grid_spec = pltpu.PrefetchScalarGridSpec(
        num_scalar_prefetch=2,
        grid=(n_slots // tm,),
        in_specs=[pl.BlockSpec((tm, d), lambda i, te, nt: (i, 0)),
                  pl.BlockSpec((tm, 1), lambda i, te, nt: (i, 0)),
                  pl.BlockSpec((None, d, d_ff), lambda i, te, nt: (te[i], 0, 0)),
                  pl.BlockSpec((None, d, d_ff), lambda i, te, nt: (te[i], 0, 0)),
                  pl.BlockSpec((None, d_ff, d), lambda i, te, nt: (te[i], 0, 0))],
        out_specs=pl.BlockSpec((tm, d), lambda i, te, nt: (i, 0)),
    )
    return pl.pallas_call(
        functools.partial(_moe_kernel, ff_chunk=ff_chunk),
        grid_spec=grid_spec,
        out_shape=jax.ShapeDtypeStruct((n_slots, d), F32),
        compiler_params=_params("arbitrary"),
        name="moe_grouped_swiglu",
    )(tile_expert, n_tiles_used, xs, ws, wg, wu, wd)


def _combine_kernel(x_ref, ya_ref, yb_ref, g_ref, b_ref, o_ref, *, alpha):
    f = ya_ref[...] + yb_ref[...]
    o_ref[...] = _layer_norm(alpha * x_ref[...] + f, g_ref[...], b_ref[...])


def _moe_combine(x, ya, yb, ln_g, ln_b, *, alpha, tm):
    n, d = x.shape
    tok = pl.BlockSpec((tm, d), lambda i: (i, 0))
    return pl.pallas_call(
        functools.partial(_combine_kernel, alpha=alpha),
        grid=(n // tm,),
        in_specs=[tok, tok, tok, _const_spec(ln_g.shape), _const_spec(ln_b.shape)],
        out_specs=tok,
        out_shape=jax.ShapeDtypeStruct((n, d), F32),
        compiler_params=_params("parallel"),
        name="moe_combine_ln2",
    )(x, ya, yb, ln_g, ln_b)


def _moe_route(top_i, top_w, n_experts, tm):
    n = top_i.shape[0]
    flat_e = top_i.reshape(-1)
    onehot = (flat_e[:, None] == jnp.arange(n_experts, dtype=jnp.int32)[None, :]).astype(jnp.int32)
    csum = jnp.cumsum(onehot, axis=0)
    counts = csum[-1]
    rank = jnp.sum((csum - 1) * onehot, axis=1)
    padded = ((counts + tm - 1) // tm) * tm
    off_end = jnp.cumsum(padded)
    off = off_end - padded
    slot = off[flat_e] + rank
    n_slots = TOP_K * n + n_experts * tm
    n_slots = ((n_slots + tm - 1) // tm) * tm
    tok = jnp.arange(TOP_K * n, dtype=jnp.int32) // TOP_K
    slot_tok = jnp.zeros((n_slots,), jnp.int32).at[slot].set(tok)
    slot_w = jnp.zeros((n_slots,), F32).at[slot].set(top_w.reshape(-1))
    tile_start = jnp.arange(n_slots // tm, dtype=jnp.int32) * tm
    tile_expert = jnp.sum((tile_start[:, None] >= off_end[None, :]).astype(jnp.int32), axis=1)
    tile_expert = jnp.minimum(tile_expert, n_experts - 1).astype(jnp.int32)
    n_used = (off_end[-1] // tm).astype(jnp.int32).reshape(1)
    return slot_tok, slot_w, tile_expert, n_used, slot.reshape(n, TOP_K)


def _pad_time_major(cache, steps):
    n_b, n_s, n_c = cache.shape
    tmaj = jnp.transpose(cache, (1, 0, 2))
    tmaj = jnp.pad(tmaj, ((steps - n_s, 0), (0, 0), (0, 0)))
    return tmaj.reshape(steps * n_b, n_c)


def _from_time_major(flat, n_b):
    n_c = flat.shape[1]
    return jnp.transpose(flat.reshape(-1, n_b, n_c), (1, 0, 2))


def kernel(x_prompt, x_sample, state_s5_re, state_s5_im, cache_pool, cache_conv, w_in, b_in, s5_a_re, s5_a_im, s5_log_dt, s5_b_re, s5_b_im, s5_c_re, s5_c_im, s5_d, s5_w_glu, s5_b_glu, pool_w, pool_scale, conv_w, conv_b, conv_ln_g, conv_ln_b, proj_s5, proj_pool, proj_conv, w_out, b_out, ln1_g, ln1_b, ln2_g, ln2_b, ffn_w_gate, ffn_w_up, ffn_w_down, moe_router, moe_w_gate, moe_w_up, moe_w_down):
    depth, d_model, _ = w_in.shape
    n_bp, n_tp, _ = x_prompt.shape
    n_bs, n_ts, _ = x_sample.shape
    assert n_ts == 1
    n_grp, n_p = s5_a_re.shape[1:]
    n_h = s5_b_re.shape[-1]
    n_state = n_grp * n_p
    d_s5 = n_grp * n_h
    d_pool = pool_scale.shape[1]
    d_conv = conv_w.shape[2]
    n_taps = conv_w.shape[1]
    d_lin = d_s5 + d_pool
    d_u = d_lin + 2 * d_conv
    d_ff = ffn_w_gate.shape[-1]
    n_exp = moe_router.shape[-1]
    alpha = (2.0 * depth) ** 0.25
    ff_chunk = 256
    d_ff_pad = ((d_ff + ff_chunk - 1) // ff_chunk) * ff_chunk
    assert cache_pool.shape[2] == POOL_HIST - 1 and n_taps - 1 <= CONV_HIST

    def row(v):
        return v.reshape(1, -1).astype(F32)

    def pad_ff(w, axis):
        pads = [(0, 0)] * w.ndim
        pads[axis] = (0, d_ff_pad - d_ff)
        return jnp.pad(w.astype(BF16), pads)

    xp = x_prompt
    xs = x_sample.reshape(1, n_bs, d_model)
    zeros_p = dict(h0=jnp.zeros((n_bp, 2 * n_state), F32),
                   pool=jnp.zeros((POOL_HIST * n_bp, d_pool), F32),
                   conv=jnp.zeros((CONV_HIST * n_bp, d_conv), F32))
    outs = {k: [] for k in ("re_p", "im_p", "pool_p", "conv_p", "re_s", "im_s", "pool_s", "conv_s")}

    for l in range(depth):
        lam, bb_re, bb_im = _s5_discretise(s5_a_re[l], s5_a_im[l], s5_log_dt[l], s5_b_re[l], s5_b_im[l])
        bb = jnp.concatenate([bb_re, bb_im], axis=0).reshape(2, n_h, n_grp, n_p)
        wb = jnp.concatenate([_block_diag(jnp.transpose(bb[0], (1, 0, 2))),
                              _block_diag(jnp.transpose(bb[1], (1, 0, 2)))], axis=1).astype(BF16)
        wc = jnp.concatenate([_block_diag(jnp.transpose(s5_c_re[l], (0, 2, 1))),
                              _block_diag(jnp.transpose(-s5_c_im[l], (0, 2, 1)))], axis=0).astype(BF16)
        conv_w_pad = jnp.pad(conv_w[l].astype(F32), ((0, CONV_HIST - n_taps), (0, 0)))
        mw = dict(wb=wb, wc=wc, lam=lam, d=row(s5_d[l]), wglu=s5_w_glu[l].astype(BF16), bglu=row(s5_b_glu[l]),
                  wpool=_block_diag(pool_w[l]).astype(BF16), pscale=row(pool_scale[l]),
                  conv_w=conv_w_pad, conv_b=row(conv_b[l]), conv_ln_g=row(conv_ln_g[l]),
                  conv_ln_b=row(conv_ln_b[l]), n_taps=n_taps)
        w1 = w_in[l, :, :d_u].astype(BF16)
        b1 = row(b_in[l, :d_u])
        lw = dict(w_gates=w_in[l, :, d_u:].astype(BF16), b_gates=row(b_in[l, d_u:]),
                  proj_s5=proj_s5[l].astype(BF16), proj_pool=proj_pool[l].astype(BF16),
                  proj_conv=proj_conv[l].astype(BF16), w_out=w_out[l].astype(BF16), b_out=row(b_out[l]),
                  ln1_g=row(ln1_g[l]), ln1_b=row(ln1_b[l]))
        is_moe = l % 2 == 1
        router = None
        if is_moe:
            r32 = moe_router[l // 2].astype(F32)
            r_hi = r32.astype(BF16)
            r_lo = (r32 - r_hi.astype(F32)).astype(BF16)
            half = ROUTER_LANES // 2
            router = jnp.concatenate([jnp.pad(r_hi, ((0, 0), (0, half - n_exp))),
                                      jnp.pad(r_lo, ((0, 0), (0, half - n_exp)))], axis=1)

        tm_p = min(512, n_tp)
        up = _inproj(xp, w1, b1, d_lin=d_lin, d_conv=d_conv, tm=tm_p)
        yp, hT_p, pool_p, conv_p = _mixers(up, zeros_p["h0"], zeros_p["pool"], zeros_p["conv"], mw,
                                           nb=n_bp, tc=64, start_pos=0)
        res_p = _merge(xp, yp, lw, alpha=alpha, tm=tm_p, router=router, n_exp=n_exp)
        h0_s = jnp.concatenate([state_s5_re[l].reshape(n_bs, n_state), state_s5_im[l].reshape(n_bs, n_state)],
                               axis=1).astype(F32)
        us = _inproj(xs, w1, b1, d_lin=d_lin, d_conv=d_conv, tm=n_bs)
        ys, hT_s, pool_s, conv_s = _mixers(us, h0_s, _pad_time_major(cache_pool[l].astype(F32), POOL_HIST),
                                           _pad_time_major(cache_conv[l].astype(F32), CONV_HIST), mw,
                                           nb=n_bs, tc=1, start_pos=PAST_LEN)
        res_s = _merge(xs, ys, lw, alpha=alpha, tm=n_bs, router=router, n_exp=n_exp)

        outs["re_p"].append(hT_p[:, :n_state].reshape(n_bp, n_grp, n_p))
        outs["im_p"].append(hT_p[:, n_state:].reshape(n_bp, n_grp, n_p))
        outs["pool_p"].append(_from_time_major(pool_p, n_bp))
        outs["conv_p"].append(_from_time_major(conv_p, n_bp))
        outs["re_s"].append(hT_s[:, :n_state].reshape(n_bs, n_grp, n_p))
        outs["im_s"].append(hT_s[:, n_state:].reshape(n_bs, n_grp, n_p))
        outs["pool_s"].append(_from_time_major(pool_s, n_bs))
        outs["conv_s"].append(_from_time_major(conv_s, n_bs))

        g2, b2 = row(ln2_g[l]), row(ln2_b[l])
        n_p_tok = n_bp * n_tp
        if not is_moe:
            i = l // 2
            wg, wu, wd = pad_ff(ffn_w_gate[i], 1), pad_ff(ffn_w_up[i], 1), pad_ff(ffn_w_down[i], 0)
            x1p = res_p[0] if isinstance(res_p, (tuple, list)) else res_p
            x1s = res_s[0] if isinstance(res_s, (tuple, list)) else res_s
            xp = _ffn_dense(x1p.reshape(n_p_tok, d_model), wg, wu, wd, g2, b2, alpha=alpha, tm=tm_p,
                            ff_chunk=ff_chunk).reshape(n_bp, n_tp, d_model)
            xs = _ffn_dense(x1s.reshape(n_bs, d_model), wg, wu, wd, g2, b2, alpha=alpha, tm=n_bs,
                            ff_chunk=ff_chunk).reshape(1, n_bs, d_model)
        else:
            i = l // 2
            wg, wu, wd = pad_ff(moe_w_gate[i], 2), pad_ff(moe_w_up[i], 2), pad_ff(moe_w_down[i], 1)
            x1p, x1p_b, ri_p, rw_p = res_p
            x1s, x1s_b, ri_s, rw_s = res_s
            n_tok = n_p_tok + n_bs
            x1 = jnp.concatenate([x1p.reshape(n_p_tok, d_model), x1s.reshape(n_bs, d_model)], axis=0)
            x1b = jnp.concatenate([x1p_b.reshape(n_p_tok, d_model), x1s_b.reshape(n_bs, d_model)], axis=0)
            top_i = jnp.concatenate([ri_p.reshape(n_p_tok, ROUTER_LANES)[:, :TOP_K],
                                     ri_s.reshape(n_bs, ROUTER_LANES)[:, :TOP_K]], axis=0)
            top_w = jnp.concatenate([rw_p.reshape(n_p_tok, ROUTER_LANES)[:, :TOP_K],
                                     rw_s.reshape(n_bs, ROUTER_LANES)[:, :TOP_K]], axis=0)
            tm_moe = 256
            slot_tok, slot_w, tile_expert, n_used, slots = _moe_route(top_i, top_w, n_exp, tm_moe)
            x_sorted = jnp.take(x1b, slot_tok, axis=0)
            y_sorted = _moe_grouped(tile_expert, n_used, x_sorted, slot_w[:, None], wg, wu, wd,
                                    tm=tm_moe, ff_chunk=ff_chunk)
            ya = jnp.take(y_sorted, slots[:, 0], axis=0)
            yb = jnp.take(y_sorted, slots[:, 1], axis=0)
            tm_c = math.gcd(n_tok, 512)
            x2 = _moe_combine(x1, ya, yb, g2, b2, alpha=alpha, tm=tm_c)
            xp = x2[:n_p_tok].reshape(n_bp, n_tp, d_model)
            xs = x2[n_p_tok:].reshape(1, n_bs, d_model)

    st = lambda k: jnp.stack(outs[k])
    return (xp, xs.reshape(n_bs, 1, d_model), st("re_p"), st("im_p"), st("pool_p"), st("conv_p"),
            st("re_s"), st("im_s"), st("pool_s"), st("conv_s"))
```

```python
import functools

import jax
import jax.numpy as jnp
from jax import lax
from jax.experimental import pallas as pl
from jax.experimental.pallas import tpu as pltpu

F32 = jnp.float32
BF16 = jnp.bfloat16

LN_EPS = 1e-5
POOL_WINDOWS = (2, 4, 8, 16)
PAST_LEN = 16384
TOP_K = 2
POOL_HIST = 16
CONV_HIST = 32
ROUTER_LANES = 128
LANES = 128
MXU_DIM = 256
VMEM_LIMIT = 56 * 1024 * 1024


def _sigmoid(x):
    return 0.5 * jnp.tanh(0.5 * x) + 0.5


def _silu(x):
    return x * _sigmoid(x)


def _layer_norm(x, g, b):
    mu = jnp.mean(x, axis=-1, keepdims=True)
    xc = x - mu
    var = jnp.mean(xc * xc, axis=-1, keepdims=True)
    return xc * lax.rsqrt(var + LN_EPS) * g + b


def _dot(a, b):
    return jnp.dot(a, b, preferred_element_type=F32)


def _split(v):
    hi = v.astype(BF16)
    return hi, (v - hi.astype(F32)).astype(BF16)


def _mm(a, w, precise):
    if not precise:
        return _dot(a.astype(BF16), w)
    a_hi, a_lo = _split(a)
    w_hi, w_lo = _split(w)
    m = a.shape[0]
    r = _dot(jnp.concatenate([a_hi, a_lo], axis=0), w_hi)
    return r[:m] + r[m:] + _dot(a_hi, w_lo)


def _params(*sem):
    return pltpu.CompilerParams(dimension_semantics=sem, vmem_limit_bytes=VMEM_LIMIT)


def _const_spec(shape):
    return pl.BlockSpec(shape, lambda *_: (0,) * len(shape), pipeline_mode=pl.Buffered(1))


def _s5_disc_kernel(are_ref, aim_ref, ldt_ref, bre_ref, bim_ref, lam_ref, bbre_ref, bbim_ref):
    a_re = are_ref[...]
    a_im = aim_ref[...]
    dt = jnp.exp(ldt_ref[...])
    mag = jnp.exp(a_re * dt)
    lr = mag * jnp.cos(a_im * dt)
    li = mag * jnp.sin(a_im * dt)
    den = a_re * a_re + a_im * a_im
    nr = lr - 1.0
    cr = (nr * a_re + li * a_im) / den
    ci = (li * a_re - nr * a_im) / den
    lam_ref[0:1, :] = lr
    lam_ref[1:2, :] = li
    b_re = bre_ref[...]
    b_im = bim_ref[...]
    bbre_ref[...] = cr * b_re - ci * b_im
    bbim_ref[...] = cr * b_im + ci * b_re


def _s5_discretise(a_re, a_im, log_dt, b_re, b_im):
    n_g, n_p = a_re.shape
    n_h = b_re.shape[-1]
    n_s = n_g * n_p
    ldt = jnp.broadcast_to(log_dt[:, None], (n_g, n_p)).reshape(1, n_s)
    b_re_t = b_re.reshape(n_s, n_h).T
    b_im_t = b_im.reshape(n_s, n_h).T
    return pl.pallas_call(
        _s5_disc_kernel,
        out_shape=(jax.ShapeDtypeStruct((2, n_s), F32),
                   jax.ShapeDtypeStruct((n_h, n_s), F32),
                   jax.ShapeDtypeStruct((n_h, n_s), F32)),
        name="s5_discretise",
    )(a_re.reshape(1, n_s), a_im.reshape(1, n_s), ldt, b_re_t, b_im_t)


def _block_diag(blocks):
    n_g, r, c = blocks.shape
    eye = jnp.eye(n_g, dtype=blocks.dtype)
    return (eye[:, None, :, None] * blocks[:, :, None, :]).reshape(n_g * r, n_g * c)


def _s5_input_blocks(bb_re, bb_im, n_grp, n_p):
    n_h = bb_re.shape[0]
    gpb = MXU_DIM // n_p
    ch = gpb * n_h
    assert MXU_DIM % n_p == 0 and LANES % ch == 0 and n_grp % gpb == 0
    n_blk = n_grp // gpb
    per_chunk = LANES // ch

    def blocks(bb):
        g = jnp.transpose(bb.reshape(n_h, n_grp, n_p), (1, 0, 2)).reshape(n_blk, gpb, n_h, n_p)
        dense = jax.vmap(_block_diag)(g)
        return jnp.stack([jnp.pad(dense[j], ((ch * (j % per_chunk), LANES - ch * (j % per_chunk + 1)), (0, 0)))
                          for j in range(n_blk)])

    return jnp.concatenate([blocks(bb_re), blocks(bb_im)], axis=2)


def _s5_output_blocks(c_re, c_im, n_p):
    n_grp, n_h, _ = c_re.shape
    gpb = MXU_DIM // n_h
    assert n_grp % gpb == 0
    n_blk = n_grp // gpb

    def blocks(cm):
        g = jnp.transpose(cm, (0, 2, 1)).reshape(n_blk, gpb, n_p, n_h)
        return jax.vmap(_block_diag)(g)

    return jnp.concatenate([blocks(c_re), blocks(-c_im)], axis=1)


def _inproj_kernel(x_ref, w_ref, b_ref, o_ref, *, d_lin, d_conv, precise):
    u = _mm(x_ref[...], w_ref[...], precise) + b_ref[...]
    o_ref[:, :d_lin] = u[:, :d_lin]
    o_ref[:, d_lin:] = u[:, d_lin:d_lin + d_conv] * _sigmoid(u[:, d_lin + d_conv:])


def _inproj(x, w, b, *, d_lin, d_conv, tm, precise):
    n_g, n_r, d = x.shape
    d_out = d_lin + d_conv
    return pl.pallas_call(
        functools.partial(_inproj_kernel, d_lin=d_lin, d_conv=d_conv, precise=precise),
        grid=(n_g, n_r // tm),
        in_specs=[pl.BlockSpec((None, tm, d), lambda g, r: (g, r, 0)),
                  _const_spec(w.shape), _const_spec(b.shape)],
        out_specs=pl.BlockSpec((None, tm, d_out), lambda g, r: (g, r, 0)),
        out_shape=jax.ShapeDtypeStruct((n_g, n_r, d_out), F32),
        compiler_params=_params("parallel", "parallel"),
        name="in_projection",
    )(x, w, b)


def _mixer_kernel(u_ref, h0_ref, pc_ref, cc_ref, wb_ref, wc_ref, lam_ref, d_ref, wglu_ref, bglu_ref,
                  wpool_ref, pscale_ref, cw_ref, cb_ref, clg_ref, clb_ref,
                  y_ref, hT_ref, pnew_ref, cnew_ref,
                  s_scr, h_scr, ph_scr, ch_scr, yt_scr, *, nb, tc, start_pos, d_s5, d_pool, d_conv, n_taps,
                  precise):
    c = pl.program_id(0)
    last = pl.num_programs(0) - 1
    rows = tc * nb
    n_state = lam_ref.shape[1]
    o_pool = d_s5
    o_conv = d_s5 + d_pool
    n_g, tr = u_ref.shape[0], u_ref.shape[1]

    @pl.when(c == 0)
    def _():
        h_scr[...] = h0_ref[...]
        ph_scr[0:POOL_HIST * nb, :] = pc_ref[...]
        ch_scr[0:CONV_HIST * nb, :] = cc_ref[...]

    u_s5 = u_ref[:, :, 0:d_s5].reshape(rows, d_s5)
    n_in_blk = wb_ref.shape[0]
    per_chunk = LANES // (d_s5 // n_in_blk)
    n_lt = n_state // LANES
    tpb = MXU_DIM // LANES
    for j in range(n_in_blk):
        k0 = (j // per_chunk) * LANES
        bu = _mm(u_s5[:, k0:k0 + LANES], wb_ref[j], precise)
        for i in range(tpb):
            s_scr[j * tpb + i] = bu[:, i * LANES:(i + 1) * LANES]
            s_scr[n_lt + j * tpb + i] = bu[:, MXU_DIM + i * LANES:MXU_DIM + (i + 1) * LANES]
    if tc == 1:
        for k in range(n_lt):
            ln = slice(k * LANES, (k + 1) * LANES)
            li_n = slice(n_state + k * LANES, n_state + (k + 1) * LANES)
            lr, li = lam_ref[0:1, ln], lam_ref[1:2, ln]
            h_re, h_im = h_scr[:, ln], h_scr[:, li_n]
            n_re = lr * h_re - li * h_im + s_scr[k]
            n_im = lr * h_im + li * h_re + s_scr[n_lt + k]
            s_scr[k] = n_re
            s_scr[n_lt + k] = n_im
            h_scr[:, ln] = n_re
            h_scr[:, li_n] = n_im
    else:
        assert nb == 8
        n_part = 2
        lt_pp = n_lt // n_part
        half = lt_pp * LANES
        for hf in range(n_part):
            re0 = hf * half
            im0 = n_state + hf * half
            lr = jnp.broadcast_to(lam_ref[0:1, re0:re0 + half], (nb, half))
            li = jnp.broadcast_to(lam_ref[1:2, re0:re0 + half], (nb, half))

            def step(t, carry, k_re=hf * lt_pp, lr=lr, li=li):
                h_re, h_im = carry
                at_t = pl.ds(t, nb, stride=tc)
                bu_re = jnp.concatenate([s_scr[k_re + i, at_t, :] for i in range(lt_pp)], axis=1)
                bu_im = jnp.concatenate([s_scr[n_lt + k_re + i, at_t, :] for i in range(lt_pp)], axis=1)
                n_re = lr * h_re - li * h_im + bu_re
                n_im = lr * h_im + li * h_re + bu_im
                for i in range(lt_pp):
                    s_scr[k_re + i, at_t, :] = n_re[:, i * LANES:(i + 1) * LANES]
                    s_scr[n_lt + k_re + i, at_t, :] = n_im[:, i * LANES:(i + 1) * LANES]
                return n_re, n_im

            h_re, h_im = lax.fori_loop(0, tc, step, (h_scr[:, re0:re0 + half], h_scr[:, im0:im0 + half]),
                                       unroll=4)
            h_scr[:, re0:re0 + half] = h_re
            h_scr[:, im0:im0 + half] = h_im

    n_out_blk = wc_ref.shape[0]
    spb = n_state // n_out_blk
    lt_pb = spb // LANES
    ys = []
    for j in range(n_out_blk):
        h_re_b = jnp.concatenate([s_scr[j * lt_pb + i] for i in range(lt_pb)], axis=1)
        h_im_b = jnp.concatenate([s_scr[n_lt + j * lt_pb + i] for i in range(lt_pb)], axis=1)
        ys.append(_mm(h_re_b, wc_ref[j, 0:spb, :], precise) + _mm(h_im_b, wc_ref[j, spb:, :], precise))
    y = jnp.concatenate(ys, axis=1) + d_ref[...] * u_s5
    y = jax.nn.gelu(y)
    y = y * _sigmoid(_mm(y, wglu_ref[...], precise) + bglu_ref[...])
    y_ref[:, :, 0:d_s5] = y.reshape(n_g, tr, d_s5).astype(y_ref.dtype)

    @pl.when(c == last)
    def _():
        hT_ref[...] = h_scr[...]

    n_pt = d_pool // LANES
    n_yt = (d_pool + d_conv) // LANES
    if tc == 1:
        ph_scr[pl.ds(POOL_HIST * nb, rows), :] = u_ref[0, :, o_pool:o_pool + d_pool]
        ch_scr[pl.ds(CONV_HIST * nb, rows), :] = u_ref[0, :, o_conv:o_conv + d_conv]
    else:
        for b in range(nb):
            for k in range(n_yt):
                yt_scr[k, b * tc:(b + 1) * tc, :] = u_ref[b, :, o_pool + k * LANES:o_pool + (k + 1) * LANES]

        def stage(t, _):
            at_t = pl.ds(t, nb, stride=tc)
            r0 = pl.multiple_of(t * nb, nb)
            ph_scr[pl.ds(POOL_HIST * nb + r0, nb), :] = jnp.concatenate(
                [yt_scr[k, at_t, :] for k in range(n_pt)], axis=1)
            ch_scr[pl.ds(CONV_HIST * nb + r0, nb), :] = jnp.concatenate(
                [yt_scr[k, at_t, :] for k in range(n_pt, n_yt)], axis=1)
            return 0

        lax.fori_loop(0, tc, stage, 0, unroll=8)

    a = ph_scr[...]
    u_pool = a[POOL_HIST * nb:, :]
    row_t = lax.broadcasted_iota(jnp.int32, (rows, d_pool), 0) >> (nb.bit_length() - 1)
    pos1 = start_pos + c * tc + row_t + 1
    lane = lax.broadcasted_iota(jnp.int32, (rows, d_pool), 1)
    group = d_pool // len(POOL_WINDOWS)
    w = 1
    s_sel = None
    cnt_sel = None
    for gi, win in enumerate(POOL_WINDOWS):
        while w < win:
            a = a[w * nb:, :] + a[:a.shape[0] - w * nb, :]
            w *= 2
        assert w == win
        i0 = (POOL_HIST + 1 - win) * nb
        s_w = a[i0:i0 + rows, :]
        cnt_w = jnp.minimum(pos1, win).astype(F32)
        if s_sel is None:
            s_sel, cnt_sel = s_w, cnt_w
        else:
            in_later = lane >= gi * group
            s_sel = jnp.where(in_later, s_w, s_sel)
            cnt_sel = jnp.where(in_later, cnt_w, cnt_sel)
    pooled = s_sel / cnt_sel - u_pool
    y_pool = _mm(pooled, wpool_ref[...], precise) * pscale_ref[...]
    for k in range(n_pt):
        yt_scr[k] = y_pool[:, k * LANES:(k + 1) * LANES]
    ph_scr[0:POOL_HIST * nb, :] = ph_scr[pl.ds(tc * nb, POOL_HIST * nb), :]

    @pl.when(c == last)
    def _():
        pnew_ref[...] = ph_scr[nb:POOL_HIST * nb, :]

    rb = min(rows, 64)
    tap0 = CONV_HIST - (n_taps - 1)

    def conv_block(i, _):
        base = pl.multiple_of(i * rb, rb)
        acc = jnp.zeros((rb, d_conv), F32)
        for k in range(n_taps):
            acc = acc + cw_ref[k:k + 1, :] * ch_scr[pl.ds(base + (tap0 + k) * nb, rb), :]
        yc = _silu(_layer_norm(acc + cb_ref[...], clg_ref[...], clb_ref[...]))
        for k in range(n_pt, n_yt):
            yt_scr[k, pl.ds(base, rb), :] = yc[:, (k - n_pt) * LANES:(k - n_pt + 1) * LANES]
        return 0

    lax.fori_loop(0, rows // rb, conv_block, 0)
    ch_scr[0:CONV_HIST * nb, :] = ch_scr[pl.ds(tc * nb, CONV_HIST * nb), :]

    @pl.when(c == last)
    def _():
        cnew_ref[...] = ch_scr[tap0 * nb:CONV_HIST * nb, :]

    if tc == 1:
        for k in range(n_yt):
            y_ref[0, :, o_pool + k * LANES:o_pool + (k + 1) * LANES] = yt_scr[k].astype(y_ref.dtype)
    else:
        for b in range(nb):
            for k in range(n_yt):
                y_ref[b, :, o_pool + k * LANES:o_pool + (k + 1) * LANES] = (
                    yt_scr[k, pl.ds(b, tc, stride=nb), :].astype(y_ref.dtype))


def _mixers(u, h0, pool_cache, conv_cache, mw, *, nb, tc, start_pos, precise):
    n_g, n_r, d_u = u.shape
    n_t = n_g * n_r // nb
    rows = tc * nb
    tr = rows // n_g
    d_s5 = mw["wglu"].shape[0]
    d_pool = mw["wpool"].shape[0]
    d_conv = mw["conv_w"].shape[1]
    n_taps = mw["n_taps"]
    d_y = d_s5 + d_pool + d_conv
    n_state2 = h0.shape[1]
    weights = [mw["wb"], mw["wc"], mw["lam"], mw["d"], mw["wglu"], mw["bglu"], mw["wpool"], mw["pscale"],
               mw["conv_w"], mw["conv_b"], mw["conv_ln_g"], mw["conv_ln_b"]]
    kern = functools.partial(_mixer_kernel, nb=nb, tc=tc, start_pos=start_pos, d_s5=d_s5, d_pool=d_pool,
                             d_conv=d_conv, n_taps=n_taps, precise=precise)
    return pl.pallas_call(
        kern,
        grid=(n_t // tc,),
        in_specs=[pl.BlockSpec((n_g, tr, d_u), lambda c: (0, c, 0)),
                  _const_spec(h0.shape), _const_spec(pool_cache.shape), _const_spec(conv_cache.shape)]
                 + [_const_spec(w.shape) for w in weights],
        out_specs=(pl.BlockSpec((n_g, tr, d_y), lambda c: (0, c, 0)),
                   pl.BlockSpec(h0.shape, lambda c: (0, 0)),
                   pl.BlockSpec(((POOL_HIST - 1) * nb, d_pool), lambda c: (0, 0)),
                   pl.BlockSpec(((n_taps - 1) * nb, d_conv), lambda c: (0, 0))),
        out_shape=(jax.ShapeDtypeStruct((n_g, n_r, d_y), F32 if precise else BF16),
                   jax.ShapeDtypeStruct(h0.shape, F32),
                   jax.ShapeDtypeStruct(((POOL_HIST - 1) * nb, d_pool), F32),
                   jax.ShapeDtypeStruct(((n_taps - 1) * nb, d_conv), F32)),
        scratch_shapes=[pltpu.VMEM((n_state2 // LANES, rows, LANES), F32),
                        pltpu.VMEM((nb, n_state2), F32),
                        pltpu.VMEM(((POOL_HIST + tc) * nb, d_pool), F32),
                        pltpu.VMEM(((CONV_HIST + tc) * nb, d_conv), F32),
                        pltpu.VMEM(((d_pool + d_conv) // LANES, rows, LANES), F32)],
        compiler_params=_params("arbitrary"),
        name="token_mixers",
    )(u, h0, pool_cache, conv_cache, *weights)


def _merge_kernel(*refs, alpha, d_s5, d_pool, d_conv, n_exp, aliased, precise):
    with_router = n_exp > 0
    n_in = 11 + (1 if with_router else 0) + (1 if aliased else 0)
    x_ref, y_ref, wg_ref, bg_ref, ps5_ref, ppool_ref, pconv_ref, wo_ref, bo_ref, g_ref, b_ref = refs[:11]
    o_ref = refs[n_in]
    x = x_ref[...]
    xa = x if precise else x.astype(BF16)
    d = x.shape[-1]
    merged = None
    col = 0
    for j, (width, p_ref) in enumerate(((d_s5, ps5_ref), (d_pool, ppool_ref), (d_conv, pconv_ref))):
        gate = _sigmoid(_mm(xa, wg_ref[:, j * d:(j + 1) * d], precise) + bg_ref[:, j * d:(j + 1) * d])
        term = gate * _mm(y_ref[:, col:col + width], p_ref[...], precise)
        merged = term if merged is None else merged + term
        col += width
    out = _mm(merged, wo_ref[...], precise) + bo_ref[...]
    x1 = _layer_norm(alpha * x + out, g_ref[...], b_ref[...])
    o_ref[...] = x1
    if with_router:
        rt_ref = refs[11]
        ri_ref, rw_ref = refs[n_in + 1:n_in + 3]
        x_hi, x_lo = _split(x1)
        rt = rt_ref[...]
        rt_hi = rt.astype(BF16).astype(F32)
        rt2 = (rt_hi + pltpu.roll(rt - rt_hi, ROUTER_LANES // 2, axis=1)).astype(BF16)
        r = _dot(x_hi, rt2) + _dot(x_lo, rt2)
        logits = r + pltpu.roll(r, ROUTER_LANES // 2, axis=1)
        lane = lax.broadcasted_iota(jnp.int32, logits.shape, 1).astype(F32)
        neg = jnp.float32(-jnp.inf)
        l1 = jnp.where(lane < n_exp, logits, neg)
        m1 = jnp.max(l1, axis=-1, keepdims=True)
        i1 = jnp.min(jnp.where(l1 == m1, lane, float(ROUTER_LANES)), axis=-1, keepdims=True)
        l2 = jnp.where(lane == i1, neg, l1)
        m2 = jnp.max(l2, axis=-1, keepdims=True)
        i2 = jnp.min(jnp.where(l2 == m2, lane, float(ROUTER_LANES)), axis=-1, keepdims=True)
        e2 = jnp.exp(m2 - m1)
        w1 = 1.0 / (1.0 + e2)
        w2 = e2 * w1
        ri_ref[...] = jnp.where(lane == 0.0, i1, jnp.where(lane == 1.0, i2, 0.0)).astype(jnp.int32)
        rw_ref[...] = jnp.where(lane == 0.0, w1, jnp.where(lane == 1.0, w2, 0.0))


def _merge(x, y, lw, *, alpha, tm, precise, router=None, n_exp=0, shared=None, shared_rows=0, row0=0):
    n_g, n_r, d = x.shape
    d_y = y.shape[-1]
    d_s5, d_pool, d_conv = lw["proj_s5"].shape[0], lw["proj_pool"].shape[0], lw["proj_conv"].shape[0]
    inputs = [x, y, lw["w_gates"], lw["b_gates"], lw["proj_s5"], lw["proj_pool"], lw["proj_conv"], lw["w_out"],
              lw["b_out"], lw["ln1_g"], lw["ln1_b"]]
    tok_spec = pl.BlockSpec((None, tm, d), lambda g, r: (g, r, 0))
    in_specs = [tok_spec, pl.BlockSpec((None, tm, d_y), lambda g, r: (g, r, 0))] \
        + [_const_spec(w.shape) for w in inputs[2:]]
    aliases = {}
    if router is None:
        out_specs = [tok_spec]
        out_shape = [jax.ShapeDtypeStruct((n_g, n_r, d), F32)]
    else:
        inputs.append(router)
        in_specs.append(_const_spec(router.shape))
        if shared is not None:
            aliases = {len(inputs): 0}
            inputs.append(shared)
            in_specs.append(pl.BlockSpec(memory_space=pl.ANY))
        lane_spec = pl.BlockSpec((None, tm, ROUTER_LANES), lambda g, r: (g, r, 0))
        blk0, per_g = row0 // tm, n_r // tm
        out_specs = [pl.BlockSpec((tm, d), lambda g, r: (blk0 + g * per_g + r, 0)), lane_spec, lane_spec]
        out_shape = [jax.ShapeDtypeStruct((shared_rows, d), F32),
                     jax.ShapeDtypeStruct((n_g, n_r, ROUTER_LANES), jnp.int32),
                     jax.ShapeDtypeStruct((n_g, n_r, ROUTER_LANES), F32)]
    kern = functools.partial(_merge_kernel, alpha=alpha, d_s5=d_s5, d_pool=d_pool, d_conv=d_conv,
                             n_exp=n_exp if router is not None else 0, aliased=shared is not None, precise=precise)
    return pl.pallas_call(
        kern,
        grid=(n_g, n_r // tm),
        in_specs=in_specs,
        out_specs=tuple(out_specs),
        out_shape=tuple(out_shape),
        input_output_aliases=aliases,
        compiler_params=_params("parallel", "parallel"),
        name="merge_outproj_ln1",
    )(*inputs)


def _swiglu_tile(x, wg_ref, wu_ref, wd_ref, ff_chunk, precise):
    d_ff = wg_ref.shape[-1]
    acc = None
    for c0 in range(0, d_ff, ff_chunk):
        c1 = min(c0 + ff_chunk, d_ff)
        g = _mm(x, wg_ref[:, c0:c1], precise)
        u = _mm(x, wu_ref[:, c0:c1], precise)
        part = _mm(_silu(g) * u, wd_ref[c0:c1, :], precise)
        acc = part if acc is None else acc + part
    return acc


def _ffn_kernel(x_ref, wg_ref, wu_ref, wd_ref, g_ref, b_ref, o_ref, *, alpha, ff_chunk, precise):
    x = x_ref[...]
    f = _swiglu_tile(x if precise else x.astype(BF16), wg_ref, wu_ref, wd_ref, ff_chunk, precise)
    o_ref[...] = _layer_norm(alpha * x + f, g_ref[...], b_ref[...])


def _ffn_dense(x, wg, wu, wd, ln_g, ln_b, *, alpha, tm, ff_chunk, precise):
    n, d = x.shape
    weights = [wg, wu, wd, ln_g, ln_b]
    return pl.pallas_call(
        functools.partial(_ffn_kernel, alpha=alpha, ff_chunk=ff_chunk, precise=precise),
        grid=(n // tm,),
        in_specs=[pl.BlockSpec((tm, d), lambda i: (i, 0))] + [_const_spec(w.shape) for w in weights],
        out_specs=pl.BlockSpec((tm, d), lambda i: (i, 0)),
        out_shape=jax.ShapeDtypeStruct((n, d), F32),
        compiler_params=_params("parallel"),
        name="ffn_dense_ln2",
    )(x, *weights)


def _moe_kernel(te_ref, nt_ref, xs_ref, wg_ref, wu_ref, wd_ref, o_ref, *, ff_chunk):
    i = pl.program_id(0)

    @pl.when(i < nt_ref[0])
    def _():
        o_ref[...] = _swiglu_tile(xs_ref[...].astype(BF16), wg_ref, wu_ref, wd_ref, ff_chunk, False)

    @pl.when(i >= nt_ref[0])
    def _():
        o_ref[...] = jnp.zeros(o_ref.shape, o_ref.dtype)


def _moe_grouped(tile_expert, n_tiles_used, xs, wg, wu, wd, *, tm, ff_chunk):
    n_slots, d = xs.shape
    d_ff = wg.shape[-1]
    grid_spec = pltpu.PrefetchScalarGridSpec(
        num_scalar_prefetch=2,
        grid=(n_slots // tm,),
        in_specs=[pl.BlockSpec((tm, d), lambda i, te, nt: (i, 0)),
                  pl.BlockSpec((None, d, d_ff), lambda i, te, nt: (te[i], 0, 0)),
                  pl.BlockSpec((None, d, d_ff), lambda i, te, nt: (te[i], 0, 0)),
                  pl.BlockSpec((None, d_ff, d), lambda i, te, nt: (te[i], 0, 0))],
        out_specs=pl.BlockSpec((tm, d), lambda i, te, nt: (i, 0)),
    )
    return pl.pallas_call(
        functools.partial(_moe_kernel, ff_chunk=ff_chunk),
        grid_spec=grid_spec,
        out_shape=jax.ShapeDtypeStruct((n_slots, d), F32),
        compiler_params=_params("arbitrary"),
        name="moe_grouped_swiglu",
    )(tile_expert, n_tiles_used, xs, wg, wu, wd)


def _combine_kernel(x_ref, ya_ref, yb_ref, rw_ref, g_ref, b_ref, o_ref, *, alpha):
    f = rw_ref[:, 0:1] * ya_ref[...] + rw_ref[:, 1:2] * yb_ref[...]
    o_ref[...] = _layer_norm(alpha * x_ref[...] + f, g_ref[...], b_ref[...])


def _moe_combine(x1, ya, yb, rw, ln_g, ln_b, *, alpha, tm, row0):
    n = rw.shape[0]
    d = x1.shape[1]
    blk0 = row0 // tm
    shared = pl.BlockSpec((tm, d), lambda i: (blk0 + i, 0))
    return pl.pallas_call(
        functools.partial(_combine_kernel, alpha=alpha),
        grid=(n // tm,),
        in_specs=[shared, shared, shared, pl.BlockSpec((tm, ROUTER_LANES), lambda i: (i, 0)),
                  _const_spec(ln_g.shape), _const_spec(ln_b.shape)],
        out_specs=pl.BlockSpec((tm, d), lambda i: (i, 0)),
        out_shape=jax.ShapeDtypeStruct((n, d), F32),
        compiler_params=_params("parallel"),
        name="moe_combine_ln2",
    )(x1, ya, yb, rw, ln_g, ln_b)


def _moe_route(top_i, n_experts, tm):
    n = top_i.shape[0]
    n_flat = TOP_K * n
    flat_e = top_i.reshape(-1)
    iota = jnp.arange(n_flat, dtype=jnp.int32)
    sorted_e, sorted_j = lax.sort((flat_e, iota), num_keys=1, is_stable=True)
    experts = jnp.arange(n_experts, dtype=jnp.int32)
    counts = jnp.sum((flat_e[:, None] == experts[None, :]).astype(jnp.int32), axis=0)
    start_end = jnp.cumsum(counts)
    padded = ((counts + tm - 1) // tm) * tm
    off_end = jnp.cumsum(padded)
    off = off_end - padded
    shift = off - (start_end - counts)
    n_slots = ((n_flat + n_experts * tm + tm - 1) // tm) * tm
    n_tiles = n_slots // tm
    tile_start = jnp.arange(n_tiles, dtype=jnp.int32) * tm
    tile_expert = jnp.sum((tile_start[:, None] >= off_end[None, :]).astype(jnp.int32), axis=1)
    tile_expert = jnp.minimum(tile_expert, n_experts - 1).astype(jnp.int32)
    n_used = (off_end[-1] // tm).astype(jnp.int32).reshape(1)
    slot = jnp.arange(n_slots, dtype=jnp.int32).reshape(n_tiles, tm)
    t_off, t_cnt, t_shift = off[tile_expert][:, None], counts[tile_expert][:, None], shift[tile_expert][:, None]
    valid = (slot - t_off) < t_cnt
    src = jnp.clip(slot - t_shift, 0, n_flat - 1)
    slot_tok = jnp.where(valid, sorted_j[src] // TOP_K, 0).reshape(n_slots)
    slot_of_sorted = iota + shift[sorted_e]
    _, slots = lax.sort((sorted_j, slot_of_sorted), num_keys=1)
    return slot_tok, tile_expert, n_used, slots.reshape(n, TOP_K)


def _pad_time_major(cache, steps):
    n_b, n_s, n_c = cache.shape
    tmaj = jnp.transpose(cache, (1, 0, 2))
    tmaj = jnp.pad(tmaj, ((steps - n_s, 0), (0, 0), (0, 0)))
    return tmaj.reshape(steps * n_b, n_c)


def _from_time_major(flat, n_b):
    n_c = flat.shape[1]
    return jnp.transpose(flat.reshape(-1, n_b, n_c), (1, 0, 2))


def kernel(x_prompt, x_sample, state_s5_re, state_s5_im, cache_pool, cache_conv, w_in, b_in, s5_a_re, s5_a_im, s5_log_dt, s5_b_re, s5_b_im, s5_c_re, s5_c_im, s5_d, s5_w_glu, s5_b_glu, pool_w, pool_scale, conv_w, conv_b, conv_ln_g, conv_ln_b, proj_s5, proj_pool, proj_conv, w_out, b_out, ln1_g, ln1_b, ln2_g, ln2_b, ffn_w_gate, ffn_w_up, ffn_w_down, moe_router, moe_w_gate, moe_w_up, moe_w_down):
    depth, d_model, _ = w_in.shape
    n_bp, n_tp, _ = x_prompt.shape
    n_bs, n_ts, _ = x_sample.shape
    assert n_ts == 1
    n_grp, n_p = s5_a_re.shape[1:]
    n_h = s5_b_re.shape[-1]
    n_state = n_grp * n_p
    d_s5 = n_grp * n_h
    d_pool = pool_scale.shape[1]
    d_conv = conv_w.shape[2]
    n_taps = conv_w.shape[1]
    d_lin = d_s5 + d_pool
    d_u = d_lin + 2 * d_conv
    n_exp = moe_router.shape[-1]
    alpha = (2.0 * depth) ** 0.25
    ff_chunk = MXU_DIM
    assert cache_pool.shape[2] == POOL_HIST - 1 and n_taps - 1 <= CONV_HIST
    n_p_tok = n_bp * n_tp
    n_tok = n_p_tok + n_bs
    tm_p = min(512, n_tp)
    tc_p = min(64, n_tp)

    def row(v):
        return v.reshape(1, -1).astype(F32)

    def both(w):
        w = w.astype(F32)
        return w.astype(BF16), w

    xp = x_prompt
    xs = x_sample.reshape(1, n_bs, d_model)
    zeros_p = dict(h0=jnp.zeros((n_bp, 2 * n_state), F32),
                   pool=jnp.zeros((POOL_HIST * n_bp, d_pool), F32),
                   conv=jnp.zeros((CONV_HIST * n_bp, d_conv), F32))
    outs = {k: [] for k in ("re_p", "im_p", "pool_p", "conv_p", "re_s", "im_s", "pool_s", "conv_s")}

    for l in range(depth):
        lam, bb_re, bb_im = _s5_discretise(s5_a_re[l], s5_a_im[l], s5_log_dt[l], s5_b_re[l], s5_b_im[l])
        conv_w_pad = jnp.pad(conv_w[l].astype(F32), ((0, CONV_HIST - n_taps), (0, 0)))
        shared_mw = dict(lam=lam, d=row(s5_d[l]), bglu=row(s5_b_glu[l]), pscale=row(pool_scale[l]),
                         conv_w=conv_w_pad, conv_b=row(conv_b[l]), conv_ln_g=row(conv_ln_g[l]),
                         conv_ln_b=row(conv_ln_b[l]), n_taps=n_taps)
        mats = dict(wb=both(_s5_input_blocks(bb_re, bb_im, n_grp, n_p)),
                    wc=both(_s5_output_blocks(s5_c_re[l].astype(F32), s5_c_im[l].astype(F32), n_p)),
                    wglu=both(s5_w_glu[l]), wpool=both(_block_diag(pool_w[l].astype(F32))))
        w1 = both(w_in[l, :, :d_u])
        b1 = row(b_in[l, :d_u])
        shared_lw = dict(b_gates=row(b_in[l, d_u:]), b_out=row(b_out[l]), ln1_g=row(ln1_g[l]), ln1_b=row(ln1_b[l]))
        lmats = dict(w_gates=both(w_in[l, :, d_u:]), proj_s5=both(proj_s5[l]), proj_pool=both(proj_pool[l]),
                     proj_conv=both(proj_conv[l]), w_out=both(w_out[l]))
        is_moe = l % 2 == 1
        router = None
        if is_moe:
            assert n_exp <= ROUTER_LANES // 2
            router = jnp.pad(moe_router[l // 2].astype(F32), ((0, 0), (0, ROUTER_LANES - n_exp)))

        mw = dict(shared_mw, **{k: v[0] for k, v in mats.items()})
        lw = dict(shared_lw, **{k: v[0] for k, v in lmats.items()})
        up = _inproj(xp, w1[0], b1, d_lin=d_lin, d_conv=d_conv, tm=tm_p, precise=False)
        yp, hT_p, pool_p, conv_p = _mixers(up, zeros_p["h0"], zeros_p["pool"], zeros_p["conv"], mw,
                                           nb=n_bp, tc=tc_p, start_pos=0, precise=False)
        res_p = _merge(xp, yp, lw, alpha=alpha, tm=tm_p, precise=False, router=router, n_exp=n_exp,
                       shared_rows=n_tok)
        mw = dict(shared_mw, **{k: v[1] for k, v in mats.items()})
        lw = dict(shared_lw, **{k: v[1] for k, v in lmats.items()})
        h0_s = jnp.concatenate([state_s5_re[l].reshape(n_bs, n_state), state_s5_im[l].reshape(n_bs, n_state)],
                               axis=1).astype(F32)
        us = _inproj(xs, w1[1], b1, d_lin=d_lin, d_conv=d_conv, tm=n_bs, precise=True)
        ys, hT_s, pool_s, conv_s = _mixers(us, h0_s, _pad_time_major(cache_pool[l].astype(F32), POOL_HIST),
                                           _pad_time_major(cache_conv[l].astype(F32), CONV_HIST), mw,
                                           nb=n_bs, tc=1, start_pos=PAST_LEN, precise=True)
        res_s = _merge(xs, ys, lw, alpha=alpha, tm=n_bs, precise=True, router=router, n_exp=n_exp,
                       shared=res_p[0] if is_moe else None, shared_rows=n_tok, row0=n_p_tok)

        outs["re_p"].append(hT_p[:, :n_state].reshape(n_bp, n_grp, n_p))
        outs["im_p"].append(hT_p[:, n_state:].reshape(n_bp, n_grp, n_p))
        outs["pool_p"].append(_from_time_major(pool_p, n_bp))
        outs["conv_p"].append(_from_time_major(conv_p, n_bp))
        outs["re_s"].append(hT_s[:, :n_state].reshape(n_bs, n_grp, n_p))
        outs["im_s"].append(hT_s[:, n_state:].reshape(n_bs, n_grp, n_p))
        outs["pool_s"].append(_from_time_major(pool_s, n_bs))
        outs["conv_s"].append(_from_time_major(conv_s, n_bs))

        g2, b2 = row(ln2_g[l]), row(ln2_b[l])
        i = l // 2
        if not is_moe:
            wg, wu, wd = both(ffn_w_gate[i]), both(ffn_w_up[i]), both(ffn_w_down[i])
            xp = _ffn_dense(res_p[0].reshape(n_p_tok, d_model), wg[0], wu[0], wd[0], g2, b2, alpha=alpha, tm=tm_p,
                            ff_chunk=ff_chunk, precise=False).reshape(n_bp, n_tp, d_model)
            xs = _ffn_dense(res_s[0].reshape(n_bs, d_model), wg[1], wu[1], wd[1], g2, b2, alpha=alpha, tm=n_bs,
                            ff_chunk=ff_chunk, precise=True).reshape(1, n_bs, d_model)
        else:
            wg, wu, wd = moe_w_gate[i].astype(BF16), moe_w_up[i].astype(BF16), moe_w_down[i].astype(BF16)
            _, ri_p, rw_p = res_p
            x1_all, ri_s, rw_s = res_s
            top_i = jnp.concatenate([ri_p.reshape(n_p_tok, ROUTER_LANES)[:, :TOP_K],
                                     ri_s.reshape(n_bs, ROUTER_LANES)[:, :TOP_K]], axis=0)
            tm_moe = 256
            slot_tok, tile_expert, n_used, slots = _moe_route(top_i, n_exp, tm_moe)
            x_sorted = jnp.take(x1_all, slot_tok, axis=0)
            y_sorted = _moe_grouped(tile_expert, n_used, x_sorted, wg, wu, wd, tm=tm_moe, ff_chunk=ff_chunk)
            ya = jnp.take(y_sorted, slots[:, 0], axis=0)
            yb = jnp.take(y_sorted, slots[:, 1], axis=0)
            xp = _moe_combine(x1_all, ya, yb, rw_p.reshape(n_p_tok, ROUTER_LANES), g2, b2,
                              alpha=alpha, tm=tm_p, row0=0).reshape(n_bp, n_tp, d_model)
            xs = _moe_combine(x1_all, ya, yb, rw_s.reshape(n_bs, ROUTER_LANES), g2, b2,
                              alpha=alpha, tm=n_bs, row0=n_p_tok).reshape(1, n_bs, d_model)

    st = lambda k: jnp.stack(outs[k])
    return (xp, xs.reshape(n_bs, 1, d_model), st("re_p"), st("im_p"), st("pool_p"), st("conv_p"),
            st("re_s"), st("im_s"), st("pool_s"), st("conv_s"))
```

```python
import functools

import jax
import jax.numpy as jnp
from jax import lax
from jax.experimental import pallas as pl
from jax.experimental.pallas import tpu as pltpu

F32 = jnp.float32
BF16 = jnp.bfloat16

LN_EPS = 1e-5
POOL_WINDOWS = (2, 4, 8, 16)
PAST_LEN = 16384
TOP_K = 2
POOL_HIST = 16
CONV_HIST = 32
ROUTER_LANES = 128
LANES = 128
MXU_DIM = 256
VMEM_LIMIT = 56 * 1024 * 1024


def _sigmoid(x):
    return 0.5 * jnp.tanh(0.5 * x) + 0.5


def _silu(x):
    return x * _sigmoid(x)


def _layer_norm(x, g, b):
    mu = jnp.mean(x, axis=-1, keepdims=True)
    xc = x - mu
    var = jnp.mean(xc * xc, axis=-1, keepdims=True)
    return xc * lax.rsqrt(var + LN_EPS) * g + b


def _dot(a, b):
    return jnp.dot(a, b, preferred_element_type=F32)


def _split(v):
    hi = v.astype(BF16)
    return hi, (v - hi.astype(F32)).astype(BF16)


def _mm(a, w, precise):
    if not precise:
        return _dot(a.astype(BF16), w)
    a_hi, a_lo = _split(a)
    w_hi, w_lo = _split(w)
    m = a.shape[0]
    r = _dot(jnp.concatenate([a_hi, a_lo], axis=0), w_hi)
    return r[:m] + r[m:] + _dot(a_hi, w_lo)


def _params(*sem):
    return pltpu.CompilerParams(dimension_semantics=sem, vmem_limit_bytes=VMEM_LIMIT)


def _const_spec(shape):
    return pl.BlockSpec(shape, lambda *_: (0,) * len(shape), pipeline_mode=pl.Buffered(1))


def _row_spec(tm, d, blk0=0):
    return pl.BlockSpec((tm, d), lambda i: (blk0 + i, 0))


def _s5_disc_kernel(are_ref, aim_ref, ldt_ref, bre_ref, bim_ref, lam_ref, bbre_ref, bbim_ref):
    a_re = are_ref[...]
    a_im = aim_ref[...]
    dt = jnp.exp(ldt_ref[...])
    mag = jnp.exp(a_re * dt)
    lr = mag * jnp.cos(a_im * dt)
    li = mag * jnp.sin(a_im * dt)
    den = a_re * a_re + a_im * a_im
    nr = lr - 1.0
    cr = (nr * a_re + li * a_im) / den
    ci = (li * a_re - nr * a_im) / den
    lam_ref[0:1, :] = lr
    lam_ref[1:2, :] = li
    b_re = bre_ref[...]
    b_im = bim_ref[...]
    bbre_ref[...] = cr * b_re - ci * b_im
    bbim_ref[...] = cr * b_im + ci * b_re


def _s5_discretise(a_re, a_im, log_dt, b_re, b_im):
    n_g, n_p = a_re.shape
    n_h = b_re.shape[-1]
    n_s = n_g * n_p
    ldt = jnp.broadcast_to(log_dt[:, None], (n_g, n_p)).reshape(1, n_s)
    b_re_t = b_re.reshape(n_s, n_h).T
    b_im_t = b_im.reshape(n_s, n_h).T
    return pl.pallas_call(
        _s5_disc_kernel,
        out_shape=(jax.ShapeDtypeStruct((2, n_s), F32),
                   jax.ShapeDtypeStruct((n_h, n_s), F32),
                   jax.ShapeDtypeStruct((n_h, n_s), F32)),
        name="s5_discretise",
    )(a_re.reshape(1, n_s), a_im.reshape(1, n_s), ldt, b_re_t, b_im_t)


def _block_diag(blocks):
    n_g, r, c = blocks.shape
    eye = jnp.eye(n_g, dtype=blocks.dtype)
    return (eye[:, None, :, None] * blocks[:, :, None, :]).reshape(n_g * r, n_g * c)


def _s5_input_blocks(bb_re, bb_im, n_grp, n_p):
    n_h = bb_re.shape[0]
    gpb = MXU_DIM // n_p
    ch = gpb * n_h
    assert MXU_DIM % n_p == 0 and LANES % ch == 0 and n_grp % gpb == 0
    n_blk = n_grp // gpb
    per_chunk = LANES // ch

    def blocks(bb):
        g = jnp.transpose(bb.reshape(n_h, n_grp, n_p), (1, 0, 2)).reshape(n_blk, gpb, n_h, n_p)
        dense = jax.vmap(_block_diag)(g)
        return jnp.stack([jnp.pad(dense[j], ((ch * (j % per_chunk), LANES - ch * (j % per_chunk + 1)), (0, 0)))
                          for j in range(n_blk)])

    return jnp.concatenate([blocks(bb_re), blocks(bb_im)], axis=2)


def _s5_output_blocks(c_re, c_im, n_p):
    n_grp, n_h, _ = c_re.shape
    gpb = MXU_DIM // n_h
    assert n_grp % gpb == 0
    n_blk = n_grp // gpb

    def blocks(cm):
        g = jnp.transpose(cm, (0, 2, 1)).reshape(n_blk, gpb, n_p, n_h)
        return jax.vmap(_block_diag)(g)

    return jnp.concatenate([blocks(c_re), blocks(-c_im)], axis=1)


def _inproj_kernel(x_ref, w_ref, b_ref, o_ref, *, d_lin, d_conv, precise):
    u = _mm(x_ref[...], w_ref[...], precise) + b_ref[...]
    o_ref[:, :d_lin] = u[:, :d_lin]
    o_ref[:, d_lin:] = u[:, d_lin:d_lin + d_conv] * _sigmoid(u[:, d_lin + d_conv:])


def _inproj(x, w, b, *, d_lin, d_conv, tm, precise):
    n, d = x.shape
    d_out = d_lin + d_conv
    return pl.pallas_call(
        functools.partial(_inproj_kernel, d_lin=d_lin, d_conv=d_conv, precise=precise),
        grid=(n // tm,),
        in_specs=[_row_spec(tm, d), _const_spec(w.shape), _const_spec(b.shape)],
        out_specs=_row_spec(tm, d_out),
        out_shape=jax.ShapeDtypeStruct((n, d_out), F32),
        compiler_params=_params("parallel"),
        name="in_projection",
    )(x, w, b)


def _mixer_kernel(u_ref, h0_ref, pc_ref, cc_ref, wb_ref, wc_ref, lam_ref, d_ref, wglu_ref, bglu_ref,
                  wpool_ref, pscale_ref, cw_ref, cb_ref, clg_ref, clb_ref,
                  y_ref, hT_ref, pnew_ref, cnew_ref,
                  s_scr, h_scr, ph_scr, ch_scr, *, nb, tc, start_pos, d_s5, d_pool, d_conv, n_taps, precise):
    c = pl.program_id(0)
    last = pl.num_programs(0) - 1
    rows = tc * nb
    n_state = lam_ref.shape[1]
    o_pool = d_s5
    o_conv = d_s5 + d_pool

    @pl.when(c == 0)
    def _():
        h_scr[...] = h0_ref[...]
        ph_scr[0:POOL_HIST * nb, :] = pc_ref[...]
        ch_scr[0:CONV_HIST * nb, :] = cc_ref[...]

    u_s5 = u_ref[:, 0:d_s5]
    n_in_blk = wb_ref.shape[0]
    per_chunk = LANES // (d_s5 // n_in_blk)
    for j in range(n_in_blk):
        k0 = (j // per_chunk) * LANES
        bu = _mm(u_s5[:, k0:k0 + LANES], wb_ref[j], precise)
        s_scr[:, j * MXU_DIM:(j + 1) * MXU_DIM] = bu[:, :MXU_DIM]
        s_scr[:, n_state + j * MXU_DIM:n_state + (j + 1) * MXU_DIM] = bu[:, MXU_DIM:]
    if tc == 1:
        lr = lam_ref[0:1, :]
        li = lam_ref[1:2, :]
        h_re = h_scr[:, 0:n_state]
        h_im = h_scr[:, n_state:]
        n_re = lr * h_re - li * h_im + s_scr[:, 0:n_state]
        n_im = lr * h_im + li * h_re + s_scr[:, n_state:]
        s_scr[:, 0:n_state] = n_re
        s_scr[:, n_state:] = n_im
        h_scr[:, 0:n_state] = n_re
        h_scr[:, n_state:] = n_im
    else:
        assert nb == 8
        n_part = 2
        half = n_state // n_part
        for hf in range(n_part):
            re0 = hf * half
            im0 = n_state + hf * half
            lr = jnp.broadcast_to(lam_ref[0:1, re0:re0 + half], (nb, half))
            li = jnp.broadcast_to(lam_ref[1:2, re0:re0 + half], (nb, half))

            def step(t, carry, re0=re0, im0=im0, lr=lr, li=li):
                h_re, h_im = carry
                at_t = pl.ds(pl.multiple_of(t * nb, nb), nb)
                n_re = lr * h_re - li * h_im + s_scr[at_t, re0:re0 + half]
                n_im = lr * h_im + li * h_re + s_scr[at_t, im0:im0 + half]
                s_scr[at_t, re0:re0 + half] = n_re
                s_scr[at_t, im0:im0 + half] = n_im
                return n_re, n_im

            h_re, h_im = lax.fori_loop(0, tc, step, (h_scr[:, re0:re0 + half], h_scr[:, im0:im0 + half]),
                                       unroll=4)
            h_scr[:, re0:re0 + half] = h_re
            h_scr[:, im0:im0 + half] = h_im

    n_out_blk = wc_ref.shape[0]
    spb = n_state // n_out_blk
    ys = []
    for j in range(n_out_blk):
        h_re_b = s_scr[:, j * spb:(j + 1) * spb]
        h_im_b = s_scr[:, n_state + j * spb:n_state + (j + 1) * spb]
        ys.append(_mm(h_re_b, wc_ref[j, 0:spb, :], precise) + _mm(h_im_b, wc_ref[j, spb:, :], precise))
    y = jnp.concatenate(ys, axis=1) + d_ref[...] * u_s5
    y = jax.nn.gelu(y)
    y = y * _sigmoid(_mm(y, wglu_ref[...], precise) + bglu_ref[...])
    y_ref[:, 0:d_s5] = y.astype(y_ref.dtype)

    @pl.when(c == last)
    def _():
        hT_ref[...] = h_scr[...]

    u_pool = u_ref[:, o_pool:o_pool + d_pool]
    ph_scr[pl.ds(POOL_HIST * nb, rows), :] = u_pool
    a = ph_scr[...]
    row_t = lax.broadcasted_iota(jnp.int32, (rows, d_pool), 0) >> (nb.bit_length() - 1)
    pos1 = start_pos + c * tc + row_t + 1
    lane = lax.broadcasted_iota(jnp.int32, (rows, d_pool), 1)
    group = d_pool // len(POOL_WINDOWS)
    w = 1
    s_sel = None
    cnt_sel = None
    for gi, win in enumerate(POOL_WINDOWS):
        while w < win:
            a = a[w * nb:, :] + a[:a.shape[0] - w * nb, :]
            w *= 2
        assert w == win
        i0 = (POOL_HIST + 1 - win) * nb
        s_w = a[i0:i0 + rows, :]
        cnt_w = jnp.minimum(pos1, win).astype(F32)
        if s_sel is None:
            s_sel, cnt_sel = s_w, cnt_w
        else:
            in_later = lane >= gi * group
            s_sel = jnp.where(in_later, s_w, s_sel)
            cnt_sel = jnp.where(in_later, cnt_w, cnt_sel)
    pooled = s_sel / cnt_sel - u_pool
    y_pool = _mm(pooled, wpool_ref[...], precise) * pscale_ref[...]
    y_ref[:, o_pool:o_pool + d_pool] = y_pool.astype(y_ref.dtype)
    ph_scr[0:POOL_HIST * nb, :] = ph_scr[pl.ds(tc * nb, POOL_HIST * nb), :]

    @pl.when(c == last)
    def _():
        pnew_ref[...] = ph_scr[nb:POOL_HIST * nb, :]

    ch_scr[pl.ds(CONV_HIST * nb, rows), :] = u_ref[:, o_conv:o_conv + d_conv]
    rb = min(rows, 64)
    tap0 = CONV_HIST - (n_taps - 1)

    def conv_block(i, _):
        base = pl.multiple_of(i * rb, rb)
        acc = jnp.zeros((rb, d_conv), F32)
        for k in range(n_taps):
            acc = acc + cw_ref[k:k + 1, :] * ch_scr[pl.ds(base + (tap0 + k) * nb, rb), :]
        yc = _silu(_layer_norm(acc + cb_ref[...], clg_ref[...], clb_ref[...]))
        y_ref[pl.ds(base, rb), o_conv:o_conv + d_conv] = yc.astype(y_ref.dtype)
        return 0

    lax.fori_loop(0, rows // rb, conv_block, 0)
    ch_scr[0:CONV_HIST * nb, :] = ch_scr[pl.ds(tc * nb, CONV_HIST * nb), :]

    @pl.when(c == last)
    def _():
        cnew_ref[...] = ch_scr[tap0 * nb:CONV_HIST * nb, :]


def _mixers(u, h0, pool_cache, conv_cache, mw, *, nb, tc, start_pos, precise):
    n_rows, d_u = u.shape
    n_t = n_rows // nb
    rows = tc * nb
    d_s5 = mw["wglu"].shape[0]
    d_pool = mw["wpool"].shape[0]
    d_conv = mw["conv_w"].shape[1]
    n_taps = mw["n_taps"]
    d_y = d_s5 + d_pool + d_conv
    n_state2 = h0.shape[1]
    weights = [mw["wb"], mw["wc"], mw["lam"], mw["d"], mw["wglu"], mw["bglu"], mw["wpool"], mw["pscale"],
               mw["conv_w"], mw["conv_b"], mw["conv_ln_g"], mw["conv_ln_b"]]
    kern = functools.partial(_mixer_kernel, nb=nb, tc=tc, start_pos=start_pos, d_s5=d_s5, d_pool=d_pool,
                             d_conv=d_conv, n_taps=n_taps, precise=precise)
    return pl.pallas_call(
        kern,
        grid=(n_t // tc,),
        in_specs=[_row_spec(rows, d_u),
                  _const_spec(h0.shape), _const_spec(pool_cache.shape), _const_spec(conv_cache.shape)]
                 + [_const_spec(w.shape) for w in weights],
        out_specs=(_row_spec(rows, d_y),
                   pl.BlockSpec(h0.shape, lambda c: (0, 0)),
                   pl.BlockSpec(((POOL_HIST - 1) * nb, d_pool), lambda c: (0, 0)),
                   pl.BlockSpec(((n_taps - 1) * nb, d_conv), lambda c: (0, 0))),
        out_shape=(jax.ShapeDtypeStruct((n_rows, d_y), F32 if precise else BF16),
                   jax.ShapeDtypeStruct(h0.shape, F32),
                   jax.ShapeDtypeStruct(((POOL_HIST - 1) * nb, d_pool), F32),
                   jax.ShapeDtypeStruct(((n_taps - 1) * nb, d_conv), F32)),
        scratch_shapes=[pltpu.VMEM((rows, n_state2), F32),
                        pltpu.VMEM((nb, n_state2), F32),
                        pltpu.VMEM(((POOL_HIST + tc) * nb, d_pool), F32),
                        pltpu.VMEM(((CONV_HIST + tc) * nb, d_conv), F32)],
        compiler_params=_params("arbitrary"),
        name="token_mixers",
    )(u, h0, pool_cache, conv_cache, *weights)


def _merge_kernel(*refs, alpha, d_s5, d_pool, d_conv, n_exp, aliased, precise):
    with_router = n_exp > 0
    n_in = 11 + (1 if with_router else 0) + (1 if aliased else 0)
    x_ref, y_ref, wg_ref, bg_ref, ps5_ref, ppool_ref, pconv_ref, wo_ref, bo_ref, g_ref, b_ref = refs[:11]
    o_ref = refs[n_in]
    x = x_ref[...]
    xa = x if precise else x.astype(BF16)
    d = x.shape[-1]
    merged = None
    col = 0
    for j, (width, p_ref) in enumerate(((d_s5, ps5_ref), (d_pool, ppool_ref), (d_conv, pconv_ref))):
        gate = _sigmoid(_mm(xa, wg_ref[:, j * d:(j + 1) * d], precise) + bg_ref[:, j * d:(j + 1) * d])
        term = gate * _mm(y_ref[:, col:col + width], p_ref[...], precise)
        merged = term if merged is None else merged + term
        col += width
    out = _mm(merged, wo_ref[...], precise) + bo_ref[...]
    x1 = _layer_norm(alpha * x + out, g_ref[...], b_ref[...])
    o_ref[...] = x1
    if with_router:
        rt_ref = refs[11]
        ri_ref, rw_ref = refs[n_in + 1:n_in + 3]
        x_hi, x_lo = _split(x1)
        rt = rt_ref[...]
        rt_hi = rt.astype(BF16).astype(F32)
        rt2 = (rt_hi + pltpu.roll(rt - rt_hi, ROUTER_LANES // 2, axis=1)).astype(BF16)
        r = _dot(x_hi, rt2) + _dot(x_lo, rt2)
        logits = r + pltpu.roll(r, ROUTER_LANES // 2, axis=1)
        lane = lax.broadcasted_iota(jnp.int32, logits.shape, 1).astype(F32)
        neg = jnp.float32(-jnp.inf)
        l1 = jnp.where(lane < n_exp, logits, neg)
        m1 = jnp.max(l1, axis=-1, keepdims=True)
        i1 = jnp.min(jnp.where(l1 == m1, lane, float(ROUTER_LANES)), axis=-1, keepdims=True)
        l2 = jnp.where(lane == i1, neg, l1)
        m2 = jnp.max(l2, axis=-1, keepdims=True)
        i2 = jnp.min(jnp.where(l2 == m2, lane, float(ROUTER_LANES)), axis=-1, keepdims=True)
        e2 = jnp.exp(m2 - m1)
        w1 = 1.0 / (1.0 + e2)
        w2 = e2 * w1
        ri_ref[...] = jnp.where(lane == 0.0, i1, jnp.where(lane == 1.0, i2, 0.0)).astype(jnp.int32)
        rw_ref[...] = jnp.where(lane == 0.0, w1, jnp.where(lane == 1.0, w2, 0.0))


def _merge(x, y, lw, *, alpha, tm, precise, router=None, n_exp=0, shared=None, shared_rows=0, row0=0):
    n, d = x.shape
    d_y = y.shape[-1]
    d_s5, d_pool, d_conv = lw["proj_s5"].shape[0], lw["proj_pool"].shape[0], lw["proj_conv"].shape[0]
    inputs = [x, y, lw["w_gates"], lw["b_gates"], lw["proj_s5"], lw["proj_pool"], lw["proj_conv"], lw["w_out"],
              lw["b_out"], lw["ln1_g"], lw["ln1_b"]]
    in_specs = [_row_spec(tm, d), _row_spec(tm, d_y)] + [_const_spec(w.shape) for w in inputs[2:]]
    aliases = {}
    if router is None:
        out_specs = [_row_spec(tm, d)]
        out_shape = [jax.ShapeDtypeStruct((n, d), F32)]
    else:
        inputs.append(router)
        in_specs.append(_const_spec(router.shape))
        if shared is not None:
            aliases = {len(inputs): 0}
            inputs.append(shared)
            in_specs.append(pl.BlockSpec(memory_space=pl.ANY))
        out_specs = [_row_spec(tm, d, row0 // tm), _row_spec(tm, ROUTER_LANES), _row_spec(tm, ROUTER_LANES)]
        out_shape = [jax.ShapeDtypeStruct((shared_rows, d), F32),
                     jax.ShapeDtypeStruct((n, ROUTER_LANES), jnp.int32),
                     jax.ShapeDtypeStruct((n, ROUTER_LANES), F32)]
    kern = functools.partial(_merge_kernel, alpha=alpha, d_s5=d_s5, d_pool=d_pool, d_conv=d_conv,
                             n_exp=n_exp if router is not None else 0, aliased=shared is not None, precise=precise)
    return pl.pallas_call(
        kern,
        grid=(n // tm,),
        in_specs=in_specs,
        out_specs=tuple(out_specs),
        out_shape=tuple(out_shape),
        input_output_aliases=aliases,
        compiler_params=_params("parallel"),
        name="merge_outproj_ln1",
    )(*inputs)


def _swiglu_tile(x, wg_ref, wu_ref, wd_ref, ff_chunk, precise):
    d_ff = wg_ref.shape[-1]
    acc = None
    for c0 in range(0, d_ff, ff_chunk):
        c1 = min(c0 + ff_chunk, d_ff)
        g = _mm(x, wg_ref[:, c0:c1], precise)
        u = _mm(x, wu_ref[:, c0:c1], precise)
        part = _mm(_silu(g) * u, wd_ref[c0:c1, :], precise)
        acc = part if acc is None else acc + part
    return acc


def _ffn_kernel(x_ref, wg_ref, wu_ref, wd_ref, g_ref, b_ref, o_ref, *, alpha, ff_chunk, precise):
    x = x_ref[...]
    f = _swiglu_tile(x if precise else x.astype(BF16), wg_ref, wu_ref, wd_ref, ff_chunk, precise)
    o_ref[...] = _layer_norm(alpha * x + f, g_ref[...], b_ref[...])


def _ffn_dense(x, wg, wu, wd, ln_g, ln_b, *, alpha, tm, ff_chunk, precise):
    n, d = x.shape
    weights = [wg, wu, wd, ln_g, ln_b]
    return pl.pallas_call(
        functools.partial(_ffn_kernel, alpha=alpha, ff_chunk=ff_chunk, precise=precise),
        grid=(n // tm,),
        in_specs=[_row_spec(tm, d)] + [_const_spec(w.shape) for w in weights],
        out_specs=_row_spec(tm, d),
        out_shape=jax.ShapeDtypeStruct((n, d), F32),
        compiler_params=_params("parallel"),
        name="ffn_dense_ln2",
    )(x, *weights)


def _moe_kernel(te_ref, nt_ref, xs_ref, wg_ref, wu_ref, wd_ref, o_ref, *, ff_chunk):
    i = pl.program_id(0)

    @pl.when(i < nt_ref[0])
    def _():
        o_ref[...] = _swiglu_tile(xs_ref[...].astype(BF16), wg_ref, wu_ref, wd_ref, ff_chunk, False)

    @pl.when(i >= nt_ref[0])
    def _():
        o_ref[...] = jnp.zeros(o_ref.shape, o_ref.dtype)


def _moe_grouped(tile_expert, n_tiles_used, xs, wg, wu, wd, *, tm, ff_chunk):
    n_slots, d = xs.shape
    d_ff = wg.shape[-1]
    grid_spec = pltpu.PrefetchScalarGridSpec(
        num_scalar_prefetch=2,
        grid=(n_slots // tm,),
        in_specs=[pl.BlockSpec((tm, d), lambda i, te, nt: (i, 0)),
                  pl.BlockSpec((None, d, d_ff), lambda i, te, nt: (te[i], 0, 0)),
                  pl.BlockSpec((None, d, d_ff), lambda i, te, nt: (te[i], 0, 0)),
                  pl.BlockSpec((None, d_ff, d), lambda i, te, nt: (te[i], 0, 0))],
        out_specs=pl.BlockSpec((tm, d), lambda i, te, nt: (i, 0)),
    )
    return pl.pallas_call(
        functools.partial(_moe_kernel, ff_chunk=ff_chunk),
        grid_spec=grid_spec,
        out_shape=jax.ShapeDtypeStruct((n_slots, d), F32),
        compiler_params=_params("arbitrary"),
        name="moe_grouped_swiglu",
    )(tile_expert, n_tiles_used, xs, wg, wu, wd)


def _combine_kernel(x_ref, ya_ref, yb_ref, rw_ref, g_ref, b_ref, o_ref, *, alpha):
    f = rw_ref[:, 0:1] * ya_ref[...] + rw_ref[:, 1:2] * yb_ref[...]
    o_ref[...] = _layer_norm(alpha * x_ref[...] + f, g_ref[...], b_ref[...])


def _moe_combine(x1, ya, yb, rw, ln_g, ln_b, *, alpha, tm, row0):
    n = rw.shape[0]
    d = x1.shape[1]
    shared = _row_spec(tm, d, row0 // tm)
    return pl.pallas_call(
        functools.partial(_combine_kernel, alpha=alpha),
        grid=(n // tm,),
        in_specs=[shared, shared, shared, _row_spec(tm, ROUTER_LANES),
                  _const_spec(ln_g.shape), _const_spec(ln_b.shape)],
        out_specs=_row_spec(tm, d),
        out_shape=jax.ShapeDtypeStruct((n, d), F32),
        compiler_params=_params("parallel"),
        name="moe_combine_ln2",
    )(x1, ya, yb, rw, ln_g, ln_b)


def _moe_route(top_i, n_experts, tm):
    n = top_i.shape[0]
    n_flat = TOP_K * n
    flat_e = top_i.reshape(-1)
    iota = jnp.arange(n_flat, dtype=jnp.int32)
    sorted_e, sorted_j = lax.sort((flat_e, iota), num_keys=1, is_stable=True)
    experts = jnp.arange(n_experts, dtype=jnp.int32)
    counts = jnp.sum((flat_e[:, None] == experts[None, :]).astype(jnp.int32), axis=0)
    start_end = jnp.cumsum(counts)
    padded = ((counts + tm - 1) // tm) * tm
    off_end = jnp.cumsum(padded)
    off = off_end - padded
    shift = off - (start_end - counts)
    n_slots = ((n_flat + n_experts * tm + tm - 1) // tm) * tm
    n_tiles = n_slots // tm
    tile_start = jnp.arange(n_tiles, dtype=jnp.int32) * tm
    tile_expert = jnp.sum((tile_start[:, None] >= off_end[None, :]).astype(jnp.int32), axis=1)
    tile_expert = jnp.minimum(tile_expert, n_experts - 1).astype(jnp.int32)
    n_used = (off_end[-1] // tm).astype(jnp.int32).reshape(1)
    slot = jnp.arange(n_slots, dtype=jnp.int32).reshape(n_tiles, tm)
    t_off, t_cnt, t_shift = off[tile_expert][:, None], counts[tile_expert][:, None], shift[tile_expert][:, None]
    valid = (slot - t_off) < t_cnt
    src = jnp.clip(slot - t_shift, 0, n_flat - 1)
    slot_tok = jnp.where(valid, sorted_j[src] // TOP_K, 0).reshape(n_slots)
    slot_of_sorted = iota + shift[sorted_e]
    _, slots = lax.sort((sorted_j, slot_of_sorted), num_keys=1)
    return slot_tok, tile_expert, n_used, slots.reshape(n, TOP_K)


def _pad_time_major(cache, steps):
    n_b, n_s, n_c = cache.shape
    tmaj = jnp.transpose(cache, (1, 0, 2))
    tmaj = jnp.pad(tmaj, ((steps - n_s, 0), (0, 0), (0, 0)))
    return tmaj.reshape(steps * n_b, n_c)


def _from_time_major(flat, n_b):
    n_c = flat.shape[1]
    return jnp.transpose(flat.reshape(-1, n_b, n_c), (1, 0, 2))


def kernel(x_prompt, x_sample, state_s5_re, state_s5_im, cache_pool, cache_conv, w_in, b_in, s5_a_re, s5_a_im, s5_log_dt, s5_b_re, s5_b_im, s5_c_re, s5_c_im, s5_d, s5_w_glu, s5_b_glu, pool_w, pool_scale, conv_w, conv_b, conv_ln_g, conv_ln_b, proj_s5, proj_pool, proj_conv, w_out, b_out, ln1_g, ln1_b, ln2_g, ln2_b, ffn_w_gate, ffn_w_up, ffn_w_down, moe_router, moe_w_gate, moe_w_up, moe_w_down):
    depth, d_model, _ = w_in.shape
    n_bp, n_tp, _ = x_prompt.shape
    n_bs, n_ts, _ = x_sample.shape
    assert n_ts == 1
    n_grp, n_p = s5_a_re.shape[1:]
    n_h = s5_b_re.shape[-1]
    n_state = n_grp * n_p
    d_s5 = n_grp * n_h
    d_pool = pool_scale.shape[1]
    d_conv = conv_w.shape[2]
    n_taps = conv_w.shape[1]
    d_lin = d_s5 + d_pool
    d_u = d_lin + 2 * d_conv
    n_exp = moe_router.shape[-1]
    alpha = (2.0 * depth) ** 0.25
    ff_chunk = MXU_DIM
    assert cache_pool.shape[2] == POOL_HIST - 1 and n_taps - 1 <= CONV_HIST
    n_p_tok = n_bp * n_tp
    n_tok = n_p_tok + n_bs
    tc_p = min(64, n_tp)
    tm_p = tc_p * n_bp

    def row(v):
        return v.reshape(1, -1).astype(F32)

    def both(w):
        w = w.astype(F32)
        return w.astype(BF16), w

    xp = jnp.transpose(x_prompt, (1, 0, 2)).reshape(n_p_tok, d_model)
    xs = x_sample.reshape(n_bs, d_model)
    zeros_p = dict(h0=jnp.zeros((n_bp, 2 * n_state), F32),
                   pool=jnp.zeros((POOL_HIST * n_bp, d_pool), F32),
                   conv=jnp.zeros((CONV_HIST * n_bp, d_conv), F32))
    outs = {k: [] for k in ("re_p", "im_p", "pool_p", "conv_p", "re_s", "im_s", "pool_s", "conv_s")}

    for l in range(depth):
        lam, bb_re, bb_im = _s5_discretise(s5_a_re[l], s5_a_im[l], s5_log_dt[l], s5_b_re[l], s5_b_im[l])
        conv_w_pad = jnp.pad(conv_w[l].astype(F32), ((0, CONV_HIST - n_taps), (0, 0)))
        shared_mw = dict(lam=lam, d=row(s5_d[l]), bglu=row(s5_b_glu[l]), pscale=row(pool_scale[l]),
                         conv_w=conv_w_pad, conv_b=row(conv_b[l]), conv_ln_g=row(conv_ln_g[l]),
                         conv_ln_b=row(conv_ln_b[l]), n_taps=n_taps)
        mats = dict(wb=both(_s5_input_blocks(bb_re, bb_im, n_grp, n_p)),
                    wc=both(_s5_output_blocks(s5_c_re[l].astype(F32), s5_c_im[l].astype(F32), n_p)),
                    wglu=both(s5_w_glu[l]), wpool=both(_block_diag(pool_w[l].astype(F32))))
        w1 = both(w_in[l, :, :d_u])
        b1 = row(b_in[l, :d_u])
        shared_lw = dict(b_gates=row(b_in[l, d_u:]), b_out=row(b_out[l]), ln1_g=row(ln1_g[l]), ln1_b=row(ln1_b[l]))
        lmats = dict(w_gates=both(w_in[l, :, d_u:]), proj_s5=both(proj_s5[l]), proj_pool=both(proj_pool[l]),
                     proj_conv=both(proj_conv[l]), w_out=both(w_out[l]))
        is_moe = l % 2 == 1
        router = None
        if is_moe:
            assert n_exp <= ROUTER_LANES // 2
            router = jnp.pad(moe_router[l // 2].astype(F32), ((0, 0), (0, ROUTER_LANES - n_exp)))

        mw = dict(shared_mw, **{k: v[0] for k, v in mats.items()})
        lw = dict(shared_lw, **{k: v[0] for k, v in lmats.items()})
        up = _inproj(xp, w1[0], b1, d_lin=d_lin, d_conv=d_conv, tm=tm_p, precise=False)
        yp, hT_p, pool_p, conv_p = _mixers(up, zeros_p["h0"], zeros_p["pool"], zeros_p["conv"], mw,
                                           nb=n_bp, tc=tc_p, start_pos=0, precise=False)
        res_p = _merge(xp, yp, lw, alpha=alpha, tm=tm_p, precise=False, router=router, n_exp=n_exp,
                       shared_rows=n_tok)
        mw = dict(shared_mw, **{k: v[1] for k, v in mats.items()})
        lw = dict(shared_lw, **{k: v[1] for k, v in lmats.items()})
        h0_s = jnp.concatenate([state_s5_re[l].reshape(n_bs, n_state), state_s5_im[l].reshape(n_bs, n_state)],
                               axis=1).astype(F32)
        us = _inproj(xs, w1[1], b1, d_lin=d_lin, d_conv=d_conv, tm=n_bs, precise=True)
        ys, hT_s, pool_s, conv_s = _mixers(us, h0_s, _pad_time_major(cache_pool[l].astype(F32), POOL_HIST),
                                           _pad_time_major(cache_conv[l].astype(F32), CONV_HIST), mw,
                                           nb=n_bs, tc=1, start_pos=PAST_LEN, precise=True)
        res_s = _merge(xs, ys, lw, alpha=alpha, tm=n_bs, precise=True, router=router, n_exp=n_exp,
                       shared=res_p[0] if is_moe else None, shared_rows=n_tok, row0=n_p_tok)

        outs["re_p"].append(hT_p[:, :n_state].reshape(n_bp, n_grp, n_p))
        outs["im_p"].append(hT_p[:, n_state:].reshape(n_bp, n_grp, n_p))
        outs["pool_p"].append(_from_time_major(pool_p, n_bp))
        outs["conv_p"].append(_from_time_major(conv_p, n_bp))
        outs["re_s"].append(hT_s[:, :n_state].reshape(n_bs, n_grp, n_p))
        outs["im_s"].append(hT_s[:, n_state:].reshape(n_bs, n_grp, n_p))
        outs["pool_s"].append(_from_time_major(pool_s, n_bs))
        outs["conv_s"].append(_from_time_major(conv_s, n_bs))

        g2, b2 = row(ln2_g[l]), row(ln2_b[l])
        i = l // 2
        if not is_moe:
            wg, wu, wd = both(ffn_w_gate[i]), both(ffn_w_up[i]), both(ffn_w_down[i])
            xp = _ffn_dense(res_p[0], wg[0], wu[0], wd[0], g2, b2, alpha=alpha, tm=tm_p, ff_chunk=ff_chunk,
                            precise=False)
            xs = _ffn_dense(res_s[0], wg[1], wu[1], wd[1], g2, b2, alpha=alpha, tm=n_bs, ff_chunk=ff_chunk,
                            precise=True)
        else:
            wg, wu, wd = moe_w_gate[i].astype(BF16), moe_w_up[i].astype(BF16), moe_w_down[i].astype(BF16)
            _, ri_p, rw_p = res_p
            x1_all, ri_s, rw_s = res_s
            top_i = jnp.concatenate([ri_p[:, :TOP_K], ri_s[:, :TOP_K]], axis=0)
            tm_moe = 512
            slot_tok, tile_expert, n_used, slots = _moe_route(top_i, n_exp, tm_moe)
            x_sorted = jnp.take(x1_all, slot_tok, axis=0)
            y_sorted = _moe_grouped(tile_expert, n_used, x_sorted, wg, wu, wd, tm=tm_moe, ff_chunk=ff_chunk)
            ya = jnp.take(y_sorted, slots[:, 0], axis=0)
            yb = jnp.take(y_sorted, slots[:, 1], axis=0)
            xp = _moe_combine(x1_all, ya, yb, rw_p, g2, b2, alpha=alpha, tm=tm_p, row0=0)
            xs = _moe_combine(x1_all, ya, yb, rw_s, g2, b2, alpha=alpha, tm=n_bs, row0=n_p_tok)

    y_prompt = _from_time_major(xp, n_bp)
    st = lambda k: jnp.stack(outs[k])
    return (y_prompt, xs.reshape(n_bs, 1, d_model), st("re_p"), st("im_p"), st("pool_p"), st("conv_p"),
            st("re_s"), st("im_s"), st("pool_s"), st("conv_s"))
```

```python
import functools

import jax
import jax.numpy as jnp
from jax import lax
from jax.experimental import pallas as pl
from jax.experimental.pallas import tpu as pltpu

F32 = jnp.float32
BF16 = jnp.bfloat16

LN_EPS = 1e-5
POOL_WINDOWS = (2, 4, 8, 16)
PAST_LEN = 16384
TOP_K = 2
POOL_HIST = 16
CONV_HIST = 32
ROUTER_LANES = 128
LANES = 128
MXU_DIM = 256
VMEM_LIMIT = 56 * 1024 * 1024


def _sigmoid(x):
    return 0.5 * jnp.tanh(0.5 * x) + 0.5


def _silu(x):
    return x * _sigmoid(x)


def _layer_norm(x, g, b):
    mu = jnp.mean(x, axis=-1, keepdims=True)
    xc = x - mu
    var = jnp.mean(xc * xc, axis=-1, keepdims=True)
    return xc * lax.rsqrt(var + LN_EPS) * g + b


def _dot(a, b):
    return jnp.dot(a, b, preferred_element_type=F32)


def _split(v):
    hi = v.astype(BF16)
    return hi, (v - hi.astype(F32)).astype(BF16)


def _mm(a, w, precise):
    if not precise:
        return _dot(a.astype(BF16), w.astype(BF16))
    a_hi, a_lo = _split(a)
    w_hi, w_lo = _split(w)
    m = a.shape[0]
    r = _dot(jnp.concatenate([a_hi, a_lo], axis=0), w_hi)
    return r[:m] + r[m:] + _dot(a_hi, w_lo)


def _params(*sem):
    return pltpu.CompilerParams(dimension_semantics=sem, vmem_limit_bytes=VMEM_LIMIT)


def _const_spec(shape):
    return pl.BlockSpec(shape, lambda *_: (0,) * len(shape), pipeline_mode=pl.Buffered(1))


def _row_spec(tm, d, blk0=0):
    return pl.BlockSpec((tm, d), lambda i: (blk0 + i, 0))


def _s5_disc_kernel(are_ref, aim_ref, ldt_ref, bre_ref, bim_ref, lam_ref, bbre_ref, bbim_ref):
    a_re = are_ref[...]
    a_im = aim_ref[...]
    dt = jnp.exp(ldt_ref[...])
    mag = jnp.exp(a_re * dt)
    lr = mag * jnp.cos(a_im * dt)
    li = mag * jnp.sin(a_im * dt)
    den = a_re * a_re + a_im * a_im
    nr = lr - 1.0
    cr = (nr * a_re + li * a_im) / den
    ci = (li * a_re - nr * a_im) / den
    lam_ref[0:1, :] = lr
    lam_ref[1:2, :] = li
    b_re = bre_ref[...]
    b_im = bim_ref[...]
    bbre_ref[...] = cr * b_re - ci * b_im
    bbim_ref[...] = cr * b_im + ci * b_re


def _s5_discretise(a_re, a_im, log_dt, b_re, b_im):
    n_g, n_p = a_re.shape
    n_h = b_re.shape[-1]
    n_s = n_g * n_p
    ldt = jnp.broadcast_to(log_dt[:, None], (n_g, n_p)).reshape(1, n_s)
    b_re_t = b_re.reshape(n_s, n_h).T
    b_im_t = b_im.reshape(n_s, n_h).T
    return pl.pallas_call(
        _s5_disc_kernel,
        out_shape=(jax.ShapeDtypeStruct((2, n_s), F32),
                   jax.ShapeDtypeStruct((n_h, n_s), F32),
                   jax.ShapeDtypeStruct((n_h, n_s), F32)),
        name="s5_discretise",
    )(a_re.reshape(1, n_s), a_im.reshape(1, n_s), ldt, b_re_t, b_im_t)


def _block_diag(blocks):
    n_g, r, c = blocks.shape
    eye = jnp.eye(n_g, dtype=blocks.dtype)
    return (eye[:, None, :, None] * blocks[:, :, None, :]).reshape(n_g * r, n_g * c)


def _s5_input_blocks(bb_re, bb_im, n_grp, n_p):
    n_h = bb_re.shape[0]
    gpb = MXU_DIM // n_p
    ch = gpb * n_h
    assert MXU_DIM % n_p == 0 and LANES % ch == 0 and n_grp % gpb == 0
    n_blk = n_grp // gpb
    per_chunk = LANES // ch

    def blocks(bb):
        g = jnp.transpose(bb.reshape(n_h, n_grp, n_p), (1, 0, 2)).reshape(n_blk, gpb, n_h, n_p)
        dense = jax.vmap(_block_diag)(g)
        return jnp.stack([jnp.pad(dense[j], ((ch * (j % per_chunk), LANES - ch * (j % per_chunk + 1)), (0, 0)))
                          for j in range(n_blk)])

    return jnp.concatenate([blocks(bb_re), blocks(bb_im)], axis=2)


def _s5_output_blocks(c_re, c_im, n_p):
    n_grp, n_h, _ = c_re.shape
    gpb = MXU_DIM // n_h
    assert n_grp % gpb == 0
    n_blk = n_grp // gpb

    def blocks(cm):
        g = jnp.transpose(cm, (0, 2, 1)).reshape(n_blk, gpb, n_p, n_h)
        return jax.vmap(_block_diag)(g)

    return jnp.concatenate([blocks(c_re), blocks(-c_im)], axis=1)


def _inproj_kernel(x_ref, w_ref, b_ref, o_ref, *, d_lin, d_conv, precise):
    u = _mm(x_ref[...], w_ref[...], precise) + b_ref[...]
    o_ref[:, :d_lin] = u[:, :d_lin]
    o_ref[:, d_lin:] = u[:, d_lin:d_lin + d_conv] * _sigmoid(u[:, d_lin + d_conv:])


def _inproj(x, w, b, *, d_lin, d_conv, tm, precise):
    n, d = x.shape
    d_out = d_lin + d_conv
    return pl.pallas_call(
        functools.partial(_inproj_kernel, d_lin=d_lin, d_conv=d_conv, precise=precise),
        grid=(n // tm,),
        in_specs=[_row_spec(tm, d), _const_spec(w.shape), _const_spec(b.shape)],
        out_specs=_row_spec(tm, d_out),
        out_shape=jax.ShapeDtypeStruct((n, d_out), F32),
        compiler_params=_params("parallel"),
        name="in_projection",
    )(x, w, b)


def _mixer_kernel(u_ref, h0_ref, pc_ref, cc_ref, wb_ref, wc_ref, lam_ref, d_ref, wglu_ref, bglu_ref,
                  wpool_ref, pscale_ref, cw_ref, cb_ref, clg_ref, clb_ref,
                  y_ref, hT_ref, pnew_ref, cnew_ref,
                  s_scr, h_scr, ph_scr, ch_scr, *, nb, tc, start_pos, d_s5, d_pool, d_conv, n_taps, precise):
    c = pl.program_id(0)
    last = pl.num_programs(0) - 1
    rows = tc * nb
    n_state = lam_ref.shape[1]
    o_pool = d_s5
    o_conv = d_s5 + d_pool

    @pl.when(c == 0)
    def _():
        h_scr[...] = h0_ref[...]
        ph_scr[0:POOL_HIST * nb, :] = pc_ref[...]
        ch_scr[0:CONV_HIST * nb, :] = cc_ref[...]

    u_s5 = u_ref[:, 0:d_s5]
    n_in_blk = wb_ref.shape[0]
    per_chunk = LANES // (d_s5 // n_in_blk)
    for j in range(n_in_blk):
        k0 = (j // per_chunk) * LANES
        bu = _mm(u_s5[:, k0:k0 + LANES], wb_ref[j], precise)
        s_scr[:, j * MXU_DIM:(j + 1) * MXU_DIM] = bu[:, :MXU_DIM]
        s_scr[:, n_state + j * MXU_DIM:n_state + (j + 1) * MXU_DIM] = bu[:, MXU_DIM:]
    if tc == 1:
        lr = lam_ref[0:1, :]
        li = lam_ref[1:2, :]
        h_re = h_scr[:, 0:n_state]
        h_im = h_scr[:, n_state:]
        n_re = lr * h_re - li * h_im + s_scr[:, 0:n_state]
        n_im = lr * h_im + li * h_re + s_scr[:, n_state:]
        s_scr[:, 0:n_state] = n_re
        s_scr[:, n_state:] = n_im
        h_scr[:, 0:n_state] = n_re
        h_scr[:, n_state:] = n_im
    else:
        assert nb == 8
        n_part = 2
        half = n_state // n_part
        for hf in range(n_part):
            re0 = hf * half
            im0 = n_state + hf * half
            lr = jnp.broadcast_to(lam_ref[0:1, re0:re0 + half], (nb, half))
            li = jnp.broadcast_to(lam_ref[1:2, re0:re0 + half], (nb, half))

            def step(t, carry, re0=re0, im0=im0, lr=lr, li=li):
                h_re, h_im = carry
                at_t = pl.ds(pl.multiple_of(t * nb, nb), nb)
                n_re = lr * h_re - li * h_im + s_scr[at_t, re0:re0 + half]
                n_im = lr * h_im + li * h_re + s_scr[at_t, im0:im0 + half]
                s_scr[at_t, re0:re0 + half] = n_re
                s_scr[at_t, im0:im0 + half] = n_im
                return n_re, n_im

            h_re, h_im = lax.fori_loop(0, tc, step, (h_scr[:, re0:re0 + half], h_scr[:, im0:im0 + half]),
                                       unroll=4)
            h_scr[:, re0:re0 + half] = h_re
            h_scr[:, im0:im0 + half] = h_im

    n_out_blk = wc_ref.shape[0]
    spb = n_state // n_out_blk
    ys = []
    for j in range(n_out_blk):
        h_re_b = s_scr[:, j * spb:(j + 1) * spb]
        h_im_b = s_scr[:, n_state + j * spb:n_state + (j + 1) * spb]
        ys.append(_mm(h_re_b, wc_ref[j, 0:spb, :], precise) + _mm(h_im_b, wc_ref[j, spb:, :], precise))
    y = jnp.concatenate(ys, axis=1) + d_ref[...] * u_s5
    y = jax.nn.gelu(y)
    y = y * _sigmoid(_mm(y, wglu_ref[...], precise) + bglu_ref[...])
    y_ref[:, 0:d_s5] = y.astype(y_ref.dtype)

    @pl.when(c == last)
    def _():
        hT_ref[...] = h_scr[...]

    u_pool = u_ref[:, o_pool:o_pool + d_pool]
    ph_scr[pl.ds(POOL_HIST * nb, rows), :] = u_pool
    a = ph_scr[...]
    row_t = lax.broadcasted_iota(jnp.int32, (rows, d_pool), 0) >> (nb.bit_length() - 1)
    pos1 = start_pos + c * tc + row_t + 1
    lane = lax.broadcasted_iota(jnp.int32, (rows, d_pool), 1)
    group = d_pool // len(POOL_WINDOWS)
    w = 1
    s_sel = None
    cnt_sel = None
    for gi, win in enumerate(POOL_WINDOWS):
        while w < win:
            a = a[w * nb:, :] + a[:a.shape[0] - w * nb, :]
            w *= 2
        assert w == win
        i0 = (POOL_HIST + 1 - win) * nb
        s_w = a[i0:i0 + rows, :]
        cnt_w = jnp.minimum(pos1, win).astype(F32)
        if s_sel is None:
            s_sel, cnt_sel = s_w, cnt_w
        else:
            in_later = lane >= gi * group
            s_sel = jnp.where(in_later, s_w, s_sel)
            cnt_sel = jnp.where(in_later, cnt_w, cnt_sel)
    pooled = s_sel / cnt_sel - u_pool
    y_pool = _mm(pooled, wpool_ref[...], precise) * pscale_ref[...]
    y_ref[:, o_pool:o_pool + d_pool] = y_pool.astype(y_ref.dtype)
    ph_scr[0:POOL_HIST * nb, :] = ph_scr[pl.ds(tc * nb, POOL_HIST * nb), :]

    @pl.when(c == last)
    def _():
        pnew_ref[...] = ph_scr[nb:POOL_HIST * nb, :]

    ch_scr[pl.ds(CONV_HIST * nb, rows), :] = u_ref[:, o_conv:o_conv + d_conv]
    rb = min(rows, 64)
    tap0 = CONV_HIST - (n_taps - 1)

    def conv_block(i, _):
        base = pl.multiple_of(i * rb, rb)
        acc = jnp.zeros((rb, d_conv), F32)
        for k in range(n_taps):
            acc = acc + cw_ref[k:k + 1, :] * ch_scr[pl.ds(base + (tap0 + k) * nb, rb), :]
        yc = _silu(_layer_norm(acc + cb_ref[...], clg_ref[...], clb_ref[...]))
        y_ref[pl.ds(base, rb), o_conv:o_conv + d_conv] = yc.astype(y_ref.dtype)
        return 0

    lax.fori_loop(0, rows // rb, conv_block, 0)
    ch_scr[0:CONV_HIST * nb, :] = ch_scr[pl.ds(tc * nb, CONV_HIST * nb), :]

    @pl.when(c == last)
    def _():
        cnew_ref[...] = ch_scr[tap0 * nb:CONV_HIST * nb, :]


def _mixers(u, h0, pool_cache, conv_cache, mw, *, nb, tc, start_pos, precise):
    n_rows, d_u = u.shape
    n_t = n_rows // nb
    rows = tc * nb
    d_s5 = mw["wglu"].shape[0]
    d_pool = mw["wpool"].shape[0]
    d_conv = mw["conv_w"].shape[1]
    n_taps = mw["n_taps"]
    d_y = d_s5 + d_pool + d_conv
    n_state2 = h0.shape[1]
    weights = [mw["wb"], mw["wc"], mw["lam"], mw["d"], mw["wglu"], mw["bglu"], mw["wpool"], mw["pscale"],
               mw["conv_w"], mw["conv_b"], mw["conv_ln_g"], mw["conv_ln_b"]]
    kern = functools.partial(_mixer_kernel, nb=nb, tc=tc, start_pos=start_pos, d_s5=d_s5, d_pool=d_pool,
                             d_conv=d_conv, n_taps=n_taps, precise=precise)
    return pl.pallas_call(
        kern,
        grid=(n_t // tc,),
        in_specs=[_row_spec(rows, d_u),
                  _const_spec(h0.shape), _const_spec(pool_cache.shape), _const_spec(conv_cache.shape)]
                 + [_const_spec(w.shape) for w in weights],
        out_specs=(_row_spec(rows, d_y),
                   pl.BlockSpec(h0.shape, lambda c: (0, 0)),
                   pl.BlockSpec(((POOL_HIST - 1) * nb, d_pool), lambda c: (0, 0)),
                   pl.BlockSpec(((n_taps - 1) * nb, d_conv), lambda c: (0, 0))),
        out_shape=(jax.ShapeDtypeStruct((n_rows, d_y), F32 if precise else BF16),
                   jax.ShapeDtypeStruct(h0.shape, F32),
                   jax.ShapeDtypeStruct(((POOL_HIST - 1) * nb, d_pool), F32),
                   jax.ShapeDtypeStruct(((n_taps - 1) * nb, d_conv), F32)),
        scratch_shapes=[pltpu.VMEM((rows, n_state2), F32),
                        pltpu.VMEM((nb, n_state2), F32),
                        pltpu.VMEM(((POOL_HIST + tc) * nb, d_pool), F32),
                        pltpu.VMEM(((CONV_HIST + tc) * nb, d_conv), F32)],
        compiler_params=_params("arbitrary"),
        name="token_mixers",
    )(u, h0, pool_cache, conv_cache, *weights)


def _merge_kernel(*refs, alpha, d_s5, d_pool, d_conv, n_exp, aliased, precise):
    with_router = n_exp > 0
    n_in = 11 + (1 if with_router else 0) + (1 if aliased else 0)
    x_ref, y_ref, wg_ref, bg_ref, ps5_ref, ppool_ref, pconv_ref, wo_ref, bo_ref, g_ref, b_ref = refs[:11]
    o_ref = refs[n_in]
    x = x_ref[...]
    xa = x if precise else x.astype(BF16)
    d = x.shape[-1]
    merged = None
    col = 0
    for j, (width, p_ref) in enumerate(((d_s5, ps5_ref), (d_pool, ppool_ref), (d_conv, pconv_ref))):
        gate = _sigmoid(_mm(xa, wg_ref[:, j * d:(j + 1) * d], precise) + bg_ref[:, j * d:(j + 1) * d])
        term = gate * _mm(y_ref[:, col:col + width], p_ref[...], precise)
        merged = term if merged is None else merged + term
        col += width
    out = _mm(merged, wo_ref[...], precise) + bo_ref[...]
    x1 = _layer_norm(alpha * x + out, g_ref[...], b_ref[...])
    o_ref[...] = x1
    if with_router:
        rt_ref = refs[11]
        ri_ref, rw_ref = refs[n_in + 1:n_in + 3]
        x_hi, x_lo = _split(x1)
        rt = rt_ref[...]
        rt_hi = rt.astype(BF16).astype(F32)
        rt2 = (rt_hi + pltpu.roll(rt - rt_hi, ROUTER_LANES // 2, axis=1)).astype(BF16)
        r = _dot(x_hi, rt2) + _dot(x_lo, rt2)
        logits = r + pltpu.roll(r, ROUTER_LANES // 2, axis=1)
        lane = lax.broadcasted_iota(jnp.int32, logits.shape, 1).astype(F32)
        neg = jnp.float32(-jnp.inf)
        l1 = jnp.where(lane < n_exp, logits, neg)
        m1 = jnp.max(l1, axis=-1, keepdims=True)
        i1 = jnp.min(jnp.where(l1 == m1, lane, float(ROUTER_LANES)), axis=-1, keepdims=True)
        l2 = jnp.where(lane == i1, neg, l1)
        m2 = jnp.max(l2, axis=-1, keepdims=True)
        i2 = jnp.min(jnp.where(l2 == m2, lane, float(ROUTER_LANES)), axis=-1, keepdims=True)
        e2 = jnp.exp(m2 - m1)
        w1 = 1.0 / (1.0 + e2)
        w2 = e2 * w1
        ri_ref[...] = jnp.where(lane == 0.0, i1, jnp.where(lane == 1.0, i2, 0.0)).astype(jnp.int32)
        rw_ref[...] = jnp.where(lane == 0.0, w1, jnp.where(lane == 1.0, w2, 0.0))


def _merge(x, y, lw, *, alpha, tm, precise, router=None, n_exp=0, shared=None, shared_rows=0, row0=0):
    n, d = x.shape
    d_y = y.shape[-1]
    d_s5, d_pool, d_conv = lw["proj_s5"].shape[0], lw["proj_pool"].shape[0], lw["proj_conv"].shape[0]
    inputs = [x, y, lw["w_gates"], lw["b_gates"], lw["proj_s5"], lw["proj_pool"], lw["proj_conv"], lw["w_out"],
              lw["b_out"], lw["ln1_g"], lw["ln1_b"]]
    in_specs = [_row_spec(tm, d), _row_spec(tm, d_y)] + [_const_spec(w.shape) for w in inputs[2:]]
    aliases = {}
    if router is None:
        out_specs = [_row_spec(tm, d)]
        out_shape = [jax.ShapeDtypeStruct((n, d), F32)]
    else:
        inputs.append(router)
        in_specs.append(_const_spec(router.shape))
        if shared is not None:
            aliases = {len(inputs): 0}
            inputs.append(shared)
            in_specs.append(pl.BlockSpec(memory_space=pl.ANY))
        out_specs = [_row_spec(tm, d, row0 // tm), _row_spec(tm, ROUTER_LANES), _row_spec(tm, ROUTER_LANES)]
        out_shape = [jax.ShapeDtypeStruct((shared_rows, d), F32),
                     jax.ShapeDtypeStruct((n, ROUTER_LANES), jnp.int32),
                     jax.ShapeDtypeStruct((n, ROUTER_LANES), F32)]
    kern = functools.partial(_merge_kernel, alpha=alpha, d_s5=d_s5, d_pool=d_pool, d_conv=d_conv,
                             n_exp=n_exp if router is not None else 0, aliased=shared is not None, precise=precise)
    return pl.pallas_call(
        kern,
        grid=(n // tm,),
        in_specs=in_specs,
        out_specs=tuple(out_specs),
        out_shape=tuple(out_shape),
        input_output_aliases=aliases,
        compiler_params=_params("parallel"),
        name="merge_outproj_ln1",
    )(*inputs)


def _swiglu_tile(x, wg_ref, wu_ref, wd_ref, ff_chunk, precise):
    d_ff = wg_ref.shape[-1]
    acc = None
    for c0 in range(0, d_ff, ff_chunk):
        c1 = min(c0 + ff_chunk, d_ff)
        g = _mm(x, wg_ref[:, c0:c1], precise)
        u = _mm(x, wu_ref[:, c0:c1], precise)
        part = _mm(_silu(g) * u, wd_ref[c0:c1, :], precise)
        acc = part if acc is None else acc + part
    return acc


def _ffn_kernel(x_ref, wg_ref, wu_ref, wd_ref, g_ref, b_ref, o_ref, *, alpha, ff_chunk, precise):
    x = x_ref[...]
    f = _swiglu_tile(x if precise else x.astype(BF16), wg_ref, wu_ref, wd_ref, ff_chunk, precise)
    o_ref[...] = _layer_norm(alpha * x + f, g_ref[...], b_ref[...])


def _ffn_dense(x, wg, wu, wd, ln_g, ln_b, *, alpha, tm, ff_chunk, precise):
    n, d = x.shape
    weights = [wg, wu, wd, ln_g, ln_b]
    return pl.pallas_call(
        functools.partial(_ffn_kernel, alpha=alpha, ff_chunk=ff_chunk, precise=precise),
        grid=(n // tm,),
        in_specs=[_row_spec(tm, d)] + [_const_spec(w.shape) for w in weights],
        out_specs=_row_spec(tm, d),
        out_shape=jax.ShapeDtypeStruct((n, d), F32),
        compiler_params=_params("parallel"),
        name="ffn_dense_ln2",
    )(x, *weights)


def _moe_kernel(te_ref, nt_ref, xs_ref, wg_ref, wu_ref, wd_ref, o_ref, *, ff_chunk):
    i = pl.program_id(0)

    @pl.when(i < nt_ref[0])
    def _():
        o_ref[...] = _swiglu_tile(xs_ref[...].astype(BF16), wg_ref, wu_ref, wd_ref, ff_chunk, False)

    @pl.when(i >= nt_ref[0])
    def _():
        o_ref[...] = jnp.zeros(o_ref.shape, o_ref.dtype)


def _moe_grouped(tile_expert, n_tiles_used, xs, wg, wu, wd, *, tm, ff_chunk):
    n_slots, d = xs.shape
    d_ff = wg.shape[-1]

    def expert_spec(shape):
        return pl.BlockSpec((None,) + shape, lambda i, te, nt: (te[i], 0, 0), pipeline_mode=pl.Buffered(1))

    grid_spec = pltpu.PrefetchScalarGridSpec(
        num_scalar_prefetch=2,
        grid=(n_slots // tm,),
        in_specs=[pl.BlockSpec((tm, d), lambda i, te, nt: (i, 0)),
                  expert_spec((d, d_ff)), expert_spec((d, d_ff)), expert_spec((d_ff, d))],
        out_specs=pl.BlockSpec((tm, d), lambda i, te, nt: (i, 0)),
    )
    return pl.pallas_call(
        functools.partial(_moe_kernel, ff_chunk=ff_chunk),
        grid_spec=grid_spec,
        out_shape=jax.ShapeDtypeStruct((n_slots, d), F32),
        compiler_params=_params("arbitrary"),
        name="moe_grouped_swiglu",
    )(tile_expert, n_tiles_used, xs, wg, wu, wd)


def _combine_kernel(x_ref, ya_ref, yb_ref, rw_ref, g_ref, b_ref, o_ref, *, alpha):
    f = rw_ref[:, 0:1] * ya_ref[...] + rw_ref[:, 1:2] * yb_ref[...]
    o_ref[...] = _layer_norm(alpha * x_ref[...] + f, g_ref[...], b_ref[...])


def _moe_combine(x1, ya, yb, rw, ln_g, ln_b, *, alpha, tm, row0):
    n = rw.shape[0]
    d = x1.shape[1]
    shared = _row_spec(tm, d, row0 // tm)
    return pl.pallas_call(
        functools.partial(_combine_kernel, alpha=alpha),
        grid=(n // tm,),
        in_specs=[shared, shared, shared, _row_spec(tm, ROUTER_LANES),
                  _const_spec(ln_g.shape), _const_spec(ln_b.shape)],
        out_specs=_row_spec(tm, d),
        out_shape=jax.ShapeDtypeStruct((n, d), F32),
        compiler_params=_params("parallel"),
        name="moe_combine_ln2",
    )(x1, ya, yb, rw, ln_g, ln_b)


def _moe_route(top_i, n_experts, tm):
    n = top_i.shape[0]
    n_flat = TOP_K * n
    flat_e = top_i.reshape(-1)
    iota = jnp.arange(n_flat, dtype=jnp.int32)
    sorted_e, sorted_j = lax.sort((flat_e, iota), num_keys=1, is_stable=True)
    experts = jnp.arange(n_experts, dtype=jnp.int32)
    counts = jnp.sum((flat_e[:, None] == experts[None, :]).astype(jnp.int32), axis=0)
    start_end = jnp.cumsum(counts)
    padded = ((counts + tm - 1) // tm) * tm
    off_end = jnp.cumsum(padded)
    off = off_end - padded
    shift = off - (start_end - counts)
    n_slots = ((n_flat + n_experts * tm + tm - 1) // tm) * tm
    n_tiles = n_slots // tm
    tile_start = jnp.arange(n_tiles, dtype=jnp.int32) * tm
    tile_expert = jnp.sum((tile_start[:, None] >= off_end[None, :]).astype(jnp.int32), axis=1)
    tile_expert = jnp.minimum(tile_expert, n_experts - 1).astype(jnp.int32)
    n_used = (off_end[-1] // tm).astype(jnp.int32).reshape(1)
    slot = jnp.arange(n_slots, dtype=jnp.int32).reshape(n_tiles, tm)
    t_off, t_cnt, t_shift = off[tile_expert][:, None], counts[tile_expert][:, None], shift[tile_expert][:, None]
    valid = (slot - t_off) < t_cnt
    src = jnp.clip(slot - t_shift, 0, n_flat - 1)
    slot_tok = jnp.where(valid, sorted_j[src] // TOP_K, 0).reshape(n_slots)
    slot_of_sorted = iota + shift[sorted_e]
    _, slots = lax.sort((sorted_j, slot_of_sorted), num_keys=1)
    return slot_tok, tile_expert, n_used, slots.reshape(n, TOP_K)


def _pad_time_major(cache, steps):
    n_b, n_s, n_c = cache.shape
    tmaj = jnp.transpose(cache, (1, 0, 2))
    tmaj = jnp.pad(tmaj, ((steps - n_s, 0), (0, 0), (0, 0)))
    return tmaj.reshape(steps * n_b, n_c)


def _from_time_major(flat, n_b):
    n_c = flat.shape[1]
    return jnp.transpose(flat.reshape(-1, n_b, n_c), (1, 0, 2))


def kernel(x_prompt, x_sample, state_s5_re, state_s5_im, cache_pool, cache_conv, w_in, b_in, s5_a_re, s5_a_im, s5_log_dt, s5_b_re, s5_b_im, s5_c_re, s5_c_im, s5_d, s5_w_glu, s5_b_glu, pool_w, pool_scale, conv_w, conv_b, conv_ln_g, conv_ln_b, proj_s5, proj_pool, proj_conv, w_out, b_out, ln1_g, ln1_b, ln2_g, ln2_b, ffn_w_gate, ffn_w_up, ffn_w_down, moe_router, moe_w_gate, moe_w_up, moe_w_down):
    depth, d_model, _ = w_in.shape
    n_bp, n_tp, _ = x_prompt.shape
    n_bs, n_ts, _ = x_sample.shape
    assert n_ts == 1
    n_grp, n_p = s5_a_re.shape[1:]
    n_h = s5_b_re.shape[-1]
    n_state = n_grp * n_p
    d_s5 = n_grp * n_h
    d_pool = pool_scale.shape[1]
    d_conv = conv_w.shape[2]
    n_taps = conv_w.shape[1]
    d_lin = d_s5 + d_pool
    d_u = d_lin + 2 * d_conv
    n_exp = moe_router.shape[-1]
    alpha = (2.0 * depth) ** 0.25
    ff_chunk = MXU_DIM
    assert cache_pool.shape[2] == POOL_HIST - 1 and n_taps - 1 <= CONV_HIST
    n_p_tok = n_bp * n_tp
    n_tok = n_p_tok + n_bs
    tc_p = min(64, n_tp)
    tm_p = tc_p * n_bp

    def row(v):
        return v.reshape(1, -1).astype(F32)

    def both(w):
        w = w.astype(F32)
        return w.astype(BF16), w

    xp = jnp.transpose(x_prompt, (1, 0, 2)).reshape(n_p_tok, d_model)
    xs = x_sample.reshape(n_bs, d_model)
    zeros_p = dict(h0=jnp.zeros((n_bp, 2 * n_state), F32),
                   pool=jnp.zeros((POOL_HIST * n_bp, d_pool), F32),
                   conv=jnp.zeros((CONV_HIST * n_bp, d_conv), F32))
    outs = {k: [] for k in ("re_p", "im_p", "pool_p", "conv_p", "re_s", "im_s", "pool_s", "conv_s")}

    for l in range(depth):
        lam, bb_re, bb_im = _s5_discretise(s5_a_re[l], s5_a_im[l], s5_log_dt[l], s5_b_re[l], s5_b_im[l])
        conv_w_pad = jnp.pad(conv_w[l].astype(F32), ((0, CONV_HIST - n_taps), (0, 0)))
        shared_mw = dict(lam=lam, d=row(s5_d[l]), bglu=row(s5_b_glu[l]), pscale=row(pool_scale[l]),
                         conv_w=conv_w_pad, conv_b=row(conv_b[l]), conv_ln_g=row(conv_ln_g[l]),
                         conv_ln_b=row(conv_ln_b[l]), n_taps=n_taps)
        mats = dict(wb=both(_s5_input_blocks(bb_re, bb_im, n_grp, n_p)),
                    wc=both(_s5_output_blocks(s5_c_re[l].astype(F32), s5_c_im[l].astype(F32), n_p)),
                    wglu=both(s5_w_glu[l]), wpool=both(_block_diag(pool_w[l].astype(F32))))
        w1 = both(w_in[l, :, :d_u])
        b1 = row(b_in[l, :d_u])
        shared_lw = dict(b_gates=row(b_in[l, d_u:]), b_out=row(b_out[l]), ln1_g=row(ln1_g[l]), ln1_b=row(ln1_b[l]))
        lmats = dict(w_gates=both(w_in[l, :, d_u:]), proj_s5=both(proj_s5[l]), proj_pool=both(proj_pool[l]),
                     proj_conv=both(proj_conv[l]), w_out=both(w_out[l]))
        is_moe = l % 2 == 1
        router = None
        if is_moe:
            assert n_exp <= ROUTER_LANES // 2
            router = jnp.pad(moe_router[l // 2].astype(F32), ((0, 0), (0, ROUTER_LANES - n_exp)))

        mw = dict(shared_mw, **{k: v[0] for k, v in mats.items()})
        lw = dict(shared_lw, **{k: v[0] for k, v in lmats.items()})
        up = _inproj(xp, w1[0], b1, d_lin=d_lin, d_conv=d_conv, tm=tm_p, precise=False)
        yp, hT_p, pool_p, conv_p = _mixers(up, zeros_p["h0"], zeros_p["pool"], zeros_p["conv"], mw,
                                           nb=n_bp, tc=tc_p, start_pos=0, precise=False)
        res_p = _merge(xp, yp, lw, alpha=alpha, tm=tm_p, precise=False, router=router, n_exp=n_exp,
                       shared_rows=n_tok)
        mw = dict(shared_mw, **{k: v[1] for k, v in mats.items()})
        lw = dict(shared_lw, **{k: v[1] for k, v in lmats.items()})
        h0_s = jnp.concatenate([state_s5_re[l].reshape(n_bs, n_state), state_s5_im[l].reshape(n_bs, n_state)],
                               axis=1).astype(F32)
        us = _inproj(xs, w1[1], b1, d_lin=d_lin, d_conv=d_conv, tm=n_bs, precise=True)
        ys, hT_s, pool_s, conv_s = _mixers(us, h0_s, _pad_time_major(cache_pool[l].astype(F32), POOL_HIST),
                                           _pad_time_major(cache_conv[l].astype(F32), CONV_HIST), mw,
                                           nb=n_bs, tc=1, start_pos=PAST_LEN, precise=True)
        res_s = _merge(xs, ys, lw, alpha=alpha, tm=n_bs, precise=True, router=router, n_exp=n_exp,
                       shared=res_p[0] if is_moe else None, shared_rows=n_tok, row0=n_p_tok)

        outs["re_p"].append(hT_p[:, :n_state].reshape(n_bp, n_grp, n_p))
        outs["im_p"].append(hT_p[:, n_state:].reshape(n_bp, n_grp, n_p))
        outs["pool_p"].append(_from_time_major(pool_p, n_bp))
        outs["conv_p"].append(_from_time_major(conv_p, n_bp))
        outs["re_s"].append(hT_s[:, :n_state].reshape(n_bs, n_grp, n_p))
        outs["im_s"].append(hT_s[:, n_state:].reshape(n_bs, n_grp, n_p))
        outs["pool_s"].append(_from_time_major(pool_s, n_bs))
        outs["conv_s"].append(_from_time_major(conv_s, n_bs))

        g2, b2 = row(ln2_g[l]), row(ln2_b[l])
        i = l // 2
        if not is_moe:
            wg, wu, wd = ffn_w_gate[i].astype(F32), ffn_w_up[i].astype(F32), ffn_w_down[i].astype(F32)
            xp = _ffn_dense(res_p[0], wg, wu, wd, g2, b2, alpha=alpha, tm=tm_p, ff_chunk=ff_chunk,
                            precise=False)
            xs = _ffn_dense(res_s[0], wg, wu, wd, g2, b2, alpha=alpha, tm=n_bs, ff_chunk=ff_chunk,
                            precise=True)
        else:
            wg, wu, wd = moe_w_gate[i].astype(F32), moe_w_up[i].astype(F32), moe_w_down[i].astype(F32)
            _, ri_p, rw_p = res_p
            x1_all, ri_s, rw_s = res_s
            top_i = jnp.concatenate([ri_p[:, :TOP_K], ri_s[:, :TOP_K]], axis=0)
            tm_moe = 512
            slot_tok, tile_expert, n_used, slots = _moe_route(top_i, n_exp, tm_moe)
            x_sorted = x1_all.at[slot_tok].get(mode="promise_in_bounds")
            y_sorted = _moe_grouped(tile_expert, n_used, x_sorted, wg, wu, wd, tm=tm_moe, ff_chunk=ff_chunk)
            ya = y_sorted.at[slots[:, 0]].get(mode="promise_in_bounds")
            yb = y_sorted.at[slots[:, 1]].get(mode="promise_in_bounds")
            xp = _moe_combine(x1_all, ya, yb, rw_p, g2, b2, alpha=alpha, tm=tm_p, row0=0)
            xs = _moe_combine(x1_all, ya, yb, rw_s, g2, b2, alpha=alpha, tm=n_bs, row0=n_p_tok)

    y_prompt = _from_time_major(xp, n_bp)
    st = lambda k: jnp.stack(outs[k])
    return (y_prompt, xs.reshape(n_bs, 1, d_model), st("re_p"), st("im_p"), st("pool_p"), st("conv_p"),
            st("re_s"), st("im_s"), st("pool_s"), st("conv_s"))
```

```python
import functools

import jax
import jax.numpy as jnp
from jax import lax
from jax.experimental import pallas as pl
from jax.experimental.pallas import tpu as pltpu

F32 = jnp.float32
BF16 = jnp.bfloat16

LN_EPS = 1e-5
POOL_WINDOWS = (2, 4, 8, 16)
PAST_LEN = 16384
TOP_K = 2
POOL_HIST = 16
CONV_HIST = 32
ROUTER_LANES = 128
LANES = 128
MXU_DIM = 256
VMEM_LIMIT = 56 * 1024 * 1024


def _sigmoid(x):
    return 0.5 * jnp.tanh(0.5 * x) + 0.5


def _silu(x):
    return x * _sigmoid(x)


def _layer_norm(x, g, b):
    mu = jnp.mean(x, axis=-1, keepdims=True)
    xc = x - mu
    var = jnp.mean(xc * xc, axis=-1, keepdims=True)
    return xc * lax.rsqrt(var + LN_EPS) * g + b


def _dot(a, b, w_transposed=False):
    dims = (((1,), (1 if w_transposed else 0,)), ((), ()))
    return lax.dot_general(a, b, dims, preferred_element_type=F32)


def _split(v):
    hi = v.astype(BF16)
    return hi, (v - hi.astype(F32)).astype(BF16)


def _mm(a, w, precise, w_transposed=False):
    if not precise:
        return _dot(a.astype(BF16), w.astype(BF16), w_transposed)
    a_hi, a_lo = _split(a)
    w_hi, w_lo = _split(w)
    m = a.shape[0]
    r = _dot(jnp.concatenate([a_hi, a_lo], axis=0), w_hi, w_transposed)
    return r[:m] + r[m:] + _dot(a_hi, w_lo, w_transposed)


def _params(*sem):
    return pltpu.CompilerParams(dimension_semantics=sem, vmem_limit_bytes=VMEM_LIMIT)


def _const_spec(shape):
    return pl.BlockSpec(shape, lambda *_: (0,) * len(shape), pipeline_mode=pl.Buffered(1))


def _row_spec(tm, d, blk0=0):
    return pl.BlockSpec((tm, d), lambda i: (blk0 + i, 0))


def _s5_disc_kernel(are_ref, aim_ref, ldt_ref, bre_ref, bim_ref, lam_ref, bbre_ref, bbim_ref):
    a_re = are_ref[...]
    a_im = aim_ref[...]
    dt = jnp.exp(ldt_ref[...])
    mag = jnp.exp(a_re * dt)
    lr = mag * jnp.cos(a_im * dt)
    li = mag * jnp.sin(a_im * dt)
    den = a_re * a_re + a_im * a_im
    nr = lr - 1.0
    cr = (nr * a_re + li * a_im) / den
    ci = (li * a_re - nr * a_im) / den
    lam_ref[0:1, :] = lr
    lam_ref[1:2, :] = li
    b_re = bre_ref[...]
    b_im = bim_ref[...]
    bbre_ref[...] = cr * b_re - ci * b_im
    bbim_ref[...] = cr * b_im + ci * b_re


def _s5_discretise(a_re, a_im, log_dt, b_re, b_im):
    n_g, n_p = a_re.shape
    n_h = b_re.shape[-1]
    n_s = n_g * n_p
    ldt = jnp.broadcast_to(log_dt[:, None], (n_g, n_p)).reshape(1, n_s)
    b_re_t = b_re.reshape(n_s, n_h).T
    b_im_t = b_im.reshape(n_s, n_h).T
    return pl.pallas_call(
        _s5_disc_kernel,
        out_shape=(jax.ShapeDtypeStruct((2, n_s), F32),
                   jax.ShapeDtypeStruct((n_h, n_s), F32),
                   jax.ShapeDtypeStruct((n_h, n_s), F32)),
        name="s5_discretise",
    )(a_re.reshape(1, n_s), a_im.reshape(1, n_s), ldt, b_re_t, b_im_t)


def _block_diag(blocks):
    n_g, r, c = blocks.shape
    eye = jnp.eye(n_g, dtype=blocks.dtype)
    return (eye[:, None, :, None] * blocks[:, :, None, :]).reshape(n_g * r, n_g * c)


def _s5_input_blocks(bb_re, bb_im, n_grp, n_p):
    n_h = bb_re.shape[0]
    gpb = MXU_DIM // n_p
    ch = gpb * n_h
    assert MXU_DIM % n_p == 0 and LANES % ch == 0 and n_grp % gpb == 0
    n_blk = n_grp // gpb
    per_chunk = LANES // ch

    def blocks(bb):
        g = jnp.transpose(bb.reshape(n_h, n_grp, n_p), (1, 0, 2)).reshape(n_blk, gpb, n_h, n_p)
        dense = jax.vmap(_block_diag)(g)
        return jnp.stack([jnp.pad(dense[j], ((ch * (j % per_chunk), LANES - ch * (j % per_chunk + 1)), (0, 0)))
                          for j in range(n_blk)])

    return jnp.concatenate([blocks(bb_re), blocks(bb_im)], axis=2)


def _s5_output_blocks(c_re, c_im, n_p):
    n_grp, n_h, _ = c_re.shape
    gpb = MXU_DIM // n_h
    assert n_grp % gpb == 0
    n_blk = n_grp // gpb

    def blocks(cm):
        g = jnp.transpose(cm, (0, 2, 1)).reshape(n_blk, gpb, n_p, n_h)
        return jax.vmap(_block_diag)(g)

    return jnp.concatenate([blocks(c_re), blocks(-c_im)], axis=1)


def _inproj_kernel(x_ref, w_ref, b_ref, o_ref, *, d_lin, d_conv, precise):
    u = _mm(x_ref[...], w_ref[...], precise) + b_ref[...]
    o_ref[:, :d_lin] = u[:, :d_lin]
    o_ref[:, d_lin:] = u[:, d_lin:d_lin + d_conv] * _sigmoid(u[:, d_lin + d_conv:])


def _inproj(x, w, b, *, d_lin, d_conv, tm, precise):
    n, d = x.shape
    d_out = d_lin + d_conv
    return pl.pallas_call(
        functools.partial(_inproj_kernel, d_lin=d_lin, d_conv=d_conv, precise=precise),
        grid=(n // tm,),
        in_specs=[_row_spec(tm, d), _const_spec(w.shape), _const_spec(b.shape)],
        out_specs=_row_spec(tm, d_out),
        out_shape=jax.ShapeDtypeStruct((n, d_out), F32),
        compiler_params=_params("parallel"),
        name="in_projection",
    )(x, w, b)


def _mixer_kernel(u_ref, h0_ref, pc_ref, cc_ref, wb_ref, wc_ref, lam_ref, d_ref, wglu_ref, bglu_ref,
                  wpool_ref, pscale_ref, cw_ref, cb_ref, clg_ref, clb_ref,
                  y_ref, hT_ref, pnew_ref, cnew_ref,
                  s_scr, h_scr, ph_scr, ch_scr, ca_scr, *, nb, tc, start_pos, d_s5, d_pool, d_conv, n_taps,
                  precise):
    c = pl.program_id(0)
    last = pl.num_programs(0) - 1
    rows = tc * nb
    n_state = lam_ref.shape[1]
    o_pool = d_s5
    o_conv = d_s5 + d_pool

    @pl.when(c == 0)
    def _():
        h_scr[...] = h0_ref[...]
        ph_scr[0:POOL_HIST * nb, :] = pc_ref[...]
        ch_scr[0:CONV_HIST * nb, :] = cc_ref[...]

    u_s5 = u_ref[:, 0:d_s5]
    n_in_blk = wb_ref.shape[0]
    per_chunk = LANES // (d_s5 // n_in_blk)
    for j in range(n_in_blk):
        k0 = (j // per_chunk) * LANES
        bu = _mm(u_s5[:, k0:k0 + LANES], wb_ref[j], precise)
        s_scr[:, j * MXU_DIM:(j + 1) * MXU_DIM] = bu[:, :MXU_DIM]
        s_scr[:, n_state + j * MXU_DIM:n_state + (j + 1) * MXU_DIM] = bu[:, MXU_DIM:]
    if tc == 1:
        lr = lam_ref[0:1, :]
        li = lam_ref[1:2, :]
        h_re = h_scr[:, 0:n_state]
        h_im = h_scr[:, n_state:]
        n_re = lr * h_re - li * h_im + s_scr[:, 0:n_state]
        n_im = lr * h_im + li * h_re + s_scr[:, n_state:]
        s_scr[:, 0:n_state] = n_re
        s_scr[:, n_state:] = n_im
        h_scr[:, 0:n_state] = n_re
        h_scr[:, n_state:] = n_im
    else:
        assert nb == 8
        n_part = 2
        half = n_state // n_part
        for hf in range(n_part):
            re0 = hf * half
            im0 = n_state + hf * half
            lr = jnp.broadcast_to(lam_ref[0:1, re0:re0 + half], (nb, half))
            li = jnp.broadcast_to(lam_ref[1:2, re0:re0 + half], (nb, half))

            def step(t, carry, re0=re0, im0=im0, lr=lr, li=li):
                h_re, h_im = carry
                at_t = pl.ds(pl.multiple_of(t * nb, nb), nb)
                n_re = lr * h_re - li * h_im + s_scr[at_t, re0:re0 + half]
                n_im = lr * h_im + li * h_re + s_scr[at_t, im0:im0 + half]
                s_scr[at_t, re0:re0 + half] = n_re
                s_scr[at_t, im0:im0 + half] = n_im
                return n_re, n_im

            h_re, h_im = lax.fori_loop(0, tc, step, (h_scr[:, re0:re0 + half], h_scr[:, im0:im0 + half]),
                                       unroll=4)
            h_scr[:, re0:re0 + half] = h_re
            h_scr[:, im0:im0 + half] = h_im

    n_out_blk = wc_ref.shape[0]
    spb = n_state // n_out_blk
    ys = []
    for j in range(n_out_blk):
        h_re_b = s_scr[:, j * spb:(j + 1) * spb]
        h_im_b = s_scr[:, n_state + j * spb:n_state + (j + 1) * spb]
        ys.append(_mm(h_re_b, wc_ref[j, 0:spb, :], precise) + _mm(h_im_b, wc_ref[j, spb:, :], precise))
    y = jnp.concatenate(ys, axis=1) + d_ref[...] * u_s5
    y = jax.nn.gelu(y)
    y = y * _sigmoid(_mm(y, wglu_ref[...], precise) + bglu_ref[...])
    y_ref[:, 0:d_s5] = y.astype(y_ref.dtype)

    @pl.when(c == last)
    def _():
        hT_ref[...] = h_scr[...]

    u_pool = u_ref[:, o_pool:o_pool + d_pool]
    ph_scr[pl.ds(POOL_HIST * nb, rows), :] = u_pool
    a = ph_scr[...]
    row_t = lax.broadcasted_iota(jnp.int32, (rows, d_pool), 0) >> (nb.bit_length() - 1)
    pos1 = start_pos + c * tc + row_t + 1
    lane = lax.broadcasted_iota(jnp.int32, (rows, d_pool), 1)
    group = d_pool // len(POOL_WINDOWS)
    w = 1
    s_sel = None
    cnt_sel = None
    for gi, win in enumerate(POOL_WINDOWS):
        while w < win:
            a = a[w * nb:, :] + a[:a.shape[0] - w * nb, :]
            w *= 2
        assert w == win
        i0 = (POOL_HIST + 1 - win) * nb
        s_w = a[i0:i0 + rows, :]
        cnt_w = jnp.minimum(pos1, win).astype(F32)
        if s_sel is None:
            s_sel, cnt_sel = s_w, cnt_w
        else:
            in_later = lane >= gi * group
            s_sel = jnp.where(in_later, s_w, s_sel)
            cnt_sel = jnp.where(in_later, cnt_w, cnt_sel)
    pooled = s_sel / cnt_sel - u_pool
    y_pool = _mm(pooled, wpool_ref[...], precise) * pscale_ref[...]
    y_ref[:, o_pool:o_pool + d_pool] = y_pool.astype(y_ref.dtype)
    ph_scr[0:POOL_HIST * nb, :] = ph_scr[pl.ds(tc * nb, POOL_HIST * nb), :]

    @pl.when(c == last)
    def _():
        pnew_ref[...] = ph_scr[nb:POOL_HIST * nb, :]

    ch_scr[pl.ds(CONV_HIST * nb, rows), :] = u_ref[:, o_conv:o_conv + d_conv]
    rb = min(rows, 64)
    tap0 = CONV_HIST - (n_taps - 1)

    def conv_block(i, _):
        base = pl.multiple_of(i * rb, rb)
        acc = [None, None]
        for k in range(n_taps):
            term = cw_ref[k:k + 1, :] * ch_scr[pl.ds(base + (tap0 + k) * nb, rb), :]
            acc[k % 2] = term if acc[k % 2] is None else acc[k % 2] + term
        ca_scr[pl.ds(base, rb), :] = acc[0] + acc[1]
        return 0

    lax.fori_loop(0, rows // rb, conv_block, 0)
    yc = _silu(_layer_norm(ca_scr[...] + cb_ref[...], clg_ref[...], clb_ref[...]))
    y_ref[:, o_conv:o_conv + d_conv] = yc.astype(y_ref.dtype)
    ch_scr[0:CONV_HIST * nb, :] = ch_scr[pl.ds(tc * nb, CONV_HIST * nb), :]

    @pl.when(c == last)
    def _():
        cnew_ref[...] = ch_scr[tap0 * nb:CONV_HIST * nb, :]


def _mixers(u, h0, pool_cache, conv_cache, mw, *, nb, tc, start_pos, precise):
    n_rows, d_u = u.shape
    n_t = n_rows // nb
    rows = tc * nb
    d_s5 = mw["wglu"].shape[0]
    d_pool = mw["wpool"].shape[0]
    d_conv = mw["conv_w"].shape[1]
    n_taps = mw["n_taps"]
    d_y = d_s5 + d_pool + d_conv
    n_state2 = h0.shape[1]
    weights = [mw["wb"], mw["wc"], mw["lam"], mw["d"], mw["wglu"], mw["bglu"], mw["wpool"], mw["pscale"],
               mw["conv_w"], mw["conv_b"], mw["conv_ln_g"], mw["conv_ln_b"]]
    kern = functools.partial(_mixer_kernel, nb=nb, tc=tc, start_pos=start_pos, d_s5=d_s5, d_pool=d_pool,
                             d_conv=d_conv, n_taps=n_taps, precise=precise)
    return pl.pallas_call(
        kern,
        grid=(n_t // tc,),
        in_specs=[_row_spec(rows, d_u),
                  _const_spec(h0.shape), _const_spec(pool_cache.shape), _const_spec(conv_cache.shape)]
                 + [_const_spec(w.shape) for w in weights],
        out_specs=(_row_spec(rows, d_y),
                   pl.BlockSpec(h0.shape, lambda c: (0, 0)),
                   pl.BlockSpec(((POOL_HIST - 1) * nb, d_pool), lambda c: (0, 0)),
                   pl.BlockSpec(((n_taps - 1) * nb, d_conv), lambda c: (0, 0))),
        out_shape=(jax.ShapeDtypeStruct((n_rows, d_y), F32 if precise else BF16),
                   jax.ShapeDtypeStruct(h0.shape, F32),
                   jax.ShapeDtypeStruct(((POOL_HIST - 1) * nb, d_pool), F32),
                   jax.ShapeDtypeStruct(((n_taps - 1) * nb, d_conv), F32)),
        scratch_shapes=[pltpu.VMEM((rows, n_state2), F32),
                        pltpu.VMEM((nb, n_state2), F32),
                        pltpu.VMEM(((POOL_HIST + tc) * nb, d_pool), F32),
                        pltpu.VMEM(((CONV_HIST + tc) * nb, d_conv), F32),
                        pltpu.VMEM((rows, d_conv), F32)],
        compiler_params=_params("arbitrary"),
        name="token_mixers",
    )(u, h0, pool_cache, conv_cache, *weights)


def _merge_kernel(*refs, alpha, d_s5, d_pool, d_conv, n_exp, aliased, precise):
    with_router = n_exp > 0
    n_in = 11 + (1 if with_router else 0) + (1 if aliased else 0)
    x_ref, y_ref, wg_ref, bg_ref, ps5_ref, ppool_ref, pconv_ref, wo_ref, bo_ref, g_ref, b_ref = refs[:11]
    o_ref = refs[n_in]
    x = x_ref[...]
    xa = x if precise else x.astype(BF16)
    d = x.shape[-1]
    merged = None
    col = 0
    for j, (width, p_ref) in enumerate(((d_s5, ps5_ref), (d_pool, ppool_ref), (d_conv, pconv_ref))):
        gate = _sigmoid(_mm(xa, wg_ref[:, j * d:(j + 1) * d], precise) + bg_ref[:, j * d:(j + 1) * d])
        term = gate * _mm(y_ref[:, col:col + width], p_ref[...], precise)
        merged = term if merged is None else merged + term
        col += width
    out = _mm(merged, wo_ref[...], precise) + bo_ref[...]
    x1 = _layer_norm(alpha * x + out, g_ref[...], b_ref[...])
    o_ref[...] = x1
    if with_router:
        rt_ref = refs[11]
        ri_ref, rw_ref = refs[n_in + 1:n_in + 3]
        x_hi, x_lo = _split(x1)
        rt = rt_ref[...]
        rt_hi = rt.astype(BF16).astype(F32)
        rt2 = (rt_hi + pltpu.roll(rt - rt_hi, ROUTER_LANES // 2, axis=1)).astype(BF16)
        r = _dot(x_hi, rt2) + _dot(x_lo, rt2)
        logits = r + pltpu.roll(r, ROUTER_LANES // 2, axis=1)
        lane = lax.broadcasted_iota(jnp.int32, logits.shape, 1).astype(F32)
        neg = jnp.float32(-jnp.inf)
        l1 = jnp.where(lane < n_exp, logits, neg)
        m1 = jnp.max(l1, axis=-1, keepdims=True)
        i1 = jnp.min(jnp.where(l1 == m1, lane, float(ROUTER_LANES)), axis=-1, keepdims=True)
        l2 = jnp.where(lane == i1, neg, l1)
        m2 = jnp.max(l2, axis=-1, keepdims=True)
        i2 = jnp.min(jnp.where(l2 == m2, lane, float(ROUTER_LANES)), axis=-1, keepdims=True)
        e2 = jnp.exp(m2 - m1)
        w1 = 1.0 / (1.0 + e2)
        w2 = e2 * w1
        ri_ref[...] = jnp.where(lane == 0.0, i1, jnp.where(lane == 1.0, i2, 0.0)).astype(jnp.int32)
        rw_ref[...] = jnp.where(lane == 0.0, w1, jnp.where(lane == 1.0, w2, 0.0))


def _merge(x, y, lw, *, alpha, tm, precise, router=None, n_exp=0, shared=None, shared_rows=0, row0=0):
    n, d = x.shape
    d_y = y.shape[-1]
    d_s5, d_pool, d_conv = lw["proj_s5"].shape[0], lw["proj_pool"].shape[0], lw["proj_conv"].shape[0]
    inputs = [x, y, lw["w_gates"], lw["b_gates"], lw["proj_s5"], lw["proj_pool"], lw["proj_conv"], lw["w_out"],
              lw["b_out"], lw["ln1_g"], lw["ln1_b"]]
    in_specs = [_row_spec(tm, d), _row_spec(tm, d_y)] + [_const_spec(w.shape) for w in inputs[2:]]
    aliases = {}
    if router is None:
        out_specs = [_row_spec(tm, d)]
        out_shape = [jax.ShapeDtypeStruct((n, d), F32)]
    else:
        inputs.append(router)
        in_specs.append(_const_spec(router.shape))
        if shared is not None:
            aliases = {len(inputs): 0}
            inputs.append(shared)
            in_specs.append(pl.BlockSpec(memory_space=pl.ANY))
        out_specs = [_row_spec(tm, d, row0 // tm), _row_spec(tm, ROUTER_LANES), _row_spec(tm, ROUTER_LANES)]
        out_shape = [jax.ShapeDtypeStruct((shared_rows, d), F32),
                     jax.ShapeDtypeStruct((n, ROUTER_LANES), jnp.int32),
                     jax.ShapeDtypeStruct((n, ROUTER_LANES), F32)]
    kern = functools.partial(_merge_kernel, alpha=alpha, d_s5=d_s5, d_pool=d_pool, d_conv=d_conv,
                             n_exp=n_exp if router is not None else 0, aliased=shared is not None, precise=precise)
    return pl.pallas_call(
        kern,
        grid=(n // tm,),
        in_specs=in_specs,
        out_specs=tuple(out_specs),
        out_shape=tuple(out_shape),
        input_output_aliases=aliases,
        compiler_params=_params("parallel"),
        name="merge_outproj_ln1",
    )(*inputs)


def _swiglu_tile(x, wg_ref, wu_ref, wd_ref, ff_chunk, precise):
    d_ff = wg_ref.shape[0]
    acc = None
    for c0 in range(0, d_ff, ff_chunk):
        c1 = min(c0 + ff_chunk, d_ff)
        g = _mm(x, wg_ref[c0:c1, :], precise, w_transposed=True)
        u = _mm(x, wu_ref[c0:c1, :], precise, w_transposed=True)
        part = _mm(_silu(g) * u, wd_ref[c0:c1, :], precise)
        acc = part if acc is None else acc + part
    return acc


def _ffn_kernel(x_ref, wg_ref, wu_ref, wd_ref, g_ref, b_ref, o_ref, *, alpha, ff_chunk, precise):
    x = x_ref[...]
    f = _swiglu_tile(x if precise else x.astype(BF16), wg_ref, wu_ref, wd_ref, ff_chunk, precise)
    o_ref[...] = _layer_norm(alpha * x + f, g_ref[...], b_ref[...])


def _ffn_dense(x, wg, wu, wd, ln_g, ln_b, *, alpha, tm, ff_chunk, precise):
    n, d = x.shape
    weights = [wg, wu, wd, ln_g, ln_b]
    return pl.pallas_call(
        functools.partial(_ffn_kernel, alpha=alpha, ff_chunk=ff_chunk, precise=precise),
        grid=(n // tm,),
        in_specs=[_row_spec(tm, d)] + [_const_spec(w.shape) for w in weights],
        out_specs=_row_spec(tm, d),
        out_shape=jax.ShapeDtypeStruct((n, d), F32),
        compiler_params=_params("parallel"),
        name="ffn_dense_ln2",
    )(x, *weights)


def _moe_kernel(te_ref, nt_ref, xs_ref, wg_ref, wu_ref, wd_ref, o_ref, *, ff_chunk):
    i = pl.program_id(0)

    @pl.when(i < nt_ref[0])
    def _():
        o_ref[...] = _swiglu_tile(xs_ref[...].astype(BF16), wg_ref, wu_ref, wd_ref, ff_chunk, False)

    @pl.when(i >= nt_ref[0])
    def _():
        o_ref[...] = jnp.zeros(o_ref.shape, o_ref.dtype)


def _moe_grouped(tile_expert, n_tiles_used, xs, wg, wu, wd, *, tm, ff_chunk):
    n_slots, d = xs.shape
    d_ff = wd.shape[1]

    def expert_spec(shape):
        return pl.BlockSpec((None,) + shape, lambda i, te, nt: (te[i], 0, 0), pipeline_mode=pl.Buffered(1))

    grid_spec = pltpu.PrefetchScalarGridSpec(
        num_scalar_prefetch=2,
        grid=(n_slots // tm,),
        in_specs=[pl.BlockSpec((tm, d), lambda i, te, nt: (i, 0)),
                  expert_spec((d_ff, d)), expert_spec((d_ff, d)), expert_spec((d_ff, d))],
        out_specs=pl.BlockSpec((tm, d), lambda i, te, nt: (i, 0)),
    )
    return pl.pallas_call(
        functools.partial(_moe_kernel, ff_chunk=ff_chunk),
        grid_spec=grid_spec,
        out_shape=jax.ShapeDtypeStruct((n_slots, d), F32),
        compiler_params=_params("arbitrary"),
        name="moe_grouped_swiglu",
    )(tile_expert, n_tiles_used, xs, wg, wu, wd)


def _combine_kernel(x_ref, ya_ref, yb_ref, rw_ref, g_ref, b_ref, o_ref, *, alpha):
    f = rw_ref[:, 0:1] * ya_ref[...] + rw_ref[:, 1:2] * yb_ref[...]
    o_ref[...] = _layer_norm(alpha * x_ref[...] + f, g_ref[...], b_ref[...])


def _moe_combine(x1, ya, yb, rw, ln_g, ln_b, *, alpha, tm, row0):
    n = rw.shape[0]
    d = x1.shape[1]
    shared = _row_spec(tm, d, row0 // tm)
    return pl.pallas_call(
        functools.partial(_combine_kernel, alpha=alpha),
        grid=(n // tm,),
        in_specs=[shared, shared, shared, _row_spec(tm, ROUTER_LANES),
                  _const_spec(ln_g.shape), _const_spec(ln_b.shape)],
        out_specs=_row_spec(tm, d),
        out_shape=jax.ShapeDtypeStruct((n, d), F32),
        compiler_params=_params("parallel"),
        name="moe_combine_ln2",
    )(x1, ya, yb, rw, ln_g, ln_b)


def _moe_route(top_i, n_experts, tm):
    n = top_i.shape[0]
    n_flat = TOP_K * n
    flat_e = top_i.reshape(-1)
    iota = jnp.arange(n_flat, dtype=jnp.int32)
    sorted_e, sorted_j = lax.sort((flat_e, iota), num_keys=1, is_stable=True)
    experts = jnp.arange(n_experts, dtype=jnp.int32)
    counts = jnp.sum((flat_e[:, None] == experts[None, :]).astype(jnp.int32), axis=0)
    start_end = jnp.cumsum(counts)
    padded = ((counts + tm - 1) // tm) * tm
    off_end = jnp.cumsum(padded)
    off = off_end - padded
    shift = off - (start_end - counts)
    n_slots = ((n_flat + n_experts * tm + tm - 1) // tm) * tm
    n_tiles = n_slots // tm
    tile_start = jnp.arange(n_tiles, dtype=jnp.int32) * tm
    tile_expert = jnp.sum((tile_start[:, None] >= off_end[None, :]).astype(jnp.int32), axis=1)
    tile_expert = jnp.minimum(tile_expert, n_experts - 1).astype(jnp.int32)
    n_used = (off_end[-1] // tm).astype(jnp.int32).reshape(1)
    slot = jnp.arange(n_slots, dtype=jnp.int32).reshape(n_tiles, tm)
    t_off, t_cnt, t_shift = off[tile_expert][:, None], counts[tile_expert][:, None], shift[tile_expert][:, None]
    valid = (slot - t_off) < t_cnt
    src = jnp.clip(slot - t_shift, 0, n_flat - 1)
    slot_tok = jnp.where(valid, sorted_j[src] // TOP_K, 0).reshape(n_slots)
    slot_of_sorted = iota + shift[sorted_e]
    _, slots = lax.sort((sorted_j, slot_of_sorted), num_keys=1)
    return slot_tok, tile_expert, n_used, slots.reshape(n, TOP_K)


def _pad_time_major(cache, steps):
    n_b, n_s, n_c = cache.shape
    tmaj = jnp.transpose(cache, (1, 0, 2))
    tmaj = jnp.pad(tmaj, ((steps - n_s, 0), (0, 0), (0, 0)))
    return tmaj.reshape(steps * n_b, n_c)


def _from_time_major(flat, n_b):
    n_c = flat.shape[1]
    return jnp.transpose(flat.reshape(-1, n_b, n_c), (1, 0, 2))


def kernel(x_prompt, x_sample, state_s5_re, state_s5_im, cache_pool, cache_conv, w_in, b_in, s5_a_re, s5_a_im, s5_log_dt, s5_b_re, s5_b_im, s5_c_re, s5_c_im, s5_d, s5_w_glu, s5_b_glu, pool_w, pool_scale, conv_w, conv_b, conv_ln_g, conv_ln_b, proj_s5, proj_pool, proj_conv, w_out, b_out, ln1_g, ln1_b, ln2_g, ln2_b, ffn_w_gate, ffn_w_up, ffn_w_down, moe_router, moe_w_gate, moe_w_up, moe_w_down):
    depth, d_model, _ = w_in.shape
    n_bp, n_tp, _ = x_prompt.shape
    n_bs, n_ts, _ = x_sample.shape
    assert n_ts == 1
    n_grp, n_p = s5_a_re.shape[1:]
    n_h = s5_b_re.shape[-1]
    n_state = n_grp * n_p
    d_s5 = n_grp * n_h
    d_pool = pool_scale.shape[1]
    d_conv = conv_w.shape[2]
    n_taps = conv_w.shape[1]
    d_lin = d_s5 + d_pool
    d_u = d_lin + 2 * d_conv
    n_exp = moe_router.shape[-1]
    alpha = (2.0 * depth) ** 0.25
    ff_chunk = MXU_DIM
    assert cache_pool.shape[2] == POOL_HIST - 1 and n_taps - 1 <= CONV_HIST
    n_p_tok = n_bp * n_tp
    n_tok = n_p_tok + n_bs
    tc_p = min(64, n_tp)
    tm_p = tc_p * n_bp

    def row(v):
        return v.reshape(1, -1).astype(F32)

    def both(w):
        w = w.astype(F32)
        return w.astype(BF16), w

    xp = jnp.transpose(x_prompt, (1, 0, 2)).reshape(n_p_tok, d_model)
    xs = x_sample.reshape(n_bs, d_model)
    zeros_p = dict(h0=jnp.zeros((n_bp, 2 * n_state), F32),
                   pool=jnp.zeros((POOL_HIST * n_bp, d_pool), F32),
                   conv=jnp.zeros((CONV_HIST * n_bp, d_conv), F32))
    outs = {k: [] for k in ("re_p", "im_p", "pool_p", "conv_p", "re_s", "im_s", "pool_s", "conv_s")}

    for l in range(depth):
        lam, bb_re, bb_im = _s5_discretise(s5_a_re[l], s5_a_im[l], s5_log_dt[l], s5_b_re[l], s5_b_im[l])
        conv_w_pad = jnp.pad(conv_w[l].astype(F32), ((0, CONV_HIST - n_taps), (0, 0)))
        shared_mw = dict(lam=lam, d=row(s5_d[l]), bglu=row(s5_b_glu[l]), pscale=row(pool_scale[l]),
                         conv_w=conv_w_pad, conv_b=row(conv_b[l]), conv_ln_g=row(conv_ln_g[l]),
                         conv_ln_b=row(conv_ln_b[l]), n_taps=n_taps)
        mats = dict(wb=both(_s5_input_blocks(bb_re, bb_im, n_grp, n_p)),
                    wc=both(_s5_output_blocks(s5_c_re[l].astype(F32), s5_c_im[l].astype(F32), n_p)),
                    wglu=both(s5_w_glu[l]), wpool=both(_block_diag(pool_w[l].astype(F32))))
        w1 = both(w_in[l, :, :d_u])
        b1 = row(b_in[l, :d_u])
        shared_lw = dict(b_gates=row(b_in[l, d_u:]), b_out=row(b_out[l]), ln1_g=row(ln1_g[l]), ln1_b=row(ln1_b[l]))
        lmats = dict(w_gates=both(w_in[l, :, d_u:]), proj_s5=both(proj_s5[l]), proj_pool=both(proj_pool[l]),
                     proj_conv=both(proj_conv[l]), w_out=both(w_out[l]))
        is_moe = l % 2 == 1
        router = None
        if is_moe:
            assert n_exp <= ROUTER_LANES // 2
            router = jnp.pad(moe_router[l // 2].astype(F32), ((0, 0), (0, ROUTER_LANES - n_exp)))

        mw = dict(shared_mw, **{k: v[0] for k, v in mats.items()})
        lw = dict(shared_lw, **{k: v[0] for k, v in lmats.items()})
        up = _inproj(xp, w1[0], b1, d_lin=d_lin, d_conv=d_conv, tm=tm_p, precise=False)
        yp, hT_p, pool_p, conv_p = _mixers(up, zeros_p["h0"], zeros_p["pool"], zeros_p["conv"], mw,
                                           nb=n_bp, tc=tc_p, start_pos=0, precise=False)
        res_p = _merge(xp, yp, lw, alpha=alpha, tm=tm_p, precise=False, router=router, n_exp=n_exp,
                       shared_rows=n_tok)
        mw = dict(shared_mw, **{k: v[1] for k, v in mats.items()})
        lw = dict(shared_lw, **{k: v[1] for k, v in lmats.items()})
        h0_s = jnp.concatenate([state_s5_re[l].reshape(n_bs, n_state), state_s5_im[l].reshape(n_bs, n_state)],
                               axis=1).astype(F32)
        us = _inproj(xs, w1[1], b1, d_lin=d_lin, d_conv=d_conv, tm=n_bs, precise=True)
        ys, hT_s, pool_s, conv_s = _mixers(us, h0_s, _pad_time_major(cache_pool[l].astype(F32), POOL_HIST),
                                           _pad_time_major(cache_conv[l].astype(F32), CONV_HIST), mw,
                                           nb=n_bs, tc=1, start_pos=PAST_LEN, precise=True)
        res_s = _merge(xs, ys, lw, alpha=alpha, tm=n_bs, precise=True, router=router, n_exp=n_exp,
                       shared=res_p[0] if is_moe else None, shared_rows=n_tok, row0=n_p_tok)

        outs["re_p"].append(hT_p[:, :n_state].reshape(n_bp, n_grp, n_p))
        outs["im_p"].append(hT_p[:, n_state:].reshape(n_bp, n_grp, n_p))
        outs["pool_p"].append(_from_time_major(pool_p, n_bp))
        outs["conv_p"].append(_from_time_major(conv_p, n_bp))
        outs["re_s"].append(hT_s[:, :n_state].reshape(n_bs, n_grp, n_p))
        outs["im_s"].append(hT_s[:, n_state:].reshape(n_bs, n_grp, n_p))
        outs["pool_s"].append(_from_time_major(pool_s, n_bs))
        outs["conv_s"].append(_from_time_major(conv_s, n_bs))

        g2, b2 = row(ln2_g[l]), row(ln2_b[l])
        i = l // 2
        if not is_moe:
            wg, wu = jnp.swapaxes(ffn_w_gate[i], -1, -2).astype(F32), jnp.swapaxes(ffn_w_up[i], -1, -2).astype(F32)
            wd = ffn_w_down[i].astype(F32)
            xp = _ffn_dense(res_p[0], wg, wu, wd, g2, b2, alpha=alpha, tm=tm_p, ff_chunk=ff_chunk,
                            precise=False)
            xs = _ffn_dense(res_s[0], wg, wu, wd, g2, b2, alpha=alpha, tm=n_bs, ff_chunk=ff_chunk,
                            precise=True)
        else:
            wg, wu = jnp.swapaxes(moe_w_gate[i], -1, -2).astype(F32), jnp.swapaxes(moe_w_up[i], -1, -2).astype(F32)
            wd = moe_w_down[i].astype(F32)
            _, ri_p, rw_p = res_p
            x1_all, ri_s, rw_s = res_s
            top_i = jnp.concatenate([ri_p[:, :TOP_K], ri_s[:, :TOP_K]], axis=0)
            tm_moe = 512
            slot_tok, tile_expert, n_used, slots = _moe_route(top_i, n_exp, tm_moe)
            x_sorted = x1_all.at[slot_tok].get(mode="promise_in_bounds")
            y_sorted = _moe_grouped(tile_expert, n_used, x_sorted, wg, wu, wd, tm=tm_moe, ff_chunk=ff_chunk)
            ya = y_sorted.at[slots[:, 0]].get(mode="promise_in_bounds")
            yb = y_sorted.at[slots[:, 1]].get(mode="promise_in_bounds")
            xp = _moe_combine(x1_all, ya, yb, rw_p, g2, b2, alpha=alpha, tm=tm_p, row0=0)
            xs = _moe_combine(x1_all, ya, yb, rw_s, g2, b2, alpha=alpha, tm=n_bs, row0=n_p_tok)

    y_prompt = _from_time_major(xp, n_bp)
    st = lambda k: jnp.stack(outs[k])
    return (y_prompt, xs.reshape(n_bs, 1, d_model), st("re_p"), st("im_p"), st("pool_p"), st("conv_p"),
            st("re_s"), st("im_s"), st("pool_s"), st("conv_s"))
```

```python
import functools

import jax
import jax.numpy as jnp
from jax import lax
from jax.experimental import pallas as pl
from jax.experimental.pallas import tpu as pltpu

F32 = jnp.float32
BF16 = jnp.bfloat16

LN_EPS = 1e-5
POOL_WINDOWS = (2, 4, 8, 16)
PAST_LEN = 16384
TOP_K = 2
POOL_HIST = 16
CONV_HIST = 32
ROUTER_LANES = 128
LANES = 128
MXU_DIM = 256
MOE_PARTS = 4
VMEM_LIMIT = 56 * 1024 * 1024


def _sigmoid(x):
    return 0.5 * jnp.tanh(0.5 * x) + 0.5


def _silu(x):
    return x * _sigmoid(x)


def _layer_norm(x, g, b):
    mu = jnp.mean(x, axis=-1, keepdims=True)
    xc = x - mu
    var = jnp.mean(xc * xc, axis=-1, keepdims=True)
    return xc * lax.rsqrt(var + LN_EPS) * g + b


def _dot(a, b, w_transposed=False):
    dims = (((1,), (1 if w_transposed else 0,)), ((), ()))
    return lax.dot_general(a, b, dims, preferred_element_type=F32)


def _split(v):
    hi = v.astype(BF16)
    return hi, (v - hi.astype(F32)).astype(BF16)


def _mm(a, w, precise, w_transposed=False):
    if not precise:
        return _dot(a.astype(BF16), w.astype(BF16), w_transposed)
    a_hi, a_lo = _split(a)
    w_hi, w_lo = _split(w)
    m = a.shape[0]
    r = _dot(jnp.concatenate([a_hi, a_lo], axis=0), w_hi, w_transposed)
    return r[:m] + r[m:] + _dot(a_hi, w_lo, w_transposed)


def _params(*sem):
    return pltpu.CompilerParams(dimension_semantics=sem, vmem_limit_bytes=VMEM_LIMIT)


def _const_spec(shape):
    return pl.BlockSpec(shape, lambda *_: (0,) * len(shape), pipeline_mode=pl.Buffered(1))


def _row_spec(tm, d, blk0=0):
    return pl.BlockSpec((tm, d), lambda i: (blk0 + i, 0))


def _s5_disc_kernel(are_ref, aim_ref, ldt_ref, bre_ref, bim_ref, lam_ref, bbre_ref, bbim_ref):
    a_re = are_ref[...]
    a_im = aim_ref[...]
    dt = jnp.exp(ldt_ref[...])
    mag = jnp.exp(a_re * dt)
    lr = mag * jnp.cos(a_im * dt)
    li = mag * jnp.sin(a_im * dt)
    den = a_re * a_re + a_im * a_im
    nr = lr - 1.0
    cr = (nr * a_re + li * a_im) / den
    ci = (li * a_re - nr * a_im) / den
    lam_ref[0:1, :] = lr
    lam_ref[1:2, :] = li
    b_re = bre_ref[...]
    b_im = bim_ref[...]
    bbre_ref[...] = cr * b_re - ci * b_im
    bbim_ref[...] = cr * b_im + ci * b_re


def _s5_discretise(a_re, a_im, log_dt, b_re, b_im):
    n_g, n_p = a_re.shape
    n_h = b_re.shape[-1]
    n_s = n_g * n_p
    ldt = jnp.broadcast_to(log_dt[:, None], (n_g, n_p)).reshape(1, n_s)
    b_re_t = b_re.reshape(n_s, n_h).T
    b_im_t = b_im.reshape(n_s, n_h).T
    return pl.pallas_call(
        _s5_disc_kernel,
        out_shape=(jax.ShapeDtypeStruct((2, n_s), F32),
                   jax.ShapeDtypeStruct((n_h, n_s), F32),
                   jax.ShapeDtypeStruct((n_h, n_s), F32)),
        name="s5_discretise",
    )(a_re.reshape(1, n_s), a_im.reshape(1, n_s), ldt, b_re_t, b_im_t)


def _block_diag(blocks):
    n_g, r, c = blocks.shape
    eye = jnp.eye(n_g, dtype=blocks.dtype)
    return (eye[:, None, :, None] * blocks[:, :, None, :]).reshape(n_g * r, n_g * c)


def _s5_input_blocks(bb_re, bb_im, n_grp, n_p):
    n_h = bb_re.shape[0]
    gpb = MXU_DIM // n_p
    ch = gpb * n_h
    assert MXU_DIM % n_p == 0 and LANES % ch == 0 and n_grp % gpb == 0
    n_blk = n_grp // gpb
    per_chunk = LANES // ch

    def blocks(bb):
        g = jnp.transpose(bb.reshape(n_h, n_grp, n_p), (1, 0, 2)).reshape(n_blk, gpb, n_h, n_p)
        dense = jax.vmap(_block_diag)(g)
        return jnp.stack([jnp.pad(dense[j], ((ch * (j % per_chunk), LANES - ch * (j % per_chunk + 1)), (0, 0)))
                          for j in range(n_blk)])

    return jnp.concatenate([blocks(bb_re), blocks(bb_im)], axis=2)


def _s5_output_blocks(c_re, c_im, n_p):
    n_grp, n_h, _ = c_re.shape
    gpb = MXU_DIM // n_h
    assert n_grp % gpb == 0
    n_blk = n_grp // gpb

    def blocks(cm):
        g = jnp.transpose(cm, (0, 2, 1)).reshape(n_blk, gpb, n_p, n_h)
        return jax.vmap(_block_diag)(g)

    return jnp.concatenate([blocks(c_re), blocks(-c_im)], axis=1)


def _inproj_kernel(x_ref, w_ref, b_ref, o_ref, *, d_lin, d_conv, precise):
    u = _mm(x_ref[...], w_ref[...], precise) + b_ref[...]
    o_ref[:, :d_lin] = u[:, :d_lin]
    o_ref[:, d_lin:] = u[:, d_lin:d_lin + d_conv] * _sigmoid(u[:, d_lin + d_conv:])


def _inproj(x, w, b, *, d_lin, d_conv, tm, precise):
    n, d = x.shape
    d_out = d_lin + d_conv
    return pl.pallas_call(
        functools.partial(_inproj_kernel, d_lin=d_lin, d_conv=d_conv, precise=precise),
        grid=(n // tm,),
        in_specs=[_row_spec(tm, d), _const_spec(w.shape), _const_spec(b.shape)],
        out_specs=_row_spec(tm, d_out),
        out_shape=jax.ShapeDtypeStruct((n, d_out), F32),
        compiler_params=_params("parallel"),
        name="in_projection",
    )(x, w, b)


def _mixer_kernel(u_ref, h0_ref, pc_ref, cc_ref, wb_ref, wc_ref, lam_ref, d_ref, wglu_ref, bglu_ref,
                  wpool_ref, pscale_ref, cw_ref, cb_ref, clg_ref, clb_ref,
                  y_ref, hT_ref, pnew_ref, cnew_ref,
                  s_scr, h_scr, ph_scr, ch_scr, ca_scr, *, nb, tc, start_pos, d_s5, d_pool, d_conv, n_taps,
                  precise):
    c = pl.program_id(0)
    last = pl.num_programs(0) - 1
    rows = tc * nb
    n_state = lam_ref.shape[1]
    o_pool = d_s5
    o_conv = d_s5 + d_pool

    @pl.when(c == 0)
    def _():
        h_scr[...] = h0_ref[...]
        ph_scr[0:POOL_HIST * nb, :] = pc_ref[...]
        ch_scr[0:CONV_HIST * nb, :] = cc_ref[...]

    u_s5 = u_ref[:, 0:d_s5]
    n_in_blk = wb_ref.shape[0]
    per_chunk = LANES // (d_s5 // n_in_blk)
    for j in range(n_in_blk):
        k0 = (j // per_chunk) * LANES
        bu = _mm(u_s5[:, k0:k0 + LANES], wb_ref[j], precise)
        s_scr[:, j * MXU_DIM:(j + 1) * MXU_DIM] = bu[:, :MXU_DIM]
        s_scr[:, n_state + j * MXU_DIM:n_state + (j + 1) * MXU_DIM] = bu[:, MXU_DIM:]
    if tc == 1:
        lr = lam_ref[0:1, :]
        li = lam_ref[1:2, :]
        h_re = h_scr[:, 0:n_state]
        h_im = h_scr[:, n_state:]
        n_re = lr * h_re - li * h_im + s_scr[:, 0:n_state]
        n_im = lr * h_im + li * h_re + s_scr[:, n_state:]
        s_scr[:, 0:n_state] = n_re
        s_scr[:, n_state:] = n_im
        h_scr[:, 0:n_state] = n_re
        h_scr[:, n_state:] = n_im
    else:
        assert nb == 8
        n_part = 2
        half = n_state // n_part
        for hf in range(n_part):
            re0 = hf * half
            im0 = n_state + hf * half
            lr = jnp.broadcast_to(lam_ref[0:1, re0:re0 + half], (nb, half))
            li = jnp.broadcast_to(lam_ref[1:2, re0:re0 + half], (nb, half))

            def step(t, carry, re0=re0, im0=im0, lr=lr, li=li):
                h_re, h_im = carry
                at_t = pl.ds(pl.multiple_of(t * nb, nb), nb)
                n_re = lr * h_re - li * h_im + s_scr[at_t, re0:re0 + half]
                n_im = lr * h_im + li * h_re + s_scr[at_t, im0:im0 + half]
                s_scr[at_t, re0:re0 + half] = n_re
                s_scr[at_t, im0:im0 + half] = n_im
                return n_re, n_im

            h_re, h_im = lax.fori_loop(0, tc, step, (h_scr[:, re0:re0 + half], h_scr[:, im0:im0 + half]),
                                       unroll=4)
            h_scr[:, re0:re0 + half] = h_re
            h_scr[:, im0:im0 + half] = h_im

    n_out_blk = wc_ref.shape[0]
    spb = n_state // n_out_blk
    ys = []
    for j in range(n_out_blk):
        h_re_b = s_scr[:, j * spb:(j + 1) * spb]
        h_im_b = s_scr[:, n_state + j * spb:n_state + (j + 1) * spb]
        ys.append(_mm(h_re_b, wc_ref[j, 0:spb, :], precise) + _mm(h_im_b, wc_ref[j, spb:, :], precise))
    y = jnp.concatenate(ys, axis=1) + d_ref[...] * u_s5
    y = jax.nn.gelu(y)
    y = y * _sigmoid(_mm(y, wglu_ref[...], precise) + bglu_ref[...])
    y_ref[:, 0:d_s5] = y.astype(y_ref.dtype)

    @pl.when(c == last)
    def _():
        hT_ref[...] = h_scr[...]

    u_pool = u_ref[:, o_pool:o_pool + d_pool]
    ph_scr[pl.ds(POOL_HIST * nb, rows), :] = u_pool
    a = ph_scr[...]
    row_t = lax.broadcasted_iota(jnp.int32, (rows, d_pool), 0) >> (nb.bit_length() - 1)
    pos1 = start_pos + c * tc + row_t + 1
    lane = lax.broadcasted_iota(jnp.int32, (rows, d_pool), 1)
    group = d_pool // len(POOL_WINDOWS)
    w = 1
    s_sel = None
    cnt_sel = None
    for gi, win in enumerate(POOL_WINDOWS):
        while w < win:
            a = a[w * nb:, :] + a[:a.shape[0] - w * nb, :]
            w *= 2
        assert w == win
        i0 = (POOL_HIST + 1 - win) * nb
        s_w = a[i0:i0 + rows, :]
        cnt_w = jnp.minimum(pos1, win).astype(F32)
        if s_sel is None:
            s_sel, cnt_sel = s_w, cnt_w
        else:
            in_later = lane >= gi * group
            s_sel = jnp.where(in_later, s_w, s_sel)
            cnt_sel = jnp.where(in_later, cnt_w, cnt_sel)
    pooled = s_sel / cnt_sel - u_pool
    y_pool = _mm(pooled, wpool_ref[...], precise) * pscale_ref[...]
    y_ref[:, o_pool:o_pool + d_pool] = y_pool.astype(y_ref.dtype)
    ph_scr[0:POOL_HIST * nb, :] = ph_scr[pl.ds(tc * nb, POOL_HIST * nb), :]

    @pl.when(c == last)
    def _():
        pnew_ref[...] = ph_scr[nb:POOL_HIST * nb, :]

    ch_scr[pl.ds(CONV_HIST * nb, rows), :] = u_ref[:, o_conv:o_conv + d_conv]
    rb = min(rows, 64)
    tap0 = CONV_HIST - (n_taps - 1)

    def conv_block(i, _):
        base = pl.multiple_of(i * rb, rb)
        acc = [None, None]
        for k in range(n_taps):
            term = cw_ref[k:k + 1, :] * ch_scr[pl.ds(base + (tap0 + k) * nb, rb), :]
            acc[k % 2] = term if acc[k % 2] is None else acc[k % 2] + term
        ca_scr[pl.ds(base, rb), :] = acc[0] + acc[1]
        return 0

    lax.fori_loop(0, rows // rb, conv_block, 0)
    yc = _silu(_layer_norm(ca_scr[...] + cb_ref[...], clg_ref[...], clb_ref[...]))
    y_ref[:, o_conv:o_conv + d_conv] = yc.astype(y_ref.dtype)
    ch_scr[0:CONV_HIST * nb, :] = ch_scr[pl.ds(tc * nb, CONV_HIST * nb), :]

    @pl.when(c == last)
    def _():
        cnew_ref[...] = ch_scr[tap0 * nb:CONV_HIST * nb, :]


def _mixers(u, h0, pool_cache, conv_cache, mw, *, nb, tc, start_pos, precise):
    n_rows, d_u = u.shape
    n_t = n_rows // nb
    rows = tc * nb
    d_s5 = mw["wglu"].shape[0]
    d_pool = mw["wpool"].shape[0]
    d_conv = mw["conv_w"].shape[1]
    n_taps = mw["n_taps"]
    d_y = d_s5 + d_pool + d_conv
    n_state2 = h0.shape[1]
    weights = [mw["wb"], mw["wc"], mw["lam"], mw["d"], mw["wglu"], mw["bglu"], mw["wpool"], mw["pscale"],
               mw["conv_w"], mw["conv_b"], mw["conv_ln_g"], mw["conv_ln_b"]]
    kern = functools.partial(_mixer_kernel, nb=nb, tc=tc, start_pos=start_pos, d_s5=d_s5, d_pool=d_pool,
                             d_conv=d_conv, n_taps=n_taps, precise=precise)
    return pl.pallas_call(
        kern,
        grid=(n_t // tc,),
        in_specs=[_row_spec(rows, d_u),
                  _const_spec(h0.shape), _const_spec(pool_cache.shape), _const_spec(conv_cache.shape)]
                 + [_const_spec(w.shape) for w in weights],
        out_specs=(_row_spec(rows, d_y),
                   pl.BlockSpec(h0.shape, lambda c: (0, 0)),
                   pl.BlockSpec(((POOL_HIST - 1) * nb, d_pool), lambda c: (0, 0)),
                   pl.BlockSpec(((n_taps - 1) * nb, d_conv), lambda c: (0, 0))),
        out_shape=(jax.ShapeDtypeStruct((n_rows, d_y), F32 if precise else BF16),
                   jax.ShapeDtypeStruct(h0.shape, F32),
                   jax.ShapeDtypeStruct(((POOL_HIST - 1) * nb, d_pool), F32),
                   jax.ShapeDtypeStruct(((n_taps - 1) * nb, d_conv), F32)),
        scratch_shapes=[pltpu.VMEM((rows, n_state2), F32),
                        pltpu.VMEM((nb, n_state2), F32),
                        pltpu.VMEM(((POOL_HIST + tc) * nb, d_pool), F32),
                        pltpu.VMEM(((CONV_HIST + tc) * nb, d_conv), F32),
                        pltpu.VMEM((rows, d_conv), F32)],
        compiler_params=_params("arbitrary"),
        name="token_mixers",
    )(u, h0, pool_cache, conv_cache, *weights)


def _merge_kernel(*refs, alpha, d_s5, d_pool, d_conv, n_exp, aliased, precise):
    with_router = n_exp > 0
    n_in = 11 + (1 if with_router else 0) + (1 if aliased else 0)
    x_ref, y_ref, wg_ref, bg_ref, ps5_ref, ppool_ref, pconv_ref, wo_ref, bo_ref, g_ref, b_ref = refs[:11]
    o_ref = refs[n_in]
    x = x_ref[...]
    xa = x if precise else x.astype(BF16)
    d = x.shape[-1]
    merged = None
    col = 0
    for j, (width, p_ref) in enumerate(((d_s5, ps5_ref), (d_pool, ppool_ref), (d_conv, pconv_ref))):
        gate = _sigmoid(_mm(xa, wg_ref[:, j * d:(j + 1) * d], precise) + bg_ref[:, j * d:(j + 1) * d])
        term = gate * _mm(y_ref[:, col:col + width], p_ref[...], precise)
        merged = term if merged is None else merged + term
        col += width
    out = _mm(merged, wo_ref[...], precise) + bo_ref[...]
    x1 = _layer_norm(alpha * x + out, g_ref[...], b_ref[...])
    o_ref[...] = x1
    if with_router:
        rt_ref = refs[11]
        ri_ref, rw_ref = refs[n_in + 1:n_in + 3]
        x_hi, x_lo = _split(x1)
        rt = rt_ref[...]
        rt_hi = rt.astype(BF16).astype(F32)
        rt2 = (rt_hi + pltpu.roll(rt - rt_hi, ROUTER_LANES // 2, axis=1)).astype(BF16)
        r = _dot(x_hi, rt2) + _dot(x_lo, rt2)
        logits = r + pltpu.roll(r, ROUTER_LANES // 2, axis=1)
        lane = lax.broadcasted_iota(jnp.int32, logits.shape, 1).astype(F32)
        neg = jnp.float32(-jnp.inf)
        l1 = jnp.where(lane < n_exp, logits, neg)
        m1 = jnp.max(l1, axis=-1, keepdims=True)
        i1 = jnp.min(jnp.where(l1 == m1, lane, float(ROUTER_LANES)), axis=-1, keepdims=True)
        l2 = jnp.where(lane == i1, neg, l1)
        m2 = jnp.max(l2, axis=-1, keepdims=True)
        i2 = jnp.min(jnp.where(l2 == m2, lane, float(ROUTER_LANES)), axis=-1, keepdims=True)
        e2 = jnp.exp(m2 - m1)
        w1 = 1.0 / (1.0 + e2)
        w2 = e2 * w1
        ri_ref[...] = jnp.where(lane == 0.0, i1, jnp.where(lane == 1.0, i2, 0.0)).astype(jnp.int32)
        rw_ref[...] = jnp.where(lane == 0.0, w1, jnp.where(lane == 1.0, w2, 0.0))


def _merge(x, y, lw, *, alpha, tm, precise, router=None, n_exp=0, shared=None, shared_rows=0, row0=0):
    n, d = x.shape
    d_y = y.shape[-1]
    d_s5, d_pool, d_conv = lw["proj_s5"].shape[0], lw["proj_pool"].shape[0], lw["proj_conv"].shape[0]
    inputs = [x, y, lw["w_gates"], lw["b_gates"], lw["proj_s5"], lw["proj_pool"], lw["proj_conv"], lw["w_out"],
              lw["b_out"], lw["ln1_g"], lw["ln1_b"]]
    in_specs = [_row_spec(tm, d), _row_spec(tm, d_y)] + [_const_spec(w.shape) for w in inputs[2:]]
    aliases = {}
    if router is None:
        out_specs = [_row_spec(tm, d)]
        out_shape = [jax.ShapeDtypeStruct((n, d), F32)]
    else:
        inputs.append(router)
        in_specs.append(_const_spec(router.shape))
        if shared is not None:
            aliases = {len(inputs): 0}
            inputs.append(shared)
            in_specs.append(pl.BlockSpec(memory_space=pl.ANY))
        out_specs = [_row_spec(tm, d, row0 // tm), _row_spec(tm, ROUTER_LANES), _row_spec(tm, ROUTER_LANES)]
        out_shape = [jax.ShapeDtypeStruct((shared_rows, d), F32),
                     jax.ShapeDtypeStruct((n, ROUTER_LANES), jnp.int32),
                     jax.ShapeDtypeStruct((n, ROUTER_LANES), F32)]
    kern = functools.partial(_merge_kernel, alpha=alpha, d_s5=d_s5, d_pool=d_pool, d_conv=d_conv,
                             n_exp=n_exp if router is not None else 0, aliased=shared is not None, precise=precise)
    return pl.pallas_call(
        kern,
        grid=(n // tm,),
        in_specs=in_specs,
        out_specs=tuple(out_specs),
        out_shape=tuple(out_shape),
        input_output_aliases=aliases,
        compiler_params=_params("parallel"),
        name="merge_outproj_ln1",
    )(*inputs)


def _swiglu_tile(x, wg_ref, wu_ref, wd_ref, ff_chunk, precise):
    d_ff = wg_ref.shape[0]
    acc = None
    for c0 in range(0, d_ff, ff_chunk):
        c1 = min(c0 + ff_chunk, d_ff)
        g = _mm(x, wg_ref[c0:c1, :], precise, w_transposed=True)
        u = _mm(x, wu_ref[c0:c1, :], precise, w_transposed=True)
        part = _mm(_silu(g) * u, wd_ref[c0:c1, :], precise)
        acc = part if acc is None else acc + part
    return acc


def _ffn_kernel(x_ref, wg_ref, wu_ref, wd_ref, g_ref, b_ref, o_ref, *, alpha, ff_chunk, precise):
    x = x_ref[...]
    f = _swiglu_tile(x if precise else x.astype(BF16), wg_ref, wu_ref, wd_ref, ff_chunk, precise)
    o_ref[...] = _layer_norm(alpha * x + f, g_ref[...], b_ref[...])


def _ffn_dense(x, wg, wu, wd, ln_g, ln_b, *, alpha, tm, ff_chunk, precise):
    n, d = x.shape
    weights = [wg, wu, wd, ln_g, ln_b]
    return pl.pallas_call(
        functools.partial(_ffn_kernel, alpha=alpha, ff_chunk=ff_chunk, precise=precise),
        grid=(n // tm,),
        in_specs=[_row_spec(tm, d)] + [_const_spec(w.shape) for w in weights],
        out_specs=_row_spec(tm, d),
        out_shape=jax.ShapeDtypeStruct((n, d), F32),
        compiler_params=_params("parallel"),
        name="ffn_dense_ln2",
    )(x, *weights)


def _moe_kernel(te_ref, nt_ref, xs_ref, wg_ref, wu_ref, wd_ref, *rest, ff_chunk, tile0):
    o_ref = rest[-1]
    i = tile0 + pl.program_id(0)

    @pl.when(i < nt_ref[0])
    def _():
        o_ref[...] = _swiglu_tile(xs_ref[...].astype(BF16), wg_ref, wu_ref, wd_ref, ff_chunk, False)

    @pl.when(i >= nt_ref[0])
    def _():
        o_ref[...] = jnp.zeros(o_ref.shape, o_ref.dtype)


def _moe_grouped(tile_expert, n_tiles_used, xs, wg, wu, wd, *, tm, ff_chunk, tile0, n_slots, y_prev=None):
    n_part, d = xs.shape
    d_ff = wd.shape[1]

    def expert_spec(shape):
        return pl.BlockSpec((None,) + shape, lambda i, te, nt: (te[tile0 + i], 0, 0),
                            pipeline_mode=pl.Buffered(1))

    inputs = [tile_expert, n_tiles_used, xs, wg, wu, wd]
    in_specs = [pl.BlockSpec((tm, d), lambda i, te, nt: (i, 0)),
                expert_spec((d_ff, d)), expert_spec((d_ff, d)), expert_spec((d_ff, d))]
    aliases = {}
    if y_prev is not None:
        aliases = {len(inputs): 0}
        inputs.append(y_prev)
        in_specs.append(pl.BlockSpec(memory_space=pl.ANY))
    grid_spec = pltpu.PrefetchScalarGridSpec(
        num_scalar_prefetch=2,
        grid=(n_part // tm,),
        in_specs=in_specs,
        out_specs=pl.BlockSpec((tm, d), lambda i, te, nt: (tile0 + i, 0)),
    )
    return pl.pallas_call(
        functools.partial(_moe_kernel, ff_chunk=ff_chunk, tile0=tile0),
        grid_spec=grid_spec,
        out_shape=jax.ShapeDtypeStruct((n_slots, d), F32),
        input_output_aliases=aliases,
        compiler_params=_params("arbitrary"),
        name="moe_grouped_swiglu",
    )(*inputs)


def _combine_kernel(x_ref, ya_ref, yb_ref, rw_ref, g_ref, b_ref, o_ref, *, alpha):
    f = rw_ref[:, 0:1] * ya_ref[...] + rw_ref[:, 1:2] * yb_ref[...]
    o_ref[...] = _layer_norm(alpha * x_ref[...] + f, g_ref[...], b_ref[...])


def _moe_combine(x1, ya, yb, rw, ln_g, ln_b, *, alpha, tm, row0):
    n = rw.shape[0]
    d = x1.shape[1]
    shared = _row_spec(tm, d, row0 // tm)
    return pl.pallas_call(
        functools.partial(_combine_kernel, alpha=alpha),
        grid=(n // tm,),
        in_specs=[shared, shared, shared, _row_spec(tm, ROUTER_LANES),
                  _const_spec(ln_g.shape), _const_spec(ln_b.shape)],
        out_specs=_row_spec(tm, d),
        out_shape=jax.ShapeDtypeStruct((n, d), F32),
        compiler_params=_params("parallel"),
        name="moe_combine_ln2",
    )(x1, ya, yb, rw, ln_g, ln_b)


def _moe_route(top_i, n_experts, tm):
    n = top_i.shape[0]
    n_flat = TOP_K * n
    flat_e = top_i.reshape(-1)
    iota = jnp.arange(n_flat, dtype=jnp.int32)
    sorted_e, sorted_j = lax.sort((flat_e, iota), num_keys=1, is_stable=True)
    experts = jnp.arange(n_experts, dtype=jnp.int32)
    counts = jnp.sum((flat_e[:, None] == experts[None, :]).astype(jnp.int32), axis=0)
    start_end = jnp.cumsum(counts)
    padded = ((counts + tm - 1) // tm) * tm
    off_end = jnp.cumsum(padded)
    off = off_end - padded
    shift = off - (start_end - counts)
    n_slots = ((n_flat + n_experts * tm + tm - 1) // tm) * tm
    n_tiles = n_slots // tm
    tile_start = jnp.arange(n_tiles, dtype=jnp.int32) * tm
    tile_expert = jnp.sum((tile_start[:, None] >= off_end[None, :]).astype(jnp.int32), axis=1)
    tile_expert = jnp.minimum(tile_expert, n_experts - 1).astype(jnp.int32)
    n_used = (off_end[-1] // tm).astype(jnp.int32).reshape(1)
    slot = jnp.arange(n_slots, dtype=jnp.int32).reshape(n_tiles, tm)
    t_off, t_cnt, t_shift = off[tile_expert][:, None], counts[tile_expert][:, None], shift[tile_expert][:, None]
    valid = (slot - t_off) < t_cnt
    src = jnp.clip(slot - t_shift, 0, n_flat - 1)
    slot_tok = jnp.where(valid, sorted_j[src] // TOP_K, 0).reshape(n_slots)
    slot_of_sorted = iota + shift[sorted_e]
    _, slots = lax.sort((sorted_j, slot_of_sorted), num_keys=1)
    return slot_tok, tile_expert, n_used, slots.reshape(n, TOP_K)


def _pad_time_major(cache, steps):
    n_b, n_s, n_c = cache.shape
    tmaj = jnp.transpose(cache, (1, 0, 2))
    tmaj = jnp.pad(tmaj, ((steps - n_s, 0), (0, 0), (0, 0)))
    return tmaj.reshape(steps * n_b, n_c)


def _from_time_major(flat, n_b):
    n_c = flat.shape[1]
    return jnp.transpose(flat.reshape(-1, n_b, n_c), (1, 0, 2))


def kernel(x_prompt, x_sample, state_s5_re, state_s5_im, cache_pool, cache_conv, w_in, b_in, s5_a_re, s5_a_im, s5_log_dt, s5_b_re, s5_b_im, s5_c_re, s5_c_im, s5_d, s5_w_glu, s5_b_glu, pool_w, pool_scale, conv_w, conv_b, conv_ln_g, conv_ln_b, proj_s5, proj_pool, proj_conv, w_out, b_out, ln1_g, ln1_b, ln2_g, ln2_b, ffn_w_gate, ffn_w_up, ffn_w_down, moe_router, moe_w_gate, moe_w_up, moe_w_down):
    depth, d_model, _ = w_in.shape
    n_bp, n_tp, _ = x_prompt.shape
    n_bs, n_ts, _ = x_sample.shape
    assert n_ts == 1
    n_grp, n_p = s5_a_re.shape[1:]
    n_h = s5_b_re.shape[-1]
    n_state = n_grp * n_p
    d_s5 = n_grp * n_h
    d_pool = pool_scale.shape[1]
    d_conv = conv_w.shape[2]
    n_taps = conv_w.shape[1]
    d_lin = d_s5 + d_pool
    d_u = d_lin + 2 * d_conv
    n_exp = moe_router.shape[-1]
    alpha = (2.0 * depth) ** 0.25
    ff_chunk = MXU_DIM
    assert cache_pool.shape[2] == POOL_HIST - 1 and n_taps - 1 <= CONV_HIST
    n_p_tok = n_bp * n_tp
    n_tok = n_p_tok + n_bs
    tc_p = min(64, n_tp)
    tm_p = tc_p * n_bp

    def row(v):
        return v.reshape(1, -1).astype(F32)

    def both(w):
        w = w.astype(F32)
        return w.astype(BF16), w

    xp = jnp.transpose(x_prompt, (1, 0, 2)).reshape(n_p_tok, d_model)
    xs = x_sample.reshape(n_bs, d_model)
    zeros_p = dict(h0=jnp.zeros((n_bp, 2 * n_state), F32),
                   pool=jnp.zeros((POOL_HIST * n_bp, d_pool), F32),
                   conv=jnp.zeros((CONV_HIST * n_bp, d_conv), F32))
    outs = {k: [] for k in ("re_p", "im_p", "pool_p", "conv_p", "re_s", "im_s", "pool_s", "conv_s")}

    for l in range(depth):
        lam, bb_re, bb_im = _s5_discretise(s5_a_re[l], s5_a_im[l], s5_log_dt[l], s5_b_re[l], s5_b_im[l])
        conv_w_pad = jnp.pad(conv_w[l].astype(F32), ((0, CONV_HIST - n_taps), (0, 0)))
        shared_mw = dict(lam=lam, d=row(s5_d[l]), bglu=row(s5_b_glu[l]), pscale=row(pool_scale[l]),
                         conv_w=conv_w_pad, conv_b=row(conv_b[l]), conv_ln_g=row(conv_ln_g[l]),
                         conv_ln_b=row(conv_ln_b[l]), n_taps=n_taps)
        mats = dict(wb=both(_s5_input_blocks(bb_re, bb_im, n_grp, n_p)),
                    wc=both(_s5_output_blocks(s5_c_re[l].astype(F32), s5_c_im[l].astype(F32), n_p)),
                    wglu=both(s5_w_glu[l]), wpool=both(_block_diag(pool_w[l].astype(F32))))
        w1 = both(w_in[l, :, :d_u])
        b1 = row(b_in[l, :d_u])
        shared_lw = dict(b_gates=row(b_in[l, d_u:]), b_out=row(b_out[l]), ln1_g=row(ln1_g[l]), ln1_b=row(ln1_b[l]))
        lmats = dict(w_gates=both(w_in[l, :, d_u:]), proj_s5=both(proj_s5[l]), proj_pool=both(proj_pool[l]),
                     proj_conv=both(proj_conv[l]), w_out=both(w_out[l]))
        is_moe = l % 2 == 1
        router = None
        if is_moe:
            assert n_exp <= ROUTER_LANES // 2
            router = jnp.pad(moe_router[l // 2].astype(F32), ((0, 0), (0, ROUTER_LANES - n_exp)))

        mw = dict(shared_mw, **{k: v[0] for k, v in mats.items()})
        lw = dict(shared_lw, **{k: v[0] for k, v in lmats.items()})
        up = _inproj(xp, w1[0], b1, d_lin=d_lin, d_conv=d_conv, tm=tm_p, precise=False)
        yp, hT_p, pool_p, conv_p = _mixers(up, zeros_p["h0"], zeros_p["pool"], zeros_p["conv"], mw,
                                           nb=n_bp, tc=tc_p, start_pos=0, precise=False)
        res_p = _merge(xp, yp, lw, alpha=alpha, tm=tm_p, precise=False, router=router, n_exp=n_exp,
                       shared_rows=n_tok)
        mw = dict(shared_mw, **{k: v[1] for k, v in mats.items()})
        lw = dict(shared_lw, **{k: v[1] for k, v in lmats.items()})
        h0_s = jnp.concatenate([state_s5_re[l].reshape(n_bs, n_state), state_s5_im[l].reshape(n_bs, n_state)],
                               axis=1).astype(F32)
        us = _inproj(xs, w1[1], b1, d_lin=d_lin, d_conv=d_conv, tm=n_bs, precise=True)
        ys, hT_s, pool_s, conv_s = _mixers(us, h0_s, _pad_time_major(cache_pool[l].astype(F32), POOL_HIST),
                                           _pad_time_major(cache_conv[l].astype(F32), CONV_HIST), mw,
                                           nb=n_bs, tc=1, start_pos=PAST_LEN, precise=True)
        res_s = _merge(xs, ys, lw, alpha=alpha, tm=n_bs, precise=True, router=router, n_exp=n_exp,
                       shared=res_p[0] if is_moe else None, shared_rows=n_tok, row0=n_p_tok)

        outs["re_p"].append(hT_p[:, :n_state].reshape(n_bp, n_grp, n_p))
        outs["im_p"].append(hT_p[:, n_state:].reshape(n_bp, n_grp, n_p))
        outs["pool_p"].append(_from_time_major(pool_p, n_bp))
        outs["conv_p"].append(_from_time_major(conv_p, n_bp))
        outs["re_s"].append(hT_s[:, :n_state].reshape(n_bs, n_grp, n_p))
        outs["im_s"].append(hT_s[:, n_state:].reshape(n_bs, n_grp, n_p))
        outs["pool_s"].append(_from_time_major(pool_s, n_bs))
        outs["conv_s"].append(_from_time_major(conv_s, n_bs))

        g2, b2 = row(ln2_g[l]), row(ln2_b[l])
        i = l // 2
        if not is_moe:
            wg, wu = jnp.swapaxes(ffn_w_gate[i], -1, -2).astype(F32), jnp.swapaxes(ffn_w_up[i], -1, -2).astype(F32)
            wd = ffn_w_down[i].astype(F32)
            xp = _ffn_dense(res_p[0], wg, wu, wd, g2, b2, alpha=alpha, tm=tm_p, ff_chunk=ff_chunk,
                            precise=False)
            xs = _ffn_dense(res_s[0], wg, wu, wd, g2, b2, alpha=alpha, tm=n_bs, ff_chunk=ff_chunk,
                            precise=True)
        else:
            wg, wu = jnp.swapaxes(moe_w_gate[i], -1, -2).astype(F32), jnp.swapaxes(moe_w_up[i], -1, -2).astype(F32)
            wd = moe_w_down[i].astype(F32)
            _, ri_p, rw_p = res_p
            x1_all, ri_s, rw_s = res_s
            top_i = jnp.concatenate([ri_p[:, :TOP_K], ri_s[:, :TOP_K]], axis=0)
            tm_moe = 512
            slot_tok, tile_expert, n_used, slots = _moe_route(top_i, n_exp, tm_moe)
            n_tiles = slot_tok.shape[0] // tm_moe
            bounds = [(n_tiles * k) // MOE_PARTS for k in range(MOE_PARTS + 1)]
            y_sorted = None
            for t0, t1 in zip(bounds[:-1], bounds[1:]):
                x_part = x1_all.at[slot_tok[t0 * tm_moe:t1 * tm_moe]].get(mode="promise_in_bounds")
                y_sorted = _moe_grouped(tile_expert, n_used, x_part, wg, wu, wd, tm=tm_moe, ff_chunk=ff_chunk,
                                        tile0=t0, n_slots=slot_tok.shape[0], y_prev=y_sorted)
            ya = y_sorted.at[slots[:, 0]].get(mode="promise_in_bounds")
            yb = y_sorted.at[slots[:, 1]].get(mode="promise_in_bounds")
            xp = _moe_combine(x1_all, ya, yb, rw_p, g2, b2, alpha=alpha, tm=tm_p, row0=0)
            xs = _moe_combine(x1_all, ya, yb, rw_s, g2, b2, alpha=alpha, tm=n_bs, row0=n_p_tok)

    y_prompt = _from_time_major(xp, n_bp)
    st = lambda k: jnp.stack(outs[k])
    return (y_prompt, xs.reshape(n_bs, 1, d_model), st("re_p"), st("im_p"), st("pool_p"), st("conv_p"),
            st("re_s"), st("im_s"), st("pool_s"), st("conv_s"))
```

```python
import functools

import jax
import jax.numpy as jnp
from jax import lax
from jax.experimental import pallas as pl
from jax.experimental.pallas import tpu as pltpu

F32 = jnp.float32
BF16 = jnp.bfloat16

LN_EPS = 1e-5
POOL_WINDOWS = (2, 4, 8, 16)
PAST_LEN = 16384
TOP_K = 2
POOL_HIST = 16
CONV_HIST = 32
ROUTER_LANES = 128
LANES = 128
MXU_DIM = 256
MOE_PARTS = 4
VMEM_LIMIT = 56 * 1024 * 1024


def _sigmoid(x):
    return 0.5 * jnp.tanh(0.5 * x) + 0.5


def _silu(x):
    return x * _sigmoid(x)


def _layer_norm(x, g, b):
    mu = jnp.mean(x, axis=-1, keepdims=True)
    xc = x - mu
    var = jnp.mean(xc * xc, axis=-1, keepdims=True)
    return xc * lax.rsqrt(var + LN_EPS) * g + b


def _dot(a, b, w_transposed=False):
    dims = (((1,), (1 if w_transposed else 0,)), ((), ()))
    return lax.dot_general(a, b, dims, preferred_element_type=F32)


def _split(v):
    hi = v.astype(BF16)
    return hi, (v - hi.astype(F32)).astype(BF16)


def _mm(a, w, precise, w_transposed=False):
    if not precise:
        return _dot(a.astype(BF16), w.astype(BF16), w_transposed)
    a_hi, a_lo = _split(a)
    w_hi, w_lo = _split(w)
    m = a.shape[0]
    r = _dot(jnp.concatenate([a_hi, a_lo], axis=0), w_hi, w_transposed)
    return r[:m] + r[m:] + _dot(a_hi, w_lo, w_transposed)


def _params(*sem):
    return pltpu.CompilerParams(dimension_semantics=sem, vmem_limit_bytes=VMEM_LIMIT)


def _const_spec(shape):
    return pl.BlockSpec(shape, lambda *_: (0,) * len(shape), pipeline_mode=pl.Buffered(1))


def _row_spec(tm, d, blk0=0):
    return pl.BlockSpec((tm, d), lambda i: (blk0 + i, 0))


def _s5_disc_kernel(are_ref, aim_ref, ldt_ref, bre_ref, bim_ref, lam_ref, bbre_ref, bbim_ref):
    a_re = are_ref[...]
    a_im = aim_ref[...]
    dt = jnp.exp(ldt_ref[...])
    mag = jnp.exp(a_re * dt)
    lr = mag * jnp.cos(a_im * dt)
    li = mag * jnp.sin(a_im * dt)
    den = a_re * a_re + a_im * a_im
    nr = lr - 1.0
    cr = (nr * a_re + li * a_im) / den
    ci = (li * a_re - nr * a_im) / den
    lam_ref[0:1, :] = lr
    lam_ref[1:2, :] = li
    b_re = bre_ref[...]
    b_im = bim_ref[...]
    bbre_ref[...] = cr * b_re - ci * b_im
    bbim_ref[...] = cr * b_im + ci * b_re


def _s5_discretise(a_re, a_im, log_dt, b_re, b_im):
    n_g, n_p = a_re.shape
    n_h = b_re.shape[-1]
    n_s = n_g * n_p
    ldt = jnp.broadcast_to(log_dt[:, None], (n_g, n_p)).reshape(1, n_s)
    b_re_t = b_re.reshape(n_s, n_h).T
    b_im_t = b_im.reshape(n_s, n_h).T
    return pl.pallas_call(
        _s5_disc_kernel,
        out_shape=(jax.ShapeDtypeStruct((2, n_s), F32),
                   jax.ShapeDtypeStruct((n_h, n_s), F32),
                   jax.ShapeDtypeStruct((n_h, n_s), F32)),
        name="s5_discretise",
    )(a_re.reshape(1, n_s), a_im.reshape(1, n_s), ldt, b_re_t, b_im_t)


def _block_diag(blocks):
    n_g, r, c = blocks.shape
    eye = jnp.eye(n_g, dtype=blocks.dtype)
    return (eye[:, None, :, None] * blocks[:, :, None, :]).reshape(n_g * r, n_g * c)


def _s5_input_blocks(bb_re, bb_im, n_grp, n_p):
    n_h = bb_re.shape[0]
    gpb = MXU_DIM // n_p
    ch = gpb * n_h
    assert MXU_DIM % n_p == 0 and LANES % ch == 0 and n_grp % gpb == 0
    n_blk = n_grp // gpb
    per_chunk = LANES // ch

    def blocks(bb):
        g = jnp.transpose(bb.reshape(n_h, n_grp, n_p), (1, 0, 2)).reshape(n_blk, gpb, n_h, n_p)
        dense = jax.vmap(_block_diag)(g)
        return jnp.stack([jnp.pad(dense[j], ((ch * (j % per_chunk), LANES - ch * (j % per_chunk + 1)), (0, 0)))
                          for j in range(n_blk)])

    return jnp.concatenate([blocks(bb_re), blocks(bb_im)], axis=2)


def _s5_output_blocks(c_re, c_im, n_p):
    n_grp, n_h, _ = c_re.shape
    gpb = MXU_DIM // n_h
    assert n_grp % gpb == 0
    n_blk = n_grp // gpb

    def blocks(cm):
        g = jnp.transpose(cm, (0, 2, 1)).reshape(n_blk, gpb, n_p, n_h)
        return jax.vmap(_block_diag)(g)

    return jnp.concatenate([blocks(c_re), blocks(-c_im)], axis=1)


def _inproj_kernel(x_ref, w_ref, b_ref, o_ref, *, d_lin, d_conv, precise):
    u = _mm(x_ref[...], w_ref[...], precise) + b_ref[...]
    o_ref[:, :d_lin] = u[:, :d_lin]
    o_ref[:, d_lin:] = u[:, d_lin:d_lin + d_conv] * _sigmoid(u[:, d_lin + d_conv:])


def _inproj(x, w, b, *, d_lin, d_conv, tm, precise):
    n, d = x.shape
    d_out = d_lin + d_conv
    return pl.pallas_call(
        functools.partial(_inproj_kernel, d_lin=d_lin, d_conv=d_conv, precise=precise),
        grid=(n // tm,),
        in_specs=[_row_spec(tm, d), _const_spec(w.shape), _const_spec(b.shape)],
        out_specs=_row_spec(tm, d_out),
        out_shape=jax.ShapeDtypeStruct((n, d_out), F32),
        compiler_params=_params("parallel"),
        name="in_projection",
    )(x, w, b)


def _mixer_kernel(u_ref, h0_ref, pc_ref, cc_ref, wb_ref, wc_ref, lam_ref, d_ref, wglu_ref, bglu_ref,
                  wpool_ref, pscale_ref, cw_ref, cb_ref, clg_ref, clb_ref,
                  y_ref, hT_ref, pnew_ref, cnew_ref,
                  s_scr, h_scr, ph_scr, ch_scr, ca_scr, *, nb, tc, start_pos, d_s5, d_pool, d_conv, n_taps,
                  precise):
    c = pl.program_id(0)
    last = pl.num_programs(0) - 1
    rows = tc * nb
    n_state = lam_ref.shape[1]
    o_pool = d_s5
    o_conv = d_s5 + d_pool

    @pl.when(c == 0)
    def _():
        h_scr[...] = h0_ref[...]
        ph_scr[0:POOL_HIST * nb, :] = pc_ref[...]
        ch_scr[0:CONV_HIST * nb, :] = cc_ref[...]

    u_s5 = u_ref[:, 0:d_s5]
    n_in_blk = wb_ref.shape[0]
    per_chunk = LANES // (d_s5 // n_in_blk)
    for j in range(n_in_blk):
        k0 = (j // per_chunk) * LANES
        bu = _mm(u_s5[:, k0:k0 + LANES], wb_ref[j], precise)
        s_scr[:, j * MXU_DIM:(j + 1) * MXU_DIM] = bu[:, :MXU_DIM]
        s_scr[:, n_state + j * MXU_DIM:n_state + (j + 1) * MXU_DIM] = bu[:, MXU_DIM:]
    if tc == 1:
        lr = lam_ref[0:1, :]
        li = lam_ref[1:2, :]
        h_re = h_scr[:, 0:n_state]
        h_im = h_scr[:, n_state:]
        n_re = lr * h_re - li * h_im + s_scr[:, 0:n_state]
        n_im = lr * h_im + li * h_re + s_scr[:, n_state:]
        s_scr[:, 0:n_state] = n_re
        s_scr[:, n_state:] = n_im
        h_scr[:, 0:n_state] = n_re
        h_scr[:, n_state:] = n_im
    else:
        assert nb == 8
        n_part = 2
        half = n_state // n_part
        for hf in range(n_part):
            re0 = hf * half
            im0 = n_state + hf * half
            lr = jnp.broadcast_to(lam_ref[0:1, re0:re0 + half], (nb, half))
            li = jnp.broadcast_to(lam_ref[1:2, re0:re0 + half], (nb, half))

            def step(t, carry, re0=re0, im0=im0, lr=lr, li=li):
                h_re, h_im = carry
                at_t = pl.ds(pl.multiple_of(t * nb, nb), nb)
                n_re = lr * h_re - li * h_im + s_scr[at_t, re0:re0 + half]
                n_im = lr * h_im + li * h_re + s_scr[at_t, im0:im0 + half]
                s_scr[at_t, re0:re0 + half] = n_re
                s_scr[at_t, im0:im0 + half] = n_im
                return n_re, n_im

            h_re, h_im = lax.fori_loop(0, tc, step, (h_scr[:, re0:re0 + half], h_scr[:, im0:im0 + half]),
                                       unroll=4)
            h_scr[:, re0:re0 + half] = h_re
            h_scr[:, im0:im0 + half] = h_im

    n_out_blk = wc_ref.shape[0]
    spb = n_state // n_out_blk
    ys = []
    for j in range(n_out_blk):
        h_re_b = s_scr[:, j * spb:(j + 1) * spb]
        h_im_b = s_scr[:, n_state + j * spb:n_state + (j + 1) * spb]
        ys.append(_mm(h_re_b, wc_ref[j, 0:spb, :], precise) + _mm(h_im_b, wc_ref[j, spb:, :], precise))
    y = jnp.concatenate(ys, axis=1) + d_ref[...] * u_s5
    y = jax.nn.gelu(y)
    y = y * _sigmoid(_mm(y, wglu_ref[...], precise) + bglu_ref[...])
    y_ref[:, 0:d_s5] = y.astype(y_ref.dtype)

    @pl.when(c == last)
    def _():
        hT_ref[...] = h_scr[...]

    u_pool = u_ref[:, o_pool:o_pool + d_pool]
    ph_scr[pl.ds(POOL_HIST * nb, rows), :] = u_pool
    a = ph_scr[...]
    row_t = lax.broadcasted_iota(jnp.int32, (rows, d_pool), 0) >> (nb.bit_length() - 1)
    pos1 = start_pos + c * tc + row_t + 1
    lane = lax.broadcasted_iota(jnp.int32, (rows, d_pool), 1)
    group = d_pool // len(POOL_WINDOWS)
    w = 1
    s_sel = None
    cnt_sel = None
    for gi, win in enumerate(POOL_WINDOWS):
        while w < win:
            a = a[w * nb:, :] + a[:a.shape[0] - w * nb, :]
            w *= 2
        assert w == win
        i0 = (POOL_HIST + 1 - win) * nb
        s_w = a[i0:i0 + rows, :]
        cnt_w = jnp.minimum(pos1, win).astype(F32)
        if s_sel is None:
            s_sel, cnt_sel = s_w, cnt_w
        else:
            in_later = lane >= gi * group
            s_sel = jnp.where(in_later, s_w, s_sel)
            cnt_sel = jnp.where(in_later, cnt_w, cnt_sel)
    pooled = s_sel / cnt_sel - u_pool
    y_pool = _mm(pooled, wpool_ref[...], precise) * pscale_ref[...]
    y_ref[:, o_pool:o_pool + d_pool] = y_pool.astype(y_ref.dtype)
    ph_scr[0:POOL_HIST * nb, :] = ph_scr[pl.ds(tc * nb, POOL_HIST * nb), :]

    @pl.when(c == last)
    def _():
        pnew_ref[...] = ph_scr[nb:POOL_HIST * nb, :]

    ch_scr[pl.ds(CONV_HIST * nb, rows), :] = u_ref[:, o_conv:o_conv + d_conv]
    rb = min(rows, 64)
    tap0 = CONV_HIST - (n_taps - 1)

    def conv_block(i, _):
        base = pl.multiple_of(i * rb, rb)
        acc = [None, None]
        for k in range(n_taps):
            term = cw_ref[k:k + 1, :] * ch_scr[pl.ds(base + (tap0 + k) * nb, rb), :]
            acc[k % 2] = term if acc[k % 2] is None else acc[k % 2] + term
        ca_scr[pl.ds(base, rb), :] = acc[0] + acc[1]
        return 0

    lax.fori_loop(0, rows // rb, conv_block, 0)
    yc = _silu(_layer_norm(ca_scr[...] + cb_ref[...], clg_ref[...], clb_ref[...]))
    y_ref[:, o_conv:o_conv + d_conv] = yc.astype(y_ref.dtype)
    ch_scr[0:CONV_HIST * nb, :] = ch_scr[pl.ds(tc * nb, CONV_HIST * nb), :]

    @pl.when(c == last)
    def _():
        cnew_ref[...] = ch_scr[tap0 * nb:CONV_HIST * nb, :]


def _mixers(u, h0, pool_cache, conv_cache, mw, *, nb, tc, start_pos, precise):
    n_rows, d_u = u.shape
    n_t = n_rows // nb
    rows = tc * nb
    d_s5 = mw["wglu"].shape[0]
    d_pool = mw["wpool"].shape[0]
    d_conv = mw["conv_w"].shape[1]
    n_taps = mw["n_taps"]
    d_y = d_s5 + d_pool + d_conv
    n_state2 = h0.shape[1]
    weights = [mw["wb"], mw["wc"], mw["lam"], mw["d"], mw["wglu"], mw["bglu"], mw["wpool"], mw["pscale"],
               mw["conv_w"], mw["conv_b"], mw["conv_ln_g"], mw["conv_ln_b"]]
    kern = functools.partial(_mixer_kernel, nb=nb, tc=tc, start_pos=start_pos, d_s5=d_s5, d_pool=d_pool,
                             d_conv=d_conv, n_taps=n_taps, precise=precise)
    return pl.pallas_call(
        kern,
        grid=(n_t // tc,),
        in_specs=[_row_spec(rows, d_u),
                  _const_spec(h0.shape), _const_spec(pool_cache.shape), _const_spec(conv_cache.shape)]
                 + [_const_spec(w.shape) for w in weights],
        out_specs=(_row_spec(rows, d_y),
                   pl.BlockSpec(h0.shape, lambda c: (0, 0)),
                   pl.BlockSpec(((POOL_HIST - 1) * nb, d_pool), lambda c: (0, 0)),
                   pl.BlockSpec(((n_taps - 1) * nb, d_conv), lambda c: (0, 0))),
        out_shape=(jax.ShapeDtypeStruct((n_rows, d_y), F32 if precise else BF16),
                   jax.ShapeDtypeStruct(h0.shape, F32),
                   jax.ShapeDtypeStruct(((POOL_HIST - 1) * nb, d_pool), F32),
                   jax.ShapeDtypeStruct(((n_taps - 1) * nb, d_conv), F32)),
        scratch_shapes=[pltpu.VMEM((rows, n_state2), F32),
                        pltpu.VMEM((nb, n_state2), F32),
                        pltpu.VMEM(((POOL_HIST + tc) * nb, d_pool), F32),
                        pltpu.VMEM(((CONV_HIST + tc) * nb, d_conv), F32),
                        pltpu.VMEM((rows, d_conv), F32)],
        compiler_params=_params("arbitrary"),
        name="token_mixers",
    )(u, h0, pool_cache, conv_cache, *weights)


def _merge_kernel(*refs, alpha, d_s5, d_pool, d_conv, n_exp, aliased, precise):
    with_router = n_exp > 0
    n_in = 11 + (1 if with_router else 0) + (1 if aliased else 0)
    x_ref, y_ref, wg_ref, bg_ref, ps5_ref, ppool_ref, pconv_ref, wo_ref, bo_ref, g_ref, b_ref = refs[:11]
    o_ref = refs[n_in]
    x = x_ref[...]
    xa = x if precise else x.astype(BF16)
    d = x.shape[-1]
    merged = None
    col = 0
    for j, (width, p_ref) in enumerate(((d_s5, ps5_ref), (d_pool, ppool_ref), (d_conv, pconv_ref))):
        gate = _sigmoid(_mm(xa, wg_ref[:, j * d:(j + 1) * d], precise) + bg_ref[:, j * d:(j + 1) * d])
        term = gate * _mm(y_ref[:, col:col + width], p_ref[...], precise)
        merged = term if merged is None else merged + term
        col += width
    out = _mm(merged, wo_ref[...], precise) + bo_ref[...]
    x1 = _layer_norm(alpha * x + out, g_ref[...], b_ref[...])
    o_ref[...] = x1
    if with_router:
        rt_ref = refs[11]
        ri_ref, rw_ref = refs[n_in + 1:n_in + 3]
        x_hi, x_lo = _split(x1)
        rt = rt_ref[...]
        rt_hi = rt.astype(BF16).astype(F32)
        rt2 = (rt_hi + pltpu.roll(rt - rt_hi, ROUTER_LANES // 2, axis=1)).astype(BF16)
        r = _dot(x_hi, rt2) + _dot(x_lo, rt2)
        logits = r + pltpu.roll(r, ROUTER_LANES // 2, axis=1)
        lane = lax.broadcasted_iota(jnp.int32, logits.shape, 1).astype(F32)
        neg = jnp.float32(-jnp.inf)
        l1 = jnp.where(lane < n_exp, logits, neg)
        m1 = jnp.max(l1, axis=-1, keepdims=True)
        i1 = jnp.min(jnp.where(l1 == m1, lane, float(ROUTER_LANES)), axis=-1, keepdims=True)
        l2 = jnp.where(lane == i1, neg, l1)
        m2 = jnp.max(l2, axis=-1, keepdims=True)
        i2 = jnp.min(jnp.where(l2 == m2, lane, float(ROUTER_LANES)), axis=-1, keepdims=True)
        e2 = jnp.exp(m2 - m1)
        w1 = 1.0 / (1.0 + e2)
        w2 = e2 * w1
        ri_ref[...] = jnp.where(lane == 0.0, i1, jnp.where(lane == 1.0, i2, 0.0)).astype(jnp.int32)
        rw_ref[...] = jnp.where(lane == 0.0, w1, jnp.where(lane == 1.0, w2, 0.0))


def _merge(x, y, lw, *, alpha, tm, precise, router=None, n_exp=0, shared=None, shared_rows=0, row0=0):
    n, d = x.shape
    d_y = y.shape[-1]
    d_s5, d_pool, d_conv = lw["proj_s5"].shape[0], lw["proj_pool"].shape[0], lw["proj_conv"].shape[0]
    inputs = [x, y, lw["w_gates"], lw["b_gates"], lw["proj_s5"], lw["proj_pool"], lw["proj_conv"], lw["w_out"],
              lw["b_out"], lw["ln1_g"], lw["ln1_b"]]
    in_specs = [_row_spec(tm, d), _row_spec(tm, d_y)] + [_const_spec(w.shape) for w in inputs[2:]]
    aliases = {}
    if router is None:
        out_specs = [_row_spec(tm, d)]
        out_shape = [jax.ShapeDtypeStruct((n, d), F32)]
    else:
        inputs.append(router)
        in_specs.append(_const_spec(router.shape))
        if shared is not None:
            aliases = {len(inputs): 0}
            inputs.append(shared)
            in_specs.append(pl.BlockSpec(memory_space=pl.ANY))
        out_specs = [_row_spec(tm, d, row0 // tm), _row_spec(tm, ROUTER_LANES), _row_spec(tm, ROUTER_LANES)]
        out_shape = [jax.ShapeDtypeStruct((shared_rows, d), F32),
                     jax.ShapeDtypeStruct((n, ROUTER_LANES), jnp.int32),
                     jax.ShapeDtypeStruct((n, ROUTER_LANES), F32)]
    kern = functools.partial(_merge_kernel, alpha=alpha, d_s5=d_s5, d_pool=d_pool, d_conv=d_conv,
                             n_exp=n_exp if router is not None else 0, aliased=shared is not None, precise=precise)
    return pl.pallas_call(
        kern,
        grid=(n // tm,),
        in_specs=in_specs,
        out_specs=tuple(out_specs),
        out_shape=tuple(out_shape),
        input_output_aliases=aliases,
        compiler_params=_params("parallel"),
        name="merge_outproj_ln1",
    )(*inputs)


def _swiglu_tile(x, wg_ref, wu_ref, wd_ref, ff_chunk, precise):
    d_ff = wg_ref.shape[0]
    acc = None
    for c0 in range(0, d_ff, ff_chunk):
        c1 = min(c0 + ff_chunk, d_ff)
        g = _mm(x, wg_ref[c0:c1, :], precise, w_transposed=True)
        u = _mm(x, wu_ref[c0:c1, :], precise, w_transposed=True)
        part = _mm(_silu(g) * u, wd_ref[c0:c1, :], precise)
        acc = part if acc is None else acc + part
    return acc


def _ffn_kernel(x_ref, wg_ref, wu_ref, wd_ref, g_ref, b_ref, o_ref, *, alpha, ff_chunk, precise):
    x = x_ref[...]
    f = _swiglu_tile(x if precise else x.astype(BF16), wg_ref, wu_ref, wd_ref, ff_chunk, precise)
    o_ref[...] = _layer_norm(alpha * x + f, g_ref[...], b_ref[...])


def _ffn_dense(x, wg, wu, wd, ln_g, ln_b, *, alpha, tm, ff_chunk, precise):
    n, d = x.shape
    weights = [wg, wu, wd, ln_g, ln_b]
    return pl.pallas_call(
        functools.partial(_ffn_kernel, alpha=alpha, ff_chunk=ff_chunk, precise=precise),
        grid=(n // tm,),
        in_specs=[_row_spec(tm, d)] + [_const_spec(w.shape) for w in weights],
        out_specs=_row_spec(tm, d),
        out_shape=jax.ShapeDtypeStruct((n, d), F32),
        compiler_params=_params("parallel"),
        name="ffn_dense_ln2",
    )(x, *weights)


def _moe_kernel(te_ref, nt_ref, xs_ref, wg_ref, wu_ref, wd_ref, *rest, ff_chunk, tile0):
    o_ref = rest[-1]
    i = tile0 + pl.program_id(0)

    @pl.when(i < nt_ref[0])
    def _():
        o_ref[...] = _swiglu_tile(xs_ref[...].astype(BF16), wg_ref, wu_ref, wd_ref, ff_chunk, False)

    @pl.when(i >= nt_ref[0])
    def _():
        o_ref[...] = jnp.zeros(o_ref.shape, o_ref.dtype)


def _moe_grouped(tile_expert, n_tiles_used, xs, wg, wu, wd, *, tm, ff_chunk, tile0, n_slots, y_prev=None):
    n_part, d = xs.shape
    d_ff = wd.shape[1]

    def expert_spec(shape):
        return pl.BlockSpec((None,) + shape, lambda i, te, nt: (te[tile0 + i], 0, 0),
                            pipeline_mode=pl.Buffered(1))

    inputs = [tile_expert, n_tiles_used, xs, wg, wu, wd]
    in_specs = [pl.BlockSpec((tm, d), lambda i, te, nt: (i, 0)),
                expert_spec((d_ff, d)), expert_spec((d_ff, d)), expert_spec((d_ff, d))]
    aliases = {}
    if y_prev is not None:
        aliases = {len(inputs): 0}
        inputs.append(y_prev)
        in_specs.append(pl.BlockSpec(memory_space=pl.ANY))
    grid_spec = pltpu.PrefetchScalarGridSpec(
        num_scalar_prefetch=2,
        grid=(n_part // tm,),
        in_specs=in_specs,
        out_specs=pl.BlockSpec((tm, d), lambda i, te, nt: (tile0 + i, 0)),
    )
    return pl.pallas_call(
        functools.partial(_moe_kernel, ff_chunk=ff_chunk, tile0=tile0),
        grid_spec=grid_spec,
        out_shape=jax.ShapeDtypeStruct((n_slots, d), F32),
        input_output_aliases=aliases,
        compiler_params=_params("arbitrary"),
        name="moe_grouped_swiglu",
    )(*inputs)


def _combine_kernel(x_ref, ya_ref, yb_ref, rw_ref, g_ref, b_ref, *rest, alpha, nb):
    f = rw_ref[:, 0:1] * ya_ref[...] + rw_ref[:, 1:2] * yb_ref[...]
    y = _layer_norm(alpha * x_ref[...] + f, g_ref[...], b_ref[...])
    if nb is None:
        rest[-1][...] = y
        return
    o_ref, t_scr = rest[-2], rest[-1]
    steps = y.shape[0] // nb
    for k in range(t_scr.shape[0]):
        t_scr[k] = y[:, k * LANES:(k + 1) * LANES]
    for b in range(nb):
        for k in range(t_scr.shape[0]):
            o_ref[b, :, k * LANES:(k + 1) * LANES] = t_scr[k, pl.ds(b, steps, stride=nb), :]


def _moe_combine(x1, ya, yb, rw, ln_g, ln_b, *, alpha, tm, x_row0, rows, nb=None, n_seq_steps=None, out_prev=None):
    r0, cnt = rows
    d = x1.shape[1]
    inputs = [x1, ya, yb, rw, ln_g, ln_b]
    in_specs = [_row_spec(tm, d, (x_row0 + r0) // tm), _row_spec(tm, d), _row_spec(tm, d),
                _row_spec(tm, ROUTER_LANES, r0 // tm), _const_spec(ln_g.shape), _const_spec(ln_b.shape)]
    aliases = {}
    if out_prev is not None:
        aliases = {len(inputs): 0}
        inputs.append(out_prev)
        in_specs.append(pl.BlockSpec(memory_space=pl.ANY))
    if nb is None:
        out_specs = _row_spec(tm, d)
        out_shape = jax.ShapeDtypeStruct((cnt, d), F32)
        scratch = []
    else:
        steps = tm // nb
        out_specs = pl.BlockSpec((nb, steps, d), lambda i: (0, r0 // tm + i, 0))
        out_shape = jax.ShapeDtypeStruct((nb, n_seq_steps, d), F32)
        scratch = [pltpu.VMEM((d // LANES, tm, LANES), F32)]
    return pl.pallas_call(
        functools.partial(_combine_kernel, alpha=alpha, nb=nb),
        grid=(cnt // tm,),
        in_specs=in_specs,
        out_specs=out_specs,
        out_shape=out_shape,
        scratch_shapes=scratch,
        input_output_aliases=aliases,
        compiler_params=_params("parallel"),
        name="moe_combine_ln2",
    )(*inputs)


def _moe_route(top_i, n_experts, tm):
    n = top_i.shape[0]
    n_flat = TOP_K * n
    flat_e = top_i.reshape(-1)
    iota = jnp.arange(n_flat, dtype=jnp.int32)
    sorted_e, sorted_j = lax.sort((flat_e, iota), num_keys=1, is_stable=True)
    experts = jnp.arange(n_experts, dtype=jnp.int32)
    counts = jnp.sum((flat_e[:, None] == experts[None, :]).astype(jnp.int32), axis=0)
    start_end = jnp.cumsum(counts)
    padded = ((counts + tm - 1) // tm) * tm
    off_end = jnp.cumsum(padded)
    off = off_end - padded
    shift = off - (start_end - counts)
    n_slots = ((n_flat + n_experts * tm + tm - 1) // tm) * tm
    n_tiles = n_slots // tm
    tile_start = jnp.arange(n_tiles, dtype=jnp.int32) * tm
    tile_expert = jnp.sum((tile_start[:, None] >= off_end[None, :]).astype(jnp.int32), axis=1)
    tile_expert = jnp.minimum(tile_expert, n_experts - 1).astype(jnp.int32)
    n_used = (off_end[-1] // tm).astype(jnp.int32).reshape(1)
    slot = jnp.arange(n_slots, dtype=jnp.int32).reshape(n_tiles, tm)
    t_off, t_cnt, t_shift = off[tile_expert][:, None], counts[tile_expert][:, None], shift[tile_expert][:, None]
    valid = (slot - t_off) < t_cnt
    src = jnp.clip(slot - t_shift, 0, n_flat - 1)
    slot_tok = jnp.where(valid, sorted_j[src] // TOP_K, 0).reshape(n_slots)
    slot_of_sorted = iota + shift[sorted_e]
    _, slots = lax.sort((sorted_j, slot_of_sorted), num_keys=1)
    return slot_tok, tile_expert, n_used, slots.reshape(n, TOP_K)


def _pad_time_major(cache, steps):
    n_b, n_s, n_c = cache.shape
    tmaj = jnp.transpose(cache, (1, 0, 2))
    tmaj = jnp.pad(tmaj, ((steps - n_s, 0), (0, 0), (0, 0)))
    return tmaj.reshape(steps * n_b, n_c)


def _from_time_major(flat, n_b):
    n_c = flat.shape[1]
    return jnp.transpose(flat.reshape(-1, n_b, n_c), (1, 0, 2))


def kernel(x_prompt, x_sample, state_s5_re, state_s5_im, cache_pool, cache_conv, w_in, b_in, s5_a_re, s5_a_im, s5_log_dt, s5_b_re, s5_b_im, s5_c_re, s5_c_im, s5_d, s5_w_glu, s5_b_glu, pool_w, pool_scale, conv_w, conv_b, conv_ln_g, conv_ln_b, proj_s5, proj_pool, proj_conv, w_out, b_out, ln1_g, ln1_b, ln2_g, ln2_b, ffn_w_gate, ffn_w_up, ffn_w_down, moe_router, moe_w_gate, moe_w_up, moe_w_down):
    depth, d_model, _ = w_in.shape
    n_bp, n_tp, _ = x_prompt.shape
    n_bs, n_ts, _ = x_sample.shape
    assert n_ts == 1
    n_grp, n_p = s5_a_re.shape[1:]
    n_h = s5_b_re.shape[-1]
    n_state = n_grp * n_p
    d_s5 = n_grp * n_h
    d_pool = pool_scale.shape[1]
    d_conv = conv_w.shape[2]
    n_taps = conv_w.shape[1]
    d_lin = d_s5 + d_pool
    d_u = d_lin + 2 * d_conv
    n_exp = moe_router.shape[-1]
    alpha = (2.0 * depth) ** 0.25
    ff_chunk = MXU_DIM
    assert cache_pool.shape[2] == POOL_HIST - 1 and n_taps - 1 <= CONV_HIST
    n_p_tok = n_bp * n_tp
    n_tok = n_p_tok + n_bs
    tc_p = min(64, n_tp)
    tm_p = tc_p * n_bp

    def row(v):
        return v.reshape(1, -1).astype(F32)

    def both(w):
        w = w.astype(F32)
        return w.astype(BF16), w

    xp = jnp.transpose(x_prompt, (1, 0, 2)).reshape(n_p_tok, d_model)
    xs = x_sample.reshape(n_bs, d_model)
    zeros_p = dict(h0=jnp.zeros((n_bp, 2 * n_state), F32),
                   pool=jnp.zeros((POOL_HIST * n_bp, d_pool), F32),
                   conv=jnp.zeros((CONV_HIST * n_bp, d_conv), F32))
    outs = {k: [] for k in ("re_p", "im_p", "pool_p", "conv_p", "re_s", "im_s", "pool_s", "conv_s")}

    for l in range(depth):
        lam, bb_re, bb_im = _s5_discretise(s5_a_re[l], s5_a_im[l], s5_log_dt[l], s5_b_re[l], s5_b_im[l])
        conv_w_pad = jnp.pad(conv_w[l].astype(F32), ((0, CONV_HIST - n_taps), (0, 0)))
        shared_mw = dict(lam=lam, d=row(s5_d[l]), bglu=row(s5_b_glu[l]), pscale=row(pool_scale[l]),
                         conv_w=conv_w_pad, conv_b=row(conv_b[l]), conv_ln_g=row(conv_ln_g[l]),
                         conv_ln_b=row(conv_ln_b[l]), n_taps=n_taps)
        mats = dict(wb=both(_s5_input_blocks(bb_re, bb_im, n_grp, n_p)),
                    wc=both(_s5_output_blocks(s5_c_re[l].astype(F32), s5_c_im[l].astype(F32), n_p)),
                    wglu=both(s5_w_glu[l]), wpool=both(_block_diag(pool_w[l].astype(F32))))
        w1 = both(w_in[l, :, :d_u])
        b1 = row(b_in[l, :d_u])
        shared_lw = dict(b_gates=row(b_in[l, d_u:]), b_out=row(b_out[l]), ln1_g=row(ln1_g[l]), ln1_b=row(ln1_b[l]))
        lmats = dict(w_gates=both(w_in[l, :, d_u:]), proj_s5=both(proj_s5[l]), proj_pool=both(proj_pool[l]),
                     proj_conv=both(proj_conv[l]), w_out=both(w_out[l]))
        is_moe = l % 2 == 1
        router = None
        if is_moe:
            assert n_exp <= ROUTER_LANES // 2
            router = jnp.pad(moe_router[l // 2].astype(F32), ((0, 0), (0, ROUTER_LANES - n_exp)))

        mw = dict(shared_mw, **{k: v[0] for k, v in mats.items()})
        lw = dict(shared_lw, **{k: v[0] for k, v in lmats.items()})
        up = _inproj(xp, w1[0], b1, d_lin=d_lin, d_conv=d_conv, tm=tm_p, precise=False)
        yp, hT_p, pool_p, conv_p = _mixers(up, zeros_p["h0"], zeros_p["pool"], zeros_p["conv"], mw,
                                           nb=n_bp, tc=tc_p, start_pos=0, precise=False)
        res_p = _merge(xp, yp, lw, alpha=alpha, tm=tm_p, precise=False, router=router, n_exp=n_exp,
                       shared_rows=n_tok)
        mw = dict(shared_mw, **{k: v[1] for k, v in mats.items()})
        lw = dict(shared_lw, **{k: v[1] for k, v in lmats.items()})
        h0_s = jnp.concatenate([state_s5_re[l].reshape(n_bs, n_state), state_s5_im[l].reshape(n_bs, n_state)],
                               axis=1).astype(F32)
        us = _inproj(xs, w1[1], b1, d_lin=d_lin, d_conv=d_conv, tm=n_bs, precise=True)
        ys, hT_s, pool_s, conv_s = _mixers(us, h0_s, _pad_time_major(cache_pool[l].astype(F32), POOL_HIST),
                                           _pad_time_major(cache_conv[l].astype(F32), CONV_HIST), mw,
                                           nb=n_bs, tc=1, start_pos=PAST_LEN, precise=True)
        res_s = _merge(xs, ys, lw, alpha=alpha, tm=n_bs, precise=True, router=router, n_exp=n_exp,
                       shared=res_p[0] if is_moe else None, shared_rows=n_tok, row0=n_p_tok)

        outs["re_p"].append(hT_p[:, :n_state].reshape(n_bp, n_grp, n_p))
        outs["im_p"].append(hT_p[:, n_state:].reshape(n_bp, n_grp, n_p))
        outs["pool_p"].append(_from_time_major(pool_p, n_bp))
        outs["conv_p"].append(_from_time_major(conv_p, n_bp))
        outs["re_s"].append(hT_s[:, :n_state].reshape(n_bs, n_grp, n_p))
        outs["im_s"].append(hT_s[:, n_state:].reshape(n_bs, n_grp, n_p))
        outs["pool_s"].append(_from_time_major(pool_s, n_bs))
        outs["conv_s"].append(_from_time_major(conv_s, n_bs))

        g2, b2 = row(ln2_g[l]), row(ln2_b[l])
        i = l // 2
        if not is_moe:
            wg, wu = jnp.swapaxes(ffn_w_gate[i], -1, -2).astype(F32), jnp.swapaxes(ffn_w_up[i], -1, -2).astype(F32)
            wd = ffn_w_down[i].astype(F32)
            xp = _ffn_dense(res_p[0], wg, wu, wd, g2, b2, alpha=alpha, tm=tm_p, ff_chunk=ff_chunk,
                            precise=False)
            xs = _ffn_dense(res_s[0], wg, wu, wd, g2, b2, alpha=alpha, tm=n_bs, ff_chunk=ff_chunk,
                            precise=True)
        else:
            wg, wu = jnp.swapaxes(moe_w_gate[i], -1, -2).astype(F32), jnp.swapaxes(moe_w_up[i], -1, -2).astype(F32)
            wd = moe_w_down[i].astype(F32)
            _, ri_p, rw_p = res_p
            x1_all, ri_s, rw_s = res_s
            top_i = jnp.concatenate([ri_p[:, :TOP_K], ri_s[:, :TOP_K]], axis=0)
            tm_moe = 512
            slot_tok, tile_expert, n_used, slots = _moe_route(top_i, n_exp, tm_moe)
            n_tiles = slot_tok.shape[0] // tm_moe
            bounds = [(n_tiles * k) // MOE_PARTS for k in range(MOE_PARTS + 1)]
            y_sorted = None
            for t0, t1 in zip(bounds[:-1], bounds[1:]):
                x_part = x1_all.at[slot_tok[t0 * tm_moe:t1 * tm_moe]].get(mode="promise_in_bounds")
                y_sorted = _moe_grouped(tile_expert, n_used, x_part, wg, wu, wd, tm=tm_moe, ff_chunk=ff_chunk,
                                        tile0=t0, n_slots=slot_tok.shape[0], y_prev=y_sorted)
            def picked(r0, cnt):
                return [y_sorted.at[slots[r0:r0 + cnt, k]].get(mode="promise_in_bounds") for k in range(TOP_K)]

            last_layer = l == depth - 1
            p_tiles = n_p_tok // tm_p
            bounds = sorted({tm_p * ((p_tiles * k) // MOE_PARTS) for k in range(MOE_PARTS + 1)})
            xp = None
            for r0, r1 in zip(bounds[:-1], bounds[1:]):
                ya, yb = picked(r0, r1 - r0)
                if last_layer:
                    xp = _moe_combine(x1_all, ya, yb, rw_p, g2, b2, alpha=alpha, tm=tm_p, x_row0=0,
                                      rows=(r0, r1 - r0), nb=n_bp, n_seq_steps=n_tp, out_prev=xp)
                else:
                    part = _moe_combine(x1_all, ya, yb, rw_p, g2, b2, alpha=alpha, tm=tm_p, x_row0=0,
                                        rows=(r0, r1 - r0))
                    xp = part if xp is None else jnp.concatenate([xp, part], axis=0)
            ya, yb = picked(n_p_tok, n_bs)
            xs = _moe_combine(x1_all, ya, yb, rw_s, g2, b2, alpha=alpha, tm=n_bs, x_row0=n_p_tok, rows=(0, n_bs))

    y_prompt = xp if xp.ndim == 3 else _from_time_major(xp, n_bp)
    st = lambda k: jnp.stack(outs[k])
    return (y_prompt, xs.reshape(n_bs, 1, d_model), st("re_p"), st("im_p"), st("pool_p"), st("conv_p"),
            st("re_s"), st("im_s"), st("pool_s"), st("conv_s"))
```

```python
import functools

import jax
import jax.numpy as jnp
from jax import lax
from jax.experimental import pallas as pl
from jax.experimental.pallas import tpu as pltpu

F32 = jnp.float32
BF16 = jnp.bfloat16

LN_EPS = 1e-5
POOL_WINDOWS = (2, 4, 8, 16)
PAST_LEN = 16384
TOP_K = 2
POOL_HIST = 16
CONV_HIST = 32
ROUTER_LANES = 128
LANES = 128
MXU_DIM = 256
MOE_PARTS = 4
MOE_PART_ENDS = (0.12, 0.4, 0.7, 1.0)
VMEM_LIMIT = 56 * 1024 * 1024


def _sigmoid(x):
    return 0.5 * jnp.tanh(0.5 * x) + 0.5


def _silu(x):
    return x * _sigmoid(x)


def _layer_norm(x, g, b):
    mu = jnp.mean(x, axis=-1, keepdims=True)
    xc = x - mu
    var = jnp.mean(xc * xc, axis=-1, keepdims=True)
    return xc * lax.rsqrt(var + LN_EPS) * g + b


def _dot(a, b, w_transposed=False):
    dims = (((1,), (1 if w_transposed else 0,)), ((), ()))
    return lax.dot_general(a, b, dims, preferred_element_type=F32)


def _split(v):
    hi = v.astype(BF16)
    return hi, (v - hi.astype(F32)).astype(BF16)


def _mm(a, w, precise, w_transposed=False):
    if not precise:
        return _dot(a.astype(BF16), w.astype(BF16), w_transposed)
    a_hi, a_lo = _split(a)
    w_hi, w_lo = _split(w)
    m = a.shape[0]
    r = _dot(jnp.concatenate([a_hi, a_lo], axis=0), w_hi, w_transposed)
    return r[:m] + r[m:] + _dot(a_hi, w_lo, w_transposed)


def _params(*sem):
    return pltpu.CompilerParams(dimension_semantics=sem, vmem_limit_bytes=VMEM_LIMIT)


def _const_spec(shape):
    return pl.BlockSpec(shape, lambda *_: (0,) * len(shape), pipeline_mode=pl.Buffered(1))


def _row_spec(tm, d, blk0=0):
    return pl.BlockSpec((tm, d), lambda i: (blk0 + i, 0))


def _s5_disc_kernel(are_ref, aim_ref, ldt_ref, bre_ref, bim_ref, lam_ref, bbre_ref, bbim_ref):
    a_re = are_ref[...]
    a_im = aim_ref[...]
    dt = jnp.exp(ldt_ref[...])
    mag = jnp.exp(a_re * dt)
    lr = mag * jnp.cos(a_im * dt)
    li = mag * jnp.sin(a_im * dt)
    den = a_re * a_re + a_im * a_im
    nr = lr - 1.0
    cr = (nr * a_re + li * a_im) / den
    ci = (li * a_re - nr * a_im) / den
    lam_ref[0:1, :] = lr
    lam_ref[1:2, :] = li
    b_re = bre_ref[...]
    b_im = bim_ref[...]
    bbre_ref[...] = cr * b_re - ci * b_im
    bbim_ref[...] = cr * b_im + ci * b_re


def _s5_discretise(a_re, a_im, log_dt, b_re, b_im):
    n_g, n_p = a_re.shape
    n_h = b_re.shape[-1]
    n_s = n_g * n_p
    ldt = jnp.broadcast_to(log_dt[:, None], (n_g, n_p)).reshape(1, n_s)
    b_re_t = b_re.reshape(n_s, n_h).T
    b_im_t = b_im.reshape(n_s, n_h).T
    return pl.pallas_call(
        _s5_disc_kernel,
        out_shape=(jax.ShapeDtypeStruct((2, n_s), F32),
                   jax.ShapeDtypeStruct((n_h, n_s), F32),
                   jax.ShapeDtypeStruct((n_h, n_s), F32)),
        name="s5_discretise",
    )(a_re.reshape(1, n_s), a_im.reshape(1, n_s), ldt, b_re_t, b_im_t)


def _block_diag(blocks):
    n_g, r, c = blocks.shape
    eye = jnp.eye(n_g, dtype=blocks.dtype)
    return (eye[:, None, :, None] * blocks[:, :, None, :]).reshape(n_g * r, n_g * c)


def _s5_input_blocks(bb_re, bb_im, n_grp, n_p):
    n_h = bb_re.shape[0]
    gpb = MXU_DIM // n_p
    ch = gpb * n_h
    assert MXU_DIM % n_p == 0 and LANES % ch == 0 and n_grp % gpb == 0
    n_blk = n_grp // gpb
    per_chunk = LANES // ch

    def blocks(bb):
        g = jnp.transpose(bb.reshape(n_h, n_grp, n_p), (1, 0, 2)).reshape(n_blk, gpb, n_h, n_p)
        dense = jax.vmap(_block_diag)(g)
        return jnp.stack([jnp.pad(dense[j], ((ch * (j % per_chunk), LANES - ch * (j % per_chunk + 1)), (0, 0)))
                          for j in range(n_blk)])

    return jnp.concatenate([blocks(bb_re), blocks(bb_im)], axis=2)


def _s5_output_blocks(c_re, c_im, n_p):
    n_grp, n_h, _ = c_re.shape
    gpb = MXU_DIM // n_h
    assert n_grp % gpb == 0
    n_blk = n_grp // gpb

    def blocks(cm):
        g = jnp.transpose(cm, (0, 2, 1)).reshape(n_blk, gpb, n_p, n_h)
        return jax.vmap(_block_diag)(g)

    return jnp.concatenate([blocks(c_re), blocks(-c_im)], axis=1)


def _mixer_kernel(x_ref, w1_ref, b1_ref, h0_ref, pc_ref, cc_ref, wb_ref, wc_ref, lam_ref, d_ref, wglu_ref, bglu_ref,
                  wpool_ref, pscale_ref, cw_ref, cb_ref, clg_ref, clb_ref,
                  y_ref, hT_ref, pnew_ref, cnew_ref,
                  u_ref, s_scr, h_scr, ph_scr, ch_scr, ca_scr, *, nb, tc, start_pos, d_s5, d_pool, d_conv, n_taps,
                  precise):
    c = pl.program_id(0)
    last = pl.num_programs(0) - 1
    rows = tc * nb
    n_state = lam_ref.shape[1]
    o_pool = d_s5
    o_conv = d_s5 + d_pool

    @pl.when(c == 0)
    def _():
        h_scr[...] = h0_ref[...]
        ph_scr[0:POOL_HIST * nb, :] = pc_ref[...]
        ch_scr[0:CONV_HIST * nb, :] = cc_ref[...]

    proj = _mm(x_ref[...], w1_ref[...], precise) + b1_ref[...]
    u_ref[:, 0:o_conv] = proj[:, 0:o_conv]
    u_ref[:, o_conv:] = proj[:, o_conv:o_conv + d_conv] * _sigmoid(proj[:, o_conv + d_conv:])

    u_s5 = u_ref[:, 0:d_s5]
    n_in_blk = wb_ref.shape[0]
    per_chunk = LANES // (d_s5 // n_in_blk)
    for j in range(n_in_blk):
        k0 = (j // per_chunk) * LANES
        bu = _mm(u_s5[:, k0:k0 + LANES], wb_ref[j], precise)
        s_scr[:, j * MXU_DIM:(j + 1) * MXU_DIM] = bu[:, :MXU_DIM]
        s_scr[:, n_state + j * MXU_DIM:n_state + (j + 1) * MXU_DIM] = bu[:, MXU_DIM:]
    if tc == 1:
        lr = lam_ref[0:1, :]
        li = lam_ref[1:2, :]
        h_re = h_scr[:, 0:n_state]
        h_im = h_scr[:, n_state:]
        n_re = lr * h_re - li * h_im + s_scr[:, 0:n_state]
        n_im = lr * h_im + li * h_re + s_scr[:, n_state:]
        s_scr[:, 0:n_state] = n_re
        s_scr[:, n_state:] = n_im
        h_scr[:, 0:n_state] = n_re
        h_scr[:, n_state:] = n_im
    else:
        assert nb == 8
        n_part = 2
        half = n_state // n_part
        for hf in range(n_part):
            re0 = hf * half
            im0 = n_state + hf * half
            lr = jnp.broadcast_to(lam_ref[0:1, re0:re0 + half], (nb, half))
            li = jnp.broadcast_to(lam_ref[1:2, re0:re0 + half], (nb, half))

            def step(t, carry, re0=re0, im0=im0, lr=lr, li=li):
                h_re, h_im = carry
                at_t = pl.ds(pl.multiple_of(t * nb, nb), nb)
                n_re = lr * h_re - li * h_im + s_scr[at_t, re0:re0 + half]
                n_im = lr * h_im + li * h_re + s_scr[at_t, im0:im0 + half]
                s_scr[at_t, re0:re0 + half] = n_re
                s_scr[at_t, im0:im0 + half] = n_im
                return n_re, n_im

            h_re, h_im = lax.fori_loop(0, tc, step, (h_scr[:, re0:re0 + half], h_scr[:, im0:im0 + half]),
                                       unroll=4)
            h_scr[:, re0:re0 + half] = h_re
            h_scr[:, im0:im0 + half] = h_im

    n_out_blk = wc_ref.shape[0]
    spb = n_state // n_out_blk
    ys = []
    for j in range(n_out_blk):
        h_re_b = s_scr[:, j * spb:(j + 1) * spb]
        h_im_b = s_scr[:, n_state + j * spb:n_state + (j + 1) * spb]
        ys.append(_mm(h_re_b, wc_ref[j, 0:spb, :], precise) + _mm(h_im_b, wc_ref[j, spb:, :], precise))
    y = jnp.concatenate(ys, axis=1) + d_ref[...] * u_s5
    y = jax.nn.gelu(y)
    y = y * _sigmoid(_mm(y, wglu_ref[...], precise) + bglu_ref[...])
    y_ref[:, 0:d_s5] = y.astype(y_ref.dtype)

    @pl.when(c == last)
    def _():
        hT_ref[...] = h_scr[...]

    u_pool = u_ref[:, o_pool:o_pool + d_pool]
    ph_scr[pl.ds(POOL_HIST * nb, rows), :] = u_pool
    a = ph_scr[...]
    row_t = lax.broadcasted_iota(jnp.int32, (rows, d_pool), 0) >> (nb.bit_length() - 1)
    pos1 = start_pos + c * tc + row_t + 1
    lane = lax.broadcasted_iota(jnp.int32, (rows, d_pool), 1)
    group = d_pool // len(POOL_WINDOWS)
    w = 1
    s_sel = None
    cnt_sel = None
    for gi, win in enumerate(POOL_WINDOWS):
        while w < win:
            a = a[w * nb:, :] + a[:a.shape[0] - w * nb, :]
            w *= 2
        assert w == win
        i0 = (POOL_HIST + 1 - win) * nb
        s_w = a[i0:i0 + rows, :]
        cnt_w = jnp.minimum(pos1, win).astype(F32)
        if s_sel is None:
            s_sel, cnt_sel = s_w, cnt_w
        else:
            in_later = lane >= gi * group
            s_sel = jnp.where(in_later, s_w, s_sel)
            cnt_sel = jnp.where(in_later, cnt_w, cnt_sel)
    pooled = s_sel / cnt_sel - u_pool
    y_pool = _mm(pooled, wpool_ref[...], precise) * pscale_ref[...]
    y_ref[:, o_pool:o_pool + d_pool] = y_pool.astype(y_ref.dtype)
    ph_scr[0:POOL_HIST * nb, :] = ph_scr[pl.ds(tc * nb, POOL_HIST * nb), :]

    @pl.when(c == last)
    def _():
        pnew_ref[...] = ph_scr[nb:POOL_HIST * nb, :]

    ch_scr[pl.ds(CONV_HIST * nb, rows), :] = u_ref[:, o_conv:o_conv + d_conv]
    rb = min(rows, 64)
    tap0 = CONV_HIST - (n_taps - 1)

    def conv_block(i, _):
        base = pl.multiple_of(i * rb, rb)
        acc = [None, None]
        for k in range(n_taps):
            term = cw_ref[k:k + 1, :] * ch_scr[pl.ds(base + (tap0 + k) * nb, rb), :]
            acc[k % 2] = term if acc[k % 2] is None else acc[k % 2] + term
        ca_scr[pl.ds(base, rb), :] = acc[0] + acc[1]
        return 0

    lax.fori_loop(0, rows // rb, conv_block, 0)
    yc = _silu(_layer_norm(ca_scr[...] + cb_ref[...], clg_ref[...], clb_ref[...]))
    y_ref[:, o_conv:o_conv + d_conv] = yc.astype(y_ref.dtype)
    ch_scr[0:CONV_HIST * nb, :] = ch_scr[pl.ds(tc * nb, CONV_HIST * nb), :]

    @pl.when(c == last)
    def _():
        cnew_ref[...] = ch_scr[tap0 * nb:CONV_HIST * nb, :]


def _mixers(x, w1, b1, h0, pool_cache, conv_cache, mw, *, nb, tc, start_pos, precise):
    n_rows, d = x.shape
    n_t = n_rows // nb
    rows = tc * nb
    d_s5 = mw["wglu"].shape[0]
    d_pool = mw["wpool"].shape[0]
    d_conv = mw["conv_w"].shape[1]
    n_taps = mw["n_taps"]
    d_y = d_s5 + d_pool + d_conv
    n_state2 = h0.shape[1]
    weights = [mw["wb"], mw["wc"], mw["lam"], mw["d"], mw["wglu"], mw["bglu"], mw["wpool"], mw["pscale"],
               mw["conv_w"], mw["conv_b"], mw["conv_ln_g"], mw["conv_ln_b"]]
    kern = functools.partial(_mixer_kernel, nb=nb, tc=tc, start_pos=start_pos, d_s5=d_s5, d_pool=d_pool,
                             d_conv=d_conv, n_taps=n_taps, precise=precise)
    return pl.pallas_call(
        kern,
        grid=(n_t // tc,),
        in_specs=[_row_spec(rows, d), _const_spec(w1.shape), _const_spec(b1.shape),
                  _const_spec(h0.shape), _const_spec(pool_cache.shape), _const_spec(conv_cache.shape)]
                 + [_const_spec(w.shape) for w in weights],
        out_specs=(_row_spec(rows, d_y),
                   pl.BlockSpec(h0.shape, lambda c: (0, 0)),
                   pl.BlockSpec(((POOL_HIST - 1) * nb, d_pool), lambda c: (0, 0)),
                   pl.BlockSpec(((n_taps - 1) * nb, d_conv), lambda c: (0, 0))),
        out_shape=(jax.ShapeDtypeStruct((n_rows, d_y), F32 if precise else BF16),
                   jax.ShapeDtypeStruct(h0.shape, F32),
                   jax.ShapeDtypeStruct(((POOL_HIST - 1) * nb, d_pool), F32),
                   jax.ShapeDtypeStruct(((n_taps - 1) * nb, d_conv), F32)),
        scratch_shapes=[pltpu.VMEM((rows, d_y), F32),
                        pltpu.VMEM((rows, n_state2), F32),
                        pltpu.VMEM((nb, n_state2), F32),
                        pltpu.VMEM(((POOL_HIST + tc) * nb, d_pool), F32),
                        pltpu.VMEM(((CONV_HIST + tc) * nb, d_conv), F32),
                        pltpu.VMEM((rows, d_conv), F32)],
        compiler_params=_params("arbitrary"),
        name="token_mixers",
    )(x, w1, b1, h0, pool_cache, conv_cache, *weights)


def _merge_kernel(*refs, alpha, d_s5, d_pool, d_conv, n_exp, aliased, precise):
    with_router = n_exp > 0
    n_in = 11 + (1 if with_router else 0) + (1 if aliased else 0)
    x_ref, y_ref, wg_ref, bg_ref, ps5_ref, ppool_ref, pconv_ref, wo_ref, bo_ref, g_ref, b_ref = refs[:11]
    o_ref = refs[n_in]
    x = x_ref[...]
    xa = x if precise else x.astype(BF16)
    d = x.shape[-1]
    merged = None
    col = 0
    for j, (width, p_ref) in enumerate(((d_s5, ps5_ref), (d_pool, ppool_ref), (d_conv, pconv_ref))):
        gate = _sigmoid(_mm(xa, wg_ref[:, j * d:(j + 1) * d], precise) + bg_ref[:, j * d:(j + 1) * d])
        term = gate * _mm(y_ref[:, col:col + width], p_ref[...], precise)
        merged = term if merged is None else merged + term
        col += width
    out = _mm(merged, wo_ref[...], precise) + bo_ref[...]
    x1 = _layer_norm(alpha * x + out, g_ref[...], b_ref[...])
    o_ref[...] = x1
    if with_router:
        rt_ref = refs[11]
        ri_ref, rw_ref = refs[n_in + 1:n_in + 3]
        x_hi, x_lo = _split(x1)
        rt = rt_ref[...]
        rt_hi = rt.astype(BF16).astype(F32)
        rt2 = (rt_hi + pltpu.roll(rt - rt_hi, ROUTER_LANES // 2, axis=1)).astype(BF16)
        r = _dot(x_hi, rt2) + _dot(x_lo, rt2)
        logits = r + pltpu.roll(r, ROUTER_LANES // 2, axis=1)
        lane = lax.broadcasted_iota(jnp.int32, logits.shape, 1).astype(F32)
        neg = jnp.float32(-jnp.inf)
        l1 = jnp.where(lane < n_exp, logits, neg)
        m1 = jnp.max(l1, axis=-1, keepdims=True)
        i1 = jnp.min(jnp.where(l1 == m1, lane, float(ROUTER_LANES)), axis=-1, keepdims=True)
        l2 = jnp.where(lane == i1, neg, l1)
        m2 = jnp.max(l2, axis=-1, keepdims=True)
        i2 = jnp.min(jnp.where(l2 == m2, lane, float(ROUTER_LANES)), axis=-1, keepdims=True)
        e2 = jnp.exp(m2 - m1)
        w1 = 1.0 / (1.0 + e2)
        w2 = e2 * w1
        ri_ref[...] = jnp.where(lane == 0.0, i1, jnp.where(lane == 1.0, i2, 0.0)).astype(jnp.int32)
        rw_ref[...] = jnp.where(lane == 0.0, w1, jnp.where(lane == 1.0, w2, 0.0))


def _merge(x, y, lw, *, alpha, tm, precise, router=None, n_exp=0, shared=None, shared_rows=0, row0=0):
    n, d = x.shape
    d_y = y.shape[-1]
    d_s5, d_pool, d_conv = lw["proj_s5"].shape[0], lw["proj_pool"].shape[0], lw["proj_conv"].shape[0]
    inputs = [x, y, lw["w_gates"], lw["b_gates"], lw["proj_s5"], lw["proj_pool"], lw["proj_conv"], lw["w_out"],
              lw["b_out"], lw["ln1_g"], lw["ln1_b"]]
    in_specs = [_row_spec(tm, d), _row_spec(tm, d_y)] + [_const_spec(w.shape) for w in inputs[2:]]
    aliases = {}
    if router is None:
        out_specs = [_row_spec(tm, d)]
        out_shape = [jax.ShapeDtypeStruct((n, d), F32)]
    else:
        inputs.append(router)
        in_specs.append(_const_spec(router.shape))
        if shared is not None:
            aliases = {len(inputs): 0}
            inputs.append(shared)
            in_specs.append(pl.BlockSpec(memory_space=pl.ANY))
        out_specs = [_row_spec(tm, d, row0 // tm), _row_spec(tm, ROUTER_LANES), _row_spec(tm, ROUTER_LANES)]
        out_shape = [jax.ShapeDtypeStruct((shared_rows, d), F32),
                     jax.ShapeDtypeStruct((n, ROUTER_LANES), jnp.int32),
                     jax.ShapeDtypeStruct((n, ROUTER_LANES), F32)]
    kern = functools.partial(_merge_kernel, alpha=alpha, d_s5=d_s5, d_pool=d_pool, d_conv=d_conv,
                             n_exp=n_exp if router is not None else 0, aliased=shared is not None, precise=precise)
    return pl.pallas_call(
        kern,
        grid=(n // tm,),
        in_specs=in_specs,
        out_specs=tuple(out_specs),
        out_shape=tuple(out_shape),
        input_output_aliases=aliases,
        compiler_params=_params("parallel"),
        name="merge_outproj_ln1",
    )(*inputs)


def _swiglu_tile(x, wg_ref, wu_ref, wd_ref, ff_chunk, precise):
    d_ff = wg_ref.shape[0]
    acc = None
    for c0 in range(0, d_ff, ff_chunk):
        c1 = min(c0 + ff_chunk, d_ff)
        g = _mm(x, wg_ref[c0:c1, :], precise, w_transposed=True)
        u = _mm(x, wu_ref[c0:c1, :], precise, w_transposed=True)
        part = _mm(_silu(g) * u, wd_ref[c0:c1, :], precise)
        acc = part if acc is None else acc + part
    return acc


def _ffn_kernel(x_ref, wg_ref, wu_ref, wd_ref, g_ref, b_ref, o_ref, *, alpha, ff_chunk, precise):
    x = x_ref[...]
    f = _swiglu_tile(x if precise else x.astype(BF16), wg_ref, wu_ref, wd_ref, ff_chunk, precise)
    o_ref[...] = _layer_norm(alpha * x + f, g_ref[...], b_ref[...])


def _ffn_dense(x, wg, wu, wd, ln_g, ln_b, *, alpha, tm, ff_chunk, precise):
    n, d = x.shape
    weights = [wg, wu, wd, ln_g, ln_b]
    return pl.pallas_call(
        functools.partial(_ffn_kernel, alpha=alpha, ff_chunk=ff_chunk, precise=precise),
        grid=(n // tm,),
        in_specs=[_row_spec(tm, d)] + [_const_spec(w.shape) for w in weights],
        out_specs=_row_spec(tm, d),
        out_shape=jax.ShapeDtypeStruct((n, d), F32),
        compiler_params=_params("parallel"),
        name="ffn_dense_ln2",
    )(x, *weights)


def _moe_kernel(te_ref, nt_ref, xs_ref, wg_ref, wu_ref, wd_ref, *rest, ff_chunk, tile0):
    o_ref = rest[-1]
    i = tile0 + pl.program_id(0)

    @pl.when(i < nt_ref[0])
    def _():
        o_ref[...] = _swiglu_tile(xs_ref[...].astype(BF16), wg_ref, wu_ref, wd_ref, ff_chunk, False)

    @pl.when(i >= nt_ref[0])
    def _():
        o_ref[...] = jnp.zeros(o_ref.shape, o_ref.dtype)


def _moe_grouped(tile_expert, n_tiles_used, xs, wg, wu, wd, *, tm, ff_chunk, tile0, n_slots, y_prev=None):
    n_part, d = xs.shape
    d_ff = wd.shape[1]

    def expert_spec(shape):
        return pl.BlockSpec((None,) + shape, lambda i, te, nt: (te[tile0 + i], 0, 0),
                            pipeline_mode=pl.Buffered(1))

    inputs = [tile_expert, n_tiles_used, xs, wg, wu, wd]
    in_specs = [pl.BlockSpec((tm, d), lambda i, te, nt: (i, 0)),
                expert_spec((d_ff, d)), expert_spec((d_ff, d)), expert_spec((d_ff, d))]
    aliases = {}
    if y_prev is not None:
        aliases = {len(inputs): 0}
        inputs.append(y_prev)
        in_specs.append(pl.BlockSpec(memory_space=pl.ANY))
    grid_spec = pltpu.PrefetchScalarGridSpec(
        num_scalar_prefetch=2,
        grid=(n_part // tm,),
        in_specs=in_specs,
        out_specs=pl.BlockSpec((tm, d), lambda i, te, nt: (tile0 + i, 0)),
    )
    return pl.pallas_call(
        functools.partial(_moe_kernel, ff_chunk=ff_chunk, tile0=tile0),
        grid_spec=grid_spec,
        out_shape=jax.ShapeDtypeStruct((n_slots, d), F32),
        input_output_aliases=aliases,
        compiler_params=_params("arbitrary"),
        name="moe_grouped_swiglu",
    )(*inputs)


def _combine_kernel(x_ref, ya_ref, yb_ref, rw_ref, g_ref, b_ref, *rest, alpha, nb):
    f = rw_ref[:, 0:1] * ya_ref[...] + rw_ref[:, 1:2] * yb_ref[...]
    y = _layer_norm(alpha * x_ref[...] + f, g_ref[...], b_ref[...])
    if nb is None:
        rest[-1][...] = y
        return
    o_ref, t_scr = rest[-2], rest[-1]
    steps = y.shape[0] // nb
    for k in range(t_scr.shape[0]):
        t_scr[k] = y[:, k * LANES:(k + 1) * LANES]
    for b in range(nb):
        for k in range(t_scr.shape[0]):
            o_ref[b, :, k * LANES:(k + 1) * LANES] = t_scr[k, pl.ds(b, steps, stride=nb), :]


def _moe_combine(x1, ya, yb, rw, ln_g, ln_b, *, alpha, tm, x_row0, rows, nb=None, n_seq_steps=None, out_prev=None):
    r0, cnt = rows
    d = x1.shape[1]
    inputs = [x1, ya, yb, rw, ln_g, ln_b]
    in_specs = [_row_spec(tm, d, (x_row0 + r0) // tm), _row_spec(tm, d), _row_spec(tm, d),
                _row_spec(tm, ROUTER_LANES, r0 // tm), _const_spec(ln_g.shape), _const_spec(ln_b.shape)]
    aliases = {}
    if out_prev is not None:
        aliases = {len(inputs): 0}
        inputs.append(out_prev)
        in_specs.append(pl.BlockSpec(memory_space=pl.ANY))
    if nb is None:
        out_specs = _row_spec(tm, d)
        out_shape = jax.ShapeDtypeStruct((cnt, d), F32)
        scratch = []
    else:
        steps = tm // nb
        out_specs = pl.BlockSpec((nb, steps, d), lambda i: (0, r0 // tm + i, 0))
        out_shape = jax.ShapeDtypeStruct((nb, n_seq_steps, d), F32)
        scratch = [pltpu.VMEM((d // LANES, tm, LANES), F32)]
    return pl.pallas_call(
        functools.partial(_combine_kernel, alpha=alpha, nb=nb),
        grid=(cnt // tm,),
        in_specs=in_specs,
        out_specs=out_specs,
        out_shape=out_shape,
        scratch_shapes=scratch,
        input_output_aliases=aliases,
        compiler_params=_params("parallel"),
        name="moe_combine_ln2",
    )(*inputs)


def _moe_route(top_i, n_experts, tm):
    n = top_i.shape[0]
    n_flat = TOP_K * n
    flat_e = top_i.reshape(-1)
    iota = jnp.arange(n_flat, dtype=jnp.int32)
    sorted_e, sorted_j = lax.sort((flat_e, iota), num_keys=1, is_stable=True)
    experts = jnp.arange(n_experts, dtype=jnp.int32)
    counts = jnp.sum((flat_e[:, None] == experts[None, :]).astype(jnp.int32), axis=0)
    start_end = jnp.cumsum(counts)
    padded = ((counts + tm - 1) // tm) * tm
    off_end = jnp.cumsum(padded)
    off = off_end - padded
    shift = off - (start_end - counts)
    n_slots = ((n_flat + n_experts * tm + tm - 1) // tm) * tm
    n_tiles = n_slots // tm
    tile_start = jnp.arange(n_tiles, dtype=jnp.int32) * tm
    tile_expert = jnp.sum((tile_start[:, None] >= off_end[None, :]).astype(jnp.int32), axis=1)
    tile_expert = jnp.minimum(tile_expert, n_experts - 1).astype(jnp.int32)
    n_used = (off_end[-1] // tm).astype(jnp.int32).reshape(1)
    slot = jnp.arange(n_slots, dtype=jnp.int32).reshape(n_tiles, tm)
    t_off, t_cnt, t_shift = off[tile_expert][:, None], counts[tile_expert][:, None], shift[tile_expert][:, None]
    valid = (slot - t_off) < t_cnt
    src = jnp.clip(slot - t_shift, 0, n_flat - 1)
    slot_tok = jnp.where(valid, sorted_j[src] // TOP_K, 0).reshape(n_slots)
    slot_of_sorted = iota + shift[sorted_e]
    _, slots = lax.sort((sorted_j, slot_of_sorted), num_keys=1)
    return slot_tok, tile_expert, n_used, slots.reshape(n, TOP_K)


def _pad_time_major(cache, steps):
    n_b, n_s, n_c = cache.shape
    tmaj = jnp.transpose(cache, (1, 0, 2))
    tmaj = jnp.pad(tmaj, ((steps - n_s, 0), (0, 0), (0, 0)))
    return tmaj.reshape(steps * n_b, n_c)


def _from_time_major(flat, n_b):
    n_c = flat.shape[1]
    return jnp.transpose(flat.reshape(-1, n_b, n_c), (1, 0, 2))


def kernel(x_prompt, x_sample, state_s5_re, state_s5_im, cache_pool, cache_conv, w_in, b_in, s5_a_re, s5_a_im, s5_log_dt, s5_b_re, s5_b_im, s5_c_re, s5_c_im, s5_d, s5_w_glu, s5_b_glu, pool_w, pool_scale, conv_w, conv_b, conv_ln_g, conv_ln_b, proj_s5, proj_pool, proj_conv, w_out, b_out, ln1_g, ln1_b, ln2_g, ln2_b, ffn_w_gate, ffn_w_up, ffn_w_down, moe_router, moe_w_gate, moe_w_up, moe_w_down):
    depth, d_model, _ = w_in.shape
    n_bp, n_tp, _ = x_prompt.shape
    n_bs, n_ts, _ = x_sample.shape
    assert n_ts == 1
    n_grp, n_p = s5_a_re.shape[1:]
    n_h = s5_b_re.shape[-1]
    n_state = n_grp * n_p
    d_s5 = n_grp * n_h
    d_pool = pool_scale.shape[1]
    d_conv = conv_w.shape[2]
    n_taps = conv_w.shape[1]
    d_lin = d_s5 + d_pool
    d_u = d_lin + 2 * d_conv
    n_exp = moe_router.shape[-1]
    alpha = (2.0 * depth) ** 0.25
    ff_chunk = MXU_DIM
    assert cache_pool.shape[2] == POOL_HIST - 1 and n_taps - 1 <= CONV_HIST
    n_p_tok = n_bp * n_tp
    n_tok = n_p_tok + n_bs
    tc_p = min(64, n_tp)
    tm_p = tc_p * n_bp

    def row(v):
        return v.reshape(1, -1).astype(F32)

    def both(w):
        w = w.astype(F32)
        return w.astype(BF16), w

    xp = jnp.transpose(x_prompt, (1, 0, 2)).reshape(n_p_tok, d_model)
    xs = x_sample.reshape(n_bs, d_model)
    zeros_p = dict(h0=jnp.zeros((n_bp, 2 * n_state), F32),
                   pool=jnp.zeros((POOL_HIST * n_bp, d_pool), F32),
                   conv=jnp.zeros((CONV_HIST * n_bp, d_conv), F32))
    outs = {k: [] for k in ("re_p", "im_p", "pool_p", "conv_p", "re_s", "im_s", "pool_s", "conv_s")}

    for l in range(depth):
        lam, bb_re, bb_im = _s5_discretise(s5_a_re[l], s5_a_im[l], s5_log_dt[l], s5_b_re[l], s5_b_im[l])
        conv_w_pad = jnp.pad(conv_w[l].astype(F32), ((0, CONV_HIST - n_taps), (0, 0)))
        shared_mw = dict(lam=lam, d=row(s5_d[l]), bglu=row(s5_b_glu[l]), pscale=row(pool_scale[l]),
                         conv_w=conv_w_pad, conv_b=row(conv_b[l]), conv_ln_g=row(conv_ln_g[l]),
                         conv_ln_b=row(conv_ln_b[l]), n_taps=n_taps)
        mats = dict(wb=both(_s5_input_blocks(bb_re, bb_im, n_grp, n_p)),
                    wc=both(_s5_output_blocks(s5_c_re[l].astype(F32), s5_c_im[l].astype(F32), n_p)),
                    wglu=both(s5_w_glu[l]), wpool=both(_block_diag(pool_w[l].astype(F32))))
        w1 = both(w_in[l, :, :d_u])
        b1 = row(b_in[l, :d_u])
        shared_lw = dict(b_gates=row(b_in[l, d_u:]), b_out=row(b_out[l]), ln1_g=row(ln1_g[l]), ln1_b=row(ln1_b[l]))
        lmats = dict(w_gates=both(w_in[l, :, d_u:]), proj_s5=both(proj_s5[l]), proj_pool=both(proj_pool[l]),
                     proj_conv=both(proj_conv[l]), w_out=both(w_out[l]))
        is_moe = l % 2 == 1
        router = None
        if is_moe:
            assert n_exp <= ROUTER_LANES // 2
            router = jnp.pad(moe_router[l // 2].astype(F32), ((0, 0), (0, ROUTER_LANES - n_exp)))

        mw = dict(shared_mw, **{k: v[0] for k, v in mats.items()})
        lw = dict(shared_lw, **{k: v[0] for k, v in lmats.items()})
        yp, hT_p, pool_p, conv_p = _mixers(xp, w1[0], b1, zeros_p["h0"], zeros_p["pool"], zeros_p["conv"], mw,
                                           nb=n_bp, tc=tc_p, start_pos=0, precise=False)
        res_p = _merge(xp, yp, lw, alpha=alpha, tm=tm_p, precise=False, router=router, n_exp=n_exp,
                       shared_rows=n_tok)
        mw = dict(shared_mw, **{k: v[1] for k, v in mats.items()})
        lw = dict(shared_lw, **{k: v[1] for k, v in lmats.items()})
        h0_s = jnp.concatenate([state_s5_re[l].reshape(n_bs, n_state), state_s5_im[l].reshape(n_bs, n_state)],
                               axis=1).astype(F32)
        ys, hT_s, pool_s, conv_s = _mixers(xs, w1[1], b1, h0_s,
                                           _pad_time_major(cache_pool[l].astype(F32), POOL_HIST),
                                           _pad_time_major(cache_conv[l].astype(F32), CONV_HIST), mw,
                                           nb=n_bs, tc=1, start_pos=PAST_LEN, precise=True)
        res_s = _merge(xs, ys, lw, alpha=alpha, tm=n_bs, precise=True, router=router, n_exp=n_exp,
                       shared=res_p[0] if is_moe else None, shared_rows=n_tok, row0=n_p_tok)

        outs["re_p"].append(hT_p[:, :n_state].reshape(n_bp, n_grp, n_p))
        outs["im_p"].append(hT_p[:, n_state:].reshape(n_bp, n_grp, n_p))
        outs["pool_p"].append(_from_time_major(pool_p, n_bp))
        outs["conv_p"].append(_from_time_major(conv_p, n_bp))
        outs["re_s"].append(hT_s[:, :n_state].reshape(n_bs, n_grp, n_p))
        outs["im_s"].append(hT_s[:, n_state:].reshape(n_bs, n_grp, n_p))
        outs["pool_s"].append(_from_time_major(pool_s, n_bs))
        outs["conv_s"].append(_from_time_major(conv_s, n_bs))

        g2, b2 = row(ln2_g[l]), row(ln2_b[l])
        i = l // 2
        if not is_moe:
            wg, wu = jnp.swapaxes(ffn_w_gate[i], -1, -2).astype(F32), jnp.swapaxes(ffn_w_up[i], -1, -2).astype(F32)
            wd = ffn_w_down[i].astype(F32)
            xp = _ffn_dense(res_p[0], wg, wu, wd, g2, b2, alpha=alpha, tm=tm_p, ff_chunk=ff_chunk,
                            precise=False)
            xs = _ffn_dense(res_s[0], wg, wu, wd, g2, b2, alpha=alpha, tm=n_bs, ff_chunk=ff_chunk,
                            precise=True)
        else:
            wg, wu = jnp.swapaxes(moe_w_gate[i], -1, -2).astype(F32), jnp.swapaxes(moe_w_up[i], -1, -2).astype(F32)
            wd = moe_w_down[i].astype(F32)
            _, ri_p, rw_p = res_p
            x1_all, ri_s, rw_s = res_s
            top_i = jnp.concatenate([ri_p[:, :TOP_K], ri_s[:, :TOP_K]], axis=0)
            tm_moe = 512
            slot_tok, tile_expert, n_used, slots = _moe_route(top_i, n_exp, tm_moe)
            n_tiles = slot_tok.shape[0] // tm_moe
            bounds = sorted({0} | {max(1, round(n_tiles * f)) for f in MOE_PART_ENDS})
            y_sorted = None
            for t0, t1 in zip(bounds[:-1], bounds[1:]):
                x_part = x1_all.at[slot_tok[t0 * tm_moe:t1 * tm_moe]].get(mode="promise_in_bounds")
                y_sorted = _moe_grouped(tile_expert, n_used, x_part, wg, wu, wd, tm=tm_moe, ff_chunk=ff_chunk,
                                        tile0=t0, n_slots=slot_tok.shape[0], y_prev=y_sorted)
            def picked(r0, cnt):
                return [y_sorted.at[slots[r0:r0 + cnt, k]].get(mode="promise_in_bounds") for k in range(TOP_K)]

            last_layer = l == depth - 1
            p_tiles = n_p_tok // tm_p
            bounds = sorted({tm_p * ((p_tiles * k) // MOE_PARTS) for k in range(MOE_PARTS + 1)})
            xp = None
            for r0, r1 in zip(bounds[:-1], bounds[1:]):
                ya, yb = picked(r0, r1 - r0)
                if last_layer:
                    xp = _moe_combine(x1_all, ya, yb, rw_p, g2, b2, alpha=alpha, tm=tm_p, x_row0=0,
                                      rows=(r0, r1 - r0), nb=n_bp, n_seq_steps=n_tp, out_prev=xp)
                else:
                    part = _moe_combine(x1_all, ya, yb, rw_p, g2, b2, alpha=alpha, tm=tm_p, x_row0=0,
                                        rows=(r0, r1 - r0))
                    xp = part if xp is None else jnp.concatenate([xp, part], axis=0)
            ya, yb = picked(n_p_tok, n_bs)
            xs = _moe_combine(x1_all, ya, yb, rw_s, g2, b2, alpha=alpha, tm=n_bs, x_row0=n_p_tok, rows=(0, n_bs))

    y_prompt = xp if xp.ndim == 3 else _from_time_major(xp, n_bp)
    st = lambda k: jnp.stack(outs[k])
    return (y_prompt, xs.reshape(n_bs, 1, d_model), st("re_p"), st("im_p"), st("pool_p"), st("conv_p"),
            st("re_s"), st("im_s"), st("pool_s"), st("conv_s"))
```

```python
import functools

import jax
import jax.numpy as jnp
from jax import lax
from jax.experimental import pallas as pl
from jax.experimental.pallas import tpu as pltpu

F32 = jnp.float32
BF16 = jnp.bfloat16

LN_EPS = 1e-5
POOL_WINDOWS = (2, 4, 8, 16)
PAST_LEN = 16384
TOP_K = 2
POOL_HIST = 16
CONV_HIST = 32
ROUTER_LANES = 128
LANES = 128
MXU_DIM = 256
MERGE_SUBTILES = 2
MOE_PARTS = 4
MOE_PART_ENDS = (0.12, 0.4, 0.7, 1.0)
VMEM_LIMIT = 58 * 1024 * 1024


def _sigmoid(x):
    return 0.5 * jnp.tanh(0.5 * x) + 0.5


def _silu(x):
    return x * _sigmoid(x)


def _layer_norm(x, g, b):
    mu = jnp.mean(x, axis=-1, keepdims=True)
    xc = x - mu
    var = jnp.mean(xc * xc, axis=-1, keepdims=True)
    return xc * lax.rsqrt(var + LN_EPS) * g + b


def _dot(a, b, w_transposed=False):
    dims = (((1,), (1 if w_transposed else 0,)), ((), ()))
    return lax.dot_general(a, b, dims, preferred_element_type=F32)


def _split(v):
    hi = v.astype(BF16)
    return hi, (v - hi.astype(F32)).astype(BF16)


def _mm(a, w, precise, w_transposed=False):
    if not precise:
        return _dot(a.astype(BF16), w.astype(BF16), w_transposed)
    a_hi, a_lo = _split(a)
    w_hi, w_lo = _split(w)
    m = a.shape[0]
    r = _dot(jnp.concatenate([a_hi, a_lo], axis=0), w_hi, w_transposed)
    return r[:m] + r[m:] + _dot(a_hi, w_lo, w_transposed)


def _params(*sem):
    return pltpu.CompilerParams(dimension_semantics=sem, vmem_limit_bytes=VMEM_LIMIT)


def _const_spec(shape):
    return pl.BlockSpec(shape, lambda *_: (0,) * len(shape), pipeline_mode=pl.Buffered(1))


def _row_spec(tm, d, blk0=0):
    return pl.BlockSpec((tm, d), lambda i: (blk0 + i, 0))


def _s5_disc_kernel(are_ref, aim_ref, ldt_ref, bre_ref, bim_ref, lam_ref, bbre_ref, bbim_ref):
    a_re = are_ref[...]
    a_im = aim_ref[...]
    dt = jnp.exp(ldt_ref[...])
    mag = jnp.exp(a_re * dt)
    lr = mag * jnp.cos(a_im * dt)
    li = mag * jnp.sin(a_im * dt)
    den = a_re * a_re + a_im * a_im
    nr = lr - 1.0
    cr = (nr * a_re + li * a_im) / den
    ci = (li * a_re - nr * a_im) / den
    lam_ref[0:1, :] = lr
    lam_ref[1:2, :] = li
    b_re = bre_ref[...]
    b_im = bim_ref[...]
    bbre_ref[...] = cr * b_re - ci * b_im
    bbim_ref[...] = cr * b_im + ci * b_re


def _s5_discretise(a_re, a_im, log_dt, b_re, b_im):
    n_g, n_p = a_re.shape
    n_h = b_re.shape[-1]
    n_s = n_g * n_p
    ldt = jnp.broadcast_to(log_dt[:, None], (n_g, n_p)).reshape(1, n_s)
    b_re_t = b_re.reshape(n_s, n_h).T
    b_im_t = b_im.reshape(n_s, n_h).T
    return pl.pallas_call(
        _s5_disc_kernel,
        out_shape=(jax.ShapeDtypeStruct((2, n_s), F32),
                   jax.ShapeDtypeStruct((n_h, n_s), F32),
                   jax.ShapeDtypeStruct((n_h, n_s), F32)),
        name="s5_discretise",
    )(a_re.reshape(1, n_s), a_im.reshape(1, n_s), ldt, b_re_t, b_im_t)


def _block_diag(blocks):
    n_g, r, c = blocks.shape
    eye = jnp.eye(n_g, dtype=blocks.dtype)
    return (eye[:, None, :, None] * blocks[:, :, None, :]).reshape(n_g * r, n_g * c)


def _s5_input_blocks(bb_re, bb_im, n_grp, n_p):
    n_h = bb_re.shape[0]
    gpb = MXU_DIM // n_p
    ch = gpb * n_h
    assert MXU_DIM % n_p == 0 and LANES % ch == 0 and n_grp % gpb == 0
    n_blk = n_grp // gpb
    per_chunk = LANES // ch

    def blocks(bb):
        g = jnp.transpose(bb.reshape(n_h, n_grp, n_p), (1, 0, 2)).reshape(n_blk, gpb, n_h, n_p)
        dense = jax.vmap(_block_diag)(g)
        return jnp.stack([jnp.pad(dense[j], ((ch * (j % per_chunk), LANES - ch * (j % per_chunk + 1)), (0, 0)))
                          for j in range(n_blk)])

    return jnp.concatenate([blocks(bb_re), blocks(bb_im)], axis=2)


def _s5_output_blocks(c_re, c_im, n_p):
    n_grp, n_h, _ = c_re.shape
    gpb = MXU_DIM // n_h
    assert n_grp % gpb == 0
    n_blk = n_grp // gpb

    def blocks(cm):
        g = jnp.transpose(cm, (0, 2, 1)).reshape(n_blk, gpb, n_p, n_h)
        return jax.vmap(_block_diag)(g)

    return jnp.concatenate([blocks(c_re), blocks(-c_im)], axis=1)


def _mixer_kernel(x_ref, w1_ref, b1_ref, h0_ref, pc_ref, cc_ref, wb_ref, wc_ref, lam_ref, d_ref, wglu_ref, bglu_ref,
                  wpool_ref, pscale_ref, cw_ref, cb_ref, clg_ref, clb_ref,
                  y_ref, hT_ref, pnew_ref, cnew_ref,
                  u_ref, s_scr, h_scr, ph_scr, ch_scr, ca_scr, *, nb, tc, start_pos, d_s5, d_pool, d_conv, n_taps,
                  precise):
    c = pl.program_id(0)
    last = pl.num_programs(0) - 1
    rows = tc * nb
    n_state = lam_ref.shape[1]
    o_pool = d_s5
    o_conv = d_s5 + d_pool

    @pl.when(c == 0)
    def _():
        h_scr[...] = h0_ref[...]
        ph_scr[0:POOL_HIST * nb, :] = pc_ref[...]
        ch_scr[0:CONV_HIST * nb, :] = cc_ref[...]

    proj = _mm(x_ref[...], w1_ref[...], precise) + b1_ref[...]
    u_ref[:, 0:o_conv] = proj[:, 0:o_conv]
    u_ref[:, o_conv:] = proj[:, o_conv:o_conv + d_conv] * _sigmoid(proj[:, o_conv + d_conv:])

    u_s5 = u_ref[:, 0:d_s5]
    n_in_blk = wb_ref.shape[0]
    per_chunk = LANES // (d_s5 // n_in_blk)
    for j in range(n_in_blk):
        k0 = (j // per_chunk) * LANES
        bu = _mm(u_s5[:, k0:k0 + LANES], wb_ref[j], precise)
        s_scr[:, j * MXU_DIM:(j + 1) * MXU_DIM] = bu[:, :MXU_DIM]
        s_scr[:, n_state + j * MXU_DIM:n_state + (j + 1) * MXU_DIM] = bu[:, MXU_DIM:]
    if tc == 1:
        lr = lam_ref[0:1, :]
        li = lam_ref[1:2, :]
        h_re = h_scr[:, 0:n_state]
        h_im = h_scr[:, n_state:]
        n_re = lr * h_re - li * h_im + s_scr[:, 0:n_state]
        n_im = lr * h_im + li * h_re + s_scr[:, n_state:]
        s_scr[:, 0:n_state] = n_re
        s_scr[:, n_state:] = n_im
        h_scr[:, 0:n_state] = n_re
        h_scr[:, n_state:] = n_im
    else:
        assert nb == 8
        n_part = 2
        half = n_state // n_part
        for hf in range(n_part):
            re0 = hf * half
            im0 = n_state + hf * half
            lr = jnp.broadcast_to(lam_ref[0:1, re0:re0 + half], (nb, half))
            li = jnp.broadcast_to(lam_ref[1:2, re0:re0 + half], (nb, half))

            def step(t, carry, re0=re0, im0=im0, lr=lr, li=li):
                h_re, h_im = carry
                at_t = pl.ds(pl.multiple_of(t * nb, nb), nb)
                n_re = lr * h_re - li * h_im + s_scr[at_t, re0:re0 + half]
                n_im = lr * h_im + li * h_re + s_scr[at_t, im0:im0 + half]
                s_scr[at_t, re0:re0 + half] = n_re
                s_scr[at_t, im0:im0 + half] = n_im
                return n_re, n_im

            h_re, h_im = lax.fori_loop(0, tc, step, (h_scr[:, re0:re0 + half], h_scr[:, im0:im0 + half]),
                                       unroll=4)
            h_scr[:, re0:re0 + half] = h_re
            h_scr[:, im0:im0 + half] = h_im

    n_out_blk = wc_ref.shape[0]
    spb = n_state // n_out_blk
    ys = []
    for j in range(n_out_blk):
        h_re_b = s_scr[:, j * spb:(j + 1) * spb]
        h_im_b = s_scr[:, n_state + j * spb:n_state + (j + 1) * spb]
        ys.append(_mm(h_re_b, wc_ref[j, 0:spb, :], precise) + _mm(h_im_b, wc_ref[j, spb:, :], precise))
    y = jnp.concatenate(ys, axis=1) + d_ref[...] * u_s5
    y = jax.nn.gelu(y)
    y = y * _sigmoid(_mm(y, wglu_ref[...], precise) + bglu_ref[...])
    y_ref[:, 0:d_s5] = y.astype(y_ref.dtype)

    @pl.when(c == last)
    def _():
        hT_ref[...] = h_scr[...]

    u_pool = u_ref[:, o_pool:o_pool + d_pool]
    ph_scr[pl.ds(POOL_HIST * nb, rows), :] = u_pool
    a = ph_scr[...]
    row_t = lax.broadcasted_iota(jnp.int32, (rows, d_pool), 0) >> (nb.bit_length() - 1)
    pos1 = start_pos + c * tc + row_t + 1
    lane = lax.broadcasted_iota(jnp.int32, (rows, d_pool), 1)
    group = d_pool // len(POOL_WINDOWS)
    w = 1
    s_sel = None
    cnt_sel = None
    for gi, win in enumerate(POOL_WINDOWS):
        while w < win:
            a = a[w * nb:, :] + a[:a.shape[0] - w * nb, :]
            w *= 2
        assert w == win
        i0 = (POOL_HIST + 1 - win) * nb
        s_w = a[i0:i0 + rows, :]
        cnt_w = jnp.minimum(pos1, win).astype(F32)
        if s_sel is None:
            s_sel, cnt_sel = s_w, cnt_w
        else:
            in_later = lane >= gi * group
            s_sel = jnp.where(in_later, s_w, s_sel)
            cnt_sel = jnp.where(in_later, cnt_w, cnt_sel)
    pooled = s_sel / cnt_sel - u_pool
    y_pool = _mm(pooled, wpool_ref[...], precise) * pscale_ref[...]
    y_ref[:, o_pool:o_pool + d_pool] = y_pool.astype(y_ref.dtype)
    ph_scr[0:POOL_HIST * nb, :] = ph_scr[pl.ds(tc * nb, POOL_HIST * nb), :]

    @pl.when(c == last)
    def _():
        pnew_ref[...] = ph_scr[nb:POOL_HIST * nb, :]

    ch_scr[pl.ds(CONV_HIST * nb, rows), :] = u_ref[:, o_conv:o_conv + d_conv]
    rb = min(rows, 64)
    tap0 = CONV_HIST - (n_taps - 1)

    def conv_block(i, _):
        base = pl.multiple_of(i * rb, rb)
        acc = [None, None]
        for k in range(n_taps):
            term = cw_ref[k:k + 1, :] * ch_scr[pl.ds(base + (tap0 + k) * nb, rb), :]
            acc[k % 2] = term if acc[k % 2] is None else acc[k % 2] + term
        ca_scr[pl.ds(base, rb), :] = acc[0] + acc[1]
        return 0

    lax.fori_loop(0, rows // rb, conv_block, 0)
    yc = _silu(_layer_norm(ca_scr[...] + cb_ref[...], clg_ref[...], clb_ref[...]))
    y_ref[:, o_conv:o_conv + d_conv] = yc.astype(y_ref.dtype)
    ch_scr[0:CONV_HIST * nb, :] = ch_scr[pl.ds(tc * nb, CONV_HIST * nb), :]

    @pl.when(c == last)
    def _():
        cnew_ref[...] = ch_scr[tap0 * nb:CONV_HIST * nb, :]


def _mixers(x, w1, b1, h0, pool_cache, conv_cache, mw, *, nb, tc, start_pos, precise):
    n_rows, d = x.shape
    n_t = n_rows // nb
    rows = tc * nb
    d_s5 = mw["wglu"].shape[0]
    d_pool = mw["wpool"].shape[0]
    d_conv = mw["conv_w"].shape[1]
    n_taps = mw["n_taps"]
    d_y = d_s5 + d_pool + d_conv
    n_state2 = h0.shape[1]
    weights = [mw["wb"], mw["wc"], mw["lam"], mw["d"], mw["wglu"], mw["bglu"], mw["wpool"], mw["pscale"],
               mw["conv_w"], mw["conv_b"], mw["conv_ln_g"], mw["conv_ln_b"]]
    kern = functools.partial(_mixer_kernel, nb=nb, tc=tc, start_pos=start_pos, d_s5=d_s5, d_pool=d_pool,
                             d_conv=d_conv, n_taps=n_taps, precise=precise)
    return pl.pallas_call(
        kern,
        grid=(n_t // tc,),
        in_specs=[_row_spec(rows, d), _const_spec(w1.shape), _const_spec(b1.shape),
                  _const_spec(h0.shape), _const_spec(pool_cache.shape), _const_spec(conv_cache.shape)]
                 + [_const_spec(w.shape) for w in weights],
        out_specs=(_row_spec(rows, d_y),
                   pl.BlockSpec(h0.shape, lambda c: (0, 0)),
                   pl.BlockSpec(((POOL_HIST - 1) * nb, d_pool), lambda c: (0, 0)),
                   pl.BlockSpec(((n_taps - 1) * nb, d_conv), lambda c: (0, 0))),
        out_shape=(jax.ShapeDtypeStruct((n_rows, d_y), F32 if precise else BF16),
                   jax.ShapeDtypeStruct(h0.shape, F32),
                   jax.ShapeDtypeStruct(((POOL_HIST - 1) * nb, d_pool), F32),
                   jax.ShapeDtypeStruct(((n_taps - 1) * nb, d_conv), F32)),
        scratch_shapes=[pltpu.VMEM((rows, d_y), F32),
                        pltpu.VMEM((rows, n_state2), F32),
                        pltpu.VMEM((nb, n_state2), F32),
                        pltpu.VMEM(((POOL_HIST + tc) * nb, d_pool), F32),
                        pltpu.VMEM(((CONV_HIST + tc) * nb, d_conv), F32),
                        pltpu.VMEM((rows, d_conv), F32)],
        compiler_params=_params("arbitrary"),
        name="token_mixers",
    )(x, w1, b1, h0, pool_cache, conv_cache, *weights)


def _merge_kernel(*refs, alpha, d_s5, d_pool, d_conv, n_exp, aliased, precise):
    with_router = n_exp > 0
    n_in = 11 + (1 if with_router else 0) + (1 if aliased else 0)
    x_ref, y_ref, wg_ref, bg_ref, ps5_ref, ppool_ref, pconv_ref, wo_ref, bo_ref, g_ref, b_ref = refs[:11]
    o_ref = refs[n_in]
    d = x_ref.shape[-1]
    if with_router:
        rt_ref = refs[11]
        ri_ref, rw_ref = refs[n_in + 1:n_in + 3]
        rt = rt_ref[...]
        rt_hi = rt.astype(BF16).astype(F32)
        rt2 = (rt_hi + pltpu.roll(rt - rt_hi, ROUTER_LANES // 2, axis=1)).astype(BF16)

    n_sub = MERGE_SUBTILES if x_ref.shape[0] % (MERGE_SUBTILES * 16) == 0 else 1
    sub = x_ref.shape[0] // n_sub
    for s in range(n_sub):
        rs = slice(s * sub, (s + 1) * sub)
        x = x_ref[rs, :]
        xa = x if precise else x.astype(BF16)
        merged = None
        col = 0
        for j, (width, p_ref) in enumerate(((d_s5, ps5_ref), (d_pool, ppool_ref), (d_conv, pconv_ref))):
            gate = _sigmoid(_mm(xa, wg_ref[:, j * d:(j + 1) * d], precise) + bg_ref[:, j * d:(j + 1) * d])
            term = gate * _mm(y_ref[rs, col:col + width], p_ref[...], precise)
            merged = term if merged is None else merged + term
            col += width
        out = _mm(merged, wo_ref[...], precise) + bo_ref[...]
        x1 = _layer_norm(alpha * x + out, g_ref[...], b_ref[...])
        o_ref[rs, :] = x1
        if with_router:
            x_hi, x_lo = _split(x1)
            r = _dot(x_hi, rt2) + _dot(x_lo, rt2)
            logits = r + pltpu.roll(r, ROUTER_LANES // 2, axis=1)
            lane = lax.broadcasted_iota(jnp.int32, logits.shape, 1).astype(F32)
            neg = jnp.float32(-jnp.inf)
            l1 = jnp.where(lane < n_exp, logits, neg)
            m1 = jnp.max(l1, axis=-1, keepdims=True)
            i1 = jnp.min(jnp.where(l1 == m1, lane, float(ROUTER_LANES)), axis=-1, keepdims=True)
            l2 = jnp.where(lane == i1, neg, l1)
            m2 = jnp.max(l2, axis=-1, keepdims=True)
            i2 = jnp.min(jnp.where(l2 == m2, lane, float(ROUTER_LANES)), axis=-1, keepdims=True)
            e2 = jnp.exp(m2 - m1)
            w1 = 1.0 / (1.0 + e2)
            w2 = e2 * w1
            ri_ref[rs, :] = jnp.where(lane == 0.0, i1, jnp.where(lane == 1.0, i2, 0.0)).astype(jnp.int32)
            rw_ref[rs, :] = jnp.where(lane == 0.0, w1, jnp.where(lane == 1.0, w2, 0.0))


def _merge(x, y, lw, *, alpha, tm, precise, router=None, n_exp=0, shared=None, shared_rows=0, row0=0):
    n, d = x.shape
    d_y = y.shape[-1]
    d_s5, d_pool, d_conv = lw["proj_s5"].shape[0], lw["proj_pool"].shape[0], lw["proj_conv"].shape[0]
    inputs = [x, y, lw["w_gates"], lw["b_gates"], lw["proj_s5"], lw["proj_pool"], lw["proj_conv"], lw["w_out"],
              lw["b_out"], lw["ln1_g"], lw["ln1_b"]]
    in_specs = [_row_spec(tm, d), _row_spec(tm, d_y)] + [_const_spec(w.shape) for w in inputs[2:]]
    aliases = {}
    if router is None:
        out_specs = [_row_spec(tm, d)]
        out_shape = [jax.ShapeDtypeStruct((n, d), F32)]
    else:
        inputs.append(router)
        in_specs.append(_const_spec(router.shape))
        if shared is not None:
            aliases = {len(inputs): 0}
            inputs.append(shared)
            in_specs.append(pl.BlockSpec(memory_space=pl.ANY))
        out_specs = [_row_spec(tm, d, row0 // tm), _row_spec(tm, ROUTER_LANES), _row_spec(tm, ROUTER_LANES)]
        out_shape = [jax.ShapeDtypeStruct((shared_rows, d), F32),
                     jax.ShapeDtypeStruct((n, ROUTER_LANES), jnp.int32),
                     jax.ShapeDtypeStruct((n, ROUTER_LANES), F32)]
    kern = functools.partial(_merge_kernel, alpha=alpha, d_s5=d_s5, d_pool=d_pool, d_conv=d_conv,
                             n_exp=n_exp if router is not None else 0, aliased=shared is not None, precise=precise)
    return pl.pallas_call(
        kern,
        grid=(n // tm,),
        in_specs=in_specs,
        out_specs=tuple(out_specs),
        out_shape=tuple(out_shape),
        input_output_aliases=aliases,
        compiler_params=_params("parallel"),
        name="merge_outproj_ln1",
    )(*inputs)


def _swiglu_tile(x, wg_ref, wu_ref, wd_ref, ff_chunk, precise):
    d_ff = wg_ref.shape[0]
    acc = None
    for c0 in range(0, d_ff, ff_chunk):
        c1 = min(c0 + ff_chunk, d_ff)
        g = _mm(x, wg_ref[c0:c1, :], precise, w_transposed=True)
        u = _mm(x, wu_ref[c0:c1, :], precise, w_transposed=True)
        part = _mm(_silu(g) * u, wd_ref[c0:c1, :], precise)
        acc = part if acc is None else acc + part
    return acc


def _ffn_kernel(x_ref, wg_ref, wu_ref, wd_ref, g_ref, b_ref, o_ref, *, alpha, ff_chunk, precise):
    x = x_ref[...]
    f = _swiglu_tile(x if precise else x.astype(BF16), wg_ref, wu_ref, wd_ref, ff_chunk, precise)
    o_ref[...] = _layer_norm(alpha * x + f, g_ref[...], b_ref[...])


def _ffn_dense(x, wg, wu, wd, ln_g, ln_b, *, alpha, tm, ff_chunk, precise):
    n, d = x.shape
    weights = [wg, wu, wd, ln_g, ln_b]
    return pl.pallas_call(
        functools.partial(_ffn_kernel, alpha=alpha, ff_chunk=ff_chunk, precise=precise),
        grid=(n // tm,),
        in_specs=[_row_spec(tm, d)] + [_const_spec(w.shape) for w in weights],
        out_specs=_row_spec(tm, d),
        out_shape=jax.ShapeDtypeStruct((n, d), F32),
        compiler_params=_params("parallel"),
        name="ffn_dense_ln2",
    )(x, *weights)


def _moe_kernel(te_ref, nt_ref, xs_ref, wg_ref, wu_ref, wd_ref, *rest, ff_chunk, tile0):
    o_ref = rest[-1]
    i = tile0 + pl.program_id(0)

    @pl.when(i < nt_ref[0])
    def _():
        o_ref[...] = _swiglu_tile(xs_ref[...].astype(BF16), wg_ref, wu_ref, wd_ref, ff_chunk, False)

    @pl.when(i >= nt_ref[0])
    def _():
        o_ref[...] = jnp.zeros(o_ref.shape, o_ref.dtype)


def _moe_grouped(tile_expert, n_tiles_used, xs, wg, wu, wd, *, tm, ff_chunk, tile0, n_slots, y_prev=None):
    n_part, d = xs.shape
    d_ff = wd.shape[1]

    def expert_spec(shape, buffers):
        return pl.BlockSpec((None,) + shape, lambda i, te, nt: (te[tile0 + i], 0, 0),
                            pipeline_mode=pl.Buffered(buffers))

    inputs = [tile_expert, n_tiles_used, xs, wg, wu, wd]
    in_specs = [pl.BlockSpec((tm, d), lambda i, te, nt: (i, 0)),
                expert_spec((d_ff, d), 1), expert_spec((d_ff, d), 1), expert_spec((d_ff, d), 2)]
    aliases = {}
    if y_prev is not None:
        aliases = {len(inputs): 0}
        inputs.append(y_prev)
        in_specs.append(pl.BlockSpec(memory_space=pl.ANY))
    grid_spec = pltpu.PrefetchScalarGridSpec(
        num_scalar_prefetch=2,
        grid=(n_part // tm,),
        in_specs=in_specs,
        out_specs=pl.BlockSpec((tm, d), lambda i, te, nt: (tile0 + i, 0)),
    )
    return pl.pallas_call(
        functools.partial(_moe_kernel, ff_chunk=ff_chunk, tile0=tile0),
        grid_spec=grid_spec,
        out_shape=jax.ShapeDtypeStruct((n_slots, d), F32),
        input_output_aliases=aliases,
        compiler_params=_params("arbitrary"),
        name="moe_grouped_swiglu",
    )(*inputs)


def _combine_kernel(x_ref, ya_ref, yb_ref, rw_ref, g_ref, b_ref, *rest, alpha, nb):
    f = rw_ref[:, 0:1] * ya_ref[...] + rw_ref[:, 1:2] * yb_ref[...]
    y = _layer_norm(alpha * x_ref[...] + f, g_ref[...], b_ref[...])
    if nb is None:
        rest[-1][...] = y
        return
    o_ref, t_scr = rest[-2], rest[-1]
    steps = y.shape[0] // nb
    for k in range(t_scr.shape[0]):
        t_scr[k] = y[:, k * LANES:(k + 1) * LANES]
    for b in range(nb):
        for k in range(t_scr.shape[0]):
            o_ref[b, :, k * LANES:(k + 1) * LANES] = t_scr[k, pl.ds(b, steps, stride=nb), :]


def _moe_combine(x1, ya, yb, rw, ln_g, ln_b, *, alpha, tm, x_row0, rows, nb=None, n_seq_steps=None, out_prev=None):
    r0, cnt = rows
    d = x1.shape[1]
    inputs = [x1, ya, yb, rw, ln_g, ln_b]
    in_specs = [_row_spec(tm, d, (x_row0 + r0) // tm), _row_spec(tm, d), _row_spec(tm, d),
                _row_spec(tm, ROUTER_LANES, r0 // tm), _const_spec(ln_g.shape), _const_spec(ln_b.shape)]
    aliases = {}
    if out_prev is not None:
        aliases = {len(inputs): 0}
        inputs.append(out_prev)
        in_specs.append(pl.BlockSpec(memory_space=pl.ANY))
    if nb is None:
        out_specs = _row_spec(tm, d)
        out_shape = jax.ShapeDtypeStruct((cnt, d), F32)
        scratch = []
    else:
        steps = tm // nb
        out_specs = pl.BlockSpec((nb, steps, d), lambda i: (0, r0 // tm + i, 0))
        out_shape = jax.ShapeDtypeStruct((nb, n_seq_steps, d), F32)
        scratch = [pltpu.VMEM((d // LANES, tm, LANES), F32)]
    return pl.pallas_call(
        functools.partial(_combine_kernel, alpha=alpha, nb=nb),
        grid=(cnt // tm,),
        in_specs=in_specs,
        out_specs=out_specs,
        out_shape=out_shape,
        scratch_shapes=scratch,
        input_output_aliases=aliases,
        compiler_params=_params("parallel"),
        name="moe_combine_ln2",
    )(*inputs)


def _moe_route(top_i, n_experts, tm):
    n = top_i.shape[0]
    n_flat = TOP_K * n
    flat_e = top_i.reshape(-1)
    iota = jnp.arange(n_flat, dtype=jnp.int32)
    sorted_e, sorted_j = lax.sort((flat_e, iota), num_keys=1, is_stable=True)
    experts = jnp.arange(n_experts, dtype=jnp.int32)
    counts = jnp.sum((flat_e[:, None] == experts[None, :]).astype(jnp.int32), axis=0)
    start_end = jnp.cumsum(counts)
    padded = ((counts + tm - 1) // tm) * tm
    off_end = jnp.cumsum(padded)
    off = off_end - padded
    shift = off - (start_end - counts)
    n_slots = ((n_flat + n_experts * tm + tm - 1) // tm) * tm
    n_tiles = n_slots // tm
    tile_start = jnp.arange(n_tiles, dtype=jnp.int32) * tm
    tile_expert = jnp.sum((tile_start[:, None] >= off_end[None, :]).astype(jnp.int32), axis=1)
    tile_expert = jnp.minimum(tile_expert, n_experts - 1).astype(jnp.int32)
    n_used = (off_end[-1] // tm).astype(jnp.int32).reshape(1)
    slot = jnp.arange(n_slots, dtype=jnp.int32).reshape(n_tiles, tm)
    t_off, t_cnt, t_shift = off[tile_expert][:, None], counts[tile_expert][:, None], shift[tile_expert][:, None]
    valid = (slot - t_off) < t_cnt
    src = jnp.clip(slot - t_shift, 0, n_flat - 1)
    slot_tok = jnp.where(valid, sorted_j[src] // TOP_K, 0).reshape(n_slots)
    slot_of_sorted = iota + shift[sorted_e]
    _, slots = lax.sort((sorted_j, slot_of_sorted), num_keys=1)
    return slot_tok, tile_expert, n_used, slots.reshape(n, TOP_K)


def _pad_time_major(cache, steps):
    n_b, n_s, n_c = cache.shape
    tmaj = jnp.transpose(cache, (1, 0, 2))
    tmaj = jnp.pad(tmaj, ((steps - n_s, 0), (0, 0), (0, 0)))
    return tmaj.reshape(steps * n_b, n_c)


def _from_time_major(flat, n_b):
    n_c = flat.shape[1]
    return jnp.transpose(flat.reshape(-1, n_b, n_c), (1, 0, 2))


def kernel(x_prompt, x_sample, state_s5_re, state_s5_im, cache_pool, cache_conv, w_in, b_in, s5_a_re, s5_a_im, s5_log_dt, s5_b_re, s5_b_im, s5_c_re, s5_c_im, s5_d, s5_w_glu, s5_b_glu, pool_w, pool_scale, conv_w, conv_b, conv_ln_g, conv_ln_b, proj_s5, proj_pool, proj_conv, w_out, b_out, ln1_g, ln1_b, ln2_g, ln2_b, ffn_w_gate, ffn_w_up, ffn_w_down, moe_router, moe_w_gate, moe_w_up, moe_w_down):
    depth, d_model, _ = w_in.shape
    n_bp, n_tp, _ = x_prompt.shape
    n_bs, n_ts, _ = x_sample.shape
    assert n_ts == 1
    n_grp, n_p = s5_a_re.shape[1:]
    n_h = s5_b_re.shape[-1]
    n_state = n_grp * n_p
    d_s5 = n_grp * n_h
    d_pool = pool_scale.shape[1]
    d_conv = conv_w.shape[2]
    n_taps = conv_w.shape[1]
    d_lin = d_s5 + d_pool
    d_u = d_lin + 2 * d_conv
    n_exp = moe_router.shape[-1]
    alpha = (2.0 * depth) ** 0.25
    ff_chunk = MXU_DIM
    assert cache_pool.shape[2] == POOL_HIST - 1 and n_taps - 1 <= CONV_HIST
    n_p_tok = n_bp * n_tp
    n_tok = n_p_tok + n_bs
    tc_p = min(64, n_tp)
    tm_p = tc_p * n_bp

    def row(v):
        return v.reshape(1, -1).astype(F32)

    def both(w):
        w = w.astype(F32)
        return w.astype(BF16), w

    xp = jnp.transpose(x_prompt, (1, 0, 2)).reshape(n_p_tok, d_model)
    xs = x_sample.reshape(n_bs, d_model)
    zeros_p = dict(h0=jnp.zeros((n_bp, 2 * n_state), F32),
                   pool=jnp.zeros((POOL_HIST * n_bp, d_pool), F32),
                   conv=jnp.zeros((CONV_HIST * n_bp, d_conv), F32))
    outs = {k: [] for k in ("re_p", "im_p", "pool_p", "conv_p", "re_s", "im_s", "pool_s", "conv_s")}

    for l in range(depth):
        lam, bb_re, bb_im = _s5_discretise(s5_a_re[l], s5_a_im[l], s5_log_dt[l], s5_b_re[l], s5_b_im[l])
        conv_w_pad = jnp.pad(conv_w[l].astype(F32), ((0, CONV_HIST - n_taps), (0, 0)))
        shared_mw = dict(lam=lam, d=row(s5_d[l]), bglu=row(s5_b_glu[l]), pscale=row(pool_scale[l]),
                         conv_w=conv_w_pad, conv_b=row(conv_b[l]), conv_ln_g=row(conv_ln_g[l]),
                         conv_ln_b=row(conv_ln_b[l]), n_taps=n_taps)
        mats = dict(wb=both(_s5_input_blocks(bb_re, bb_im, n_grp, n_p)),
                    wc=both(_s5_output_blocks(s5_c_re[l].astype(F32), s5_c_im[l].astype(F32), n_p)),
                    wglu=both(s5_w_glu[l]), wpool=both(_block_diag(pool_w[l].astype(F32))))
        w1 = both(w_in[l, :, :d_u])
        b1 = row(b_in[l, :d_u])
        shared_lw = dict(b_gates=row(b_in[l, d_u:]), b_out=row(b_out[l]), ln1_g=row(ln1_g[l]), ln1_b=row(ln1_b[l]))
        lmats = dict(w_gates=both(w_in[l, :, d_u:]), proj_s5=both(proj_s5[l]), proj_pool=both(proj_pool[l]),
                     proj_conv=both(proj_conv[l]), w_out=both(w_out[l]))
        is_moe = l % 2 == 1
        router = None
        if is_moe:
            assert n_exp <= ROUTER_LANES // 2
            router = jnp.pad(moe_router[l // 2].astype(F32), ((0, 0), (0, ROUTER_LANES - n_exp)))

        mw = dict(shared_mw, **{k: v[0] for k, v in mats.items()})
        lw = dict(shared_lw, **{k: v[0] for k, v in lmats.items()})
        yp, hT_p, pool_p, conv_p = _mixers(xp, w1[0], b1, zeros_p["h0"], zeros_p["pool"], zeros_p["conv"], mw,
                                           nb=n_bp, tc=tc_p, start_pos=0, precise=False)
        res_p = _merge(xp, yp, lw, alpha=alpha, tm=tm_p, precise=False, router=router, n_exp=n_exp,
                       shared_rows=n_tok)
        mw = dict(shared_mw, **{k: v[1] for k, v in mats.items()})
        lw = dict(shared_lw, **{k: v[1] for k, v in lmats.items()})
        h0_s = jnp.concatenate([state_s5_re[l].reshape(n_bs, n_state), state_s5_im[l].reshape(n_bs, n_state)],
                               axis=1).astype(F32)
        ys, hT_s, pool_s, conv_s = _mixers(xs, w1[1], b1, h0_s,
                                           _pad_time_major(cache_pool[l].astype(F32), POOL_HIST),
                                           _pad_time_major(cache_conv[l].astype(F32), CONV_HIST), mw,
                                           nb=n_bs, tc=1, start_pos=PAST_LEN, precise=True)
        res_s = _merge(xs, ys, lw, alpha=alpha, tm=n_bs, precise=True, router=router, n_exp=n_exp,
                       shared=res_p[0] if is_moe else None, shared_rows=n_tok, row0=n_p_tok)

        outs["re_p"].append(hT_p[:, :n_state].reshape(n_bp, n_grp, n_p))
        outs["im_p"].append(hT_p[:, n_state:].reshape(n_bp, n_grp, n_p))
        outs["pool_p"].append(_from_time_major(pool_p, n_bp))
        outs["conv_p"].append(_from_time_major(conv_p, n_bp))
        outs["re_s"].append(hT_s[:, :n_state].reshape(n_bs, n_grp, n_p))
        outs["im_s"].append(hT_s[:, n_state:].reshape(n_bs, n_grp, n_p))
        outs["pool_s"].append(_from_time_major(pool_s, n_bs))
        outs["conv_s"].append(_from_time_major(conv_s, n_bs))

        g2, b2 = row(ln2_g[l]), row(ln2_b[l])
        i = l // 2
        if not is_moe:
            wg, wu = jnp.swapaxes(ffn_w_gate[i], -1, -2).astype(F32), jnp.swapaxes(ffn_w_up[i], -1, -2).astype(F32)
            wd = ffn_w_down[i].astype(F32)
            xp = _ffn_dense(res_p[0], wg, wu, wd, g2, b2, alpha=alpha, tm=tm_p, ff_chunk=ff_chunk,
                            precise=False)
            xs = _ffn_dense(res_s[0], wg, wu, wd, g2, b2, alpha=alpha, tm=n_bs, ff_chunk=ff_chunk,
                            precise=True)
        else:
            wg, wu = jnp.swapaxes(moe_w_gate[i], -1, -2).astype(F32), jnp.swapaxes(moe_w_up[i], -1, -2).astype(F32)
            wd = moe_w_down[i].astype(F32)
            _, ri_p, rw_p = res_p
            x1_all, ri_s, rw_s = res_s
            top_i = jnp.concatenate([ri_p[:, :TOP_K], ri_s[:, :TOP_K]], axis=0)
            tm_moe = 512
            slot_tok, tile_expert, n_used, slots = _moe_route(top_i, n_exp, tm_moe)
            n_tiles = slot_tok.shape[0] // tm_moe
            bounds = sorted({0} | {max(1, round(n_tiles * f)) for f in MOE_PART_ENDS})
            y_sorted = None
            for t0, t1 in zip(bounds[:-1], bounds[1:]):
                x_part = x1_all.at[slot_tok[t0 * tm_moe:t1 * tm_moe]].get(mode="promise_in_bounds")
                y_sorted = _moe_grouped(tile_expert, n_used, x_part, wg, wu, wd, tm=tm_moe, ff_chunk=ff_chunk,
                                        tile0=t0, n_slots=slot_tok.shape[0], y_prev=y_sorted)
            def picked(r0, cnt):
                return [y_sorted.at[slots[r0:r0 + cnt, k]].get(mode="promise_in_bounds") for k in range(TOP_K)]

            last_layer = l == depth - 1
            p_tiles = n_p_tok // tm_p
            bounds = sorted({tm_p * ((p_tiles * k) // MOE_PARTS) for k in range(MOE_PARTS + 1)})
            xp = None
            for r0, r1 in zip(bounds[:-1], bounds[1:]):
                ya, yb = picked(r0, r1 - r0)
                if last_layer:
                    xp = _moe_combine(x1_all, ya, yb, rw_p, g2, b2, alpha=alpha, tm=tm_p, x_row0=0,
                                      rows=(r0, r1 - r0), nb=n_bp, n_seq_steps=n_tp, out_prev=xp)
                else:
                    part = _moe_combine(x1_all, ya, yb, rw_p, g2, b2, alpha=alpha, tm=tm_p, x_row0=0,
                                        rows=(r0, r1 - r0))
                    xp = part if xp is None else jnp.concatenate([xp, part], axis=0)
            ya, yb = picked(n_p_tok, n_bs)
            xs = _moe_combine(x1_all, ya, yb, rw_s, g2, b2, alpha=alpha, tm=n_bs, x_row0=n_p_tok, rows=(0, n_bs))

    y_prompt = xp if xp.ndim == 3 else _from_time_major(xp, n_bp)
    st = lambda k: jnp.stack(outs[k])
    return (y_prompt, xs.reshape(n_bs, 1, d_model), st("re_p"), st("im_p"), st("pool_p"), st("conv_p"),
            st("re_s"), st("im_s"), st("pool_s"), st("conv_s"))
```

```python
import functools

import jax
import jax.numpy as jnp
from jax import lax
from jax.experimental import pallas as pl
from jax.experimental.pallas import tpu as pltpu

F32 = jnp.float32
BF16 = jnp.bfloat16

LN_EPS = 1e-5
POOL_WINDOWS = (2, 4, 8, 16)
PAST_LEN = 16384
TOP_K = 2
POOL_HIST = 16
CONV_HIST = 32
ROUTER_LANES = 128
LANES = 128
MXU_DIM = 256
MERGE_SUBTILES = 2
MOE_PARTS = 4
MOE_PART_ENDS = (0.12, 0.4, 0.7, 1.0)
VMEM_LIMIT = 58 * 1024 * 1024


def _sigmoid(x):
    return 0.5 * jnp.tanh(0.5 * x) + 0.5


def _silu(x):
    return x * _sigmoid(x)


def _layer_norm(x, g, b):
    mu = jnp.mean(x, axis=-1, keepdims=True)
    xc = x - mu
    var = jnp.mean(xc * xc, axis=-1, keepdims=True)
    return xc * lax.rsqrt(var + LN_EPS) * g + b


def _dot(a, b, w_transposed=False):
    dims = (((1,), (1 if w_transposed else 0,)), ((), ()))
    return lax.dot_general(a, b, dims, preferred_element_type=F32)


def _split(v):
    hi = v.astype(BF16)
    return hi, (v - hi.astype(F32)).astype(BF16)


def _mm(a, w, precise, w_transposed=False):
    if not precise:
        return _dot(a.astype(BF16), w.astype(BF16), w_transposed)
    a_hi, a_lo = _split(a)
    w_hi, w_lo = _split(w)
    m = a.shape[0]
    r = _dot(jnp.concatenate([a_hi, a_lo], axis=0), w_hi, w_transposed)
    return r[:m] + r[m:] + _dot(a_hi, w_lo, w_transposed)


def _params(*sem):
    return pltpu.CompilerParams(dimension_semantics=sem, vmem_limit_bytes=VMEM_LIMIT)


def _const_spec(shape):
    return pl.BlockSpec(shape, lambda *_: (0,) * len(shape), pipeline_mode=pl.Buffered(1))


def _row_spec(tm, d, blk0=0):
    return pl.BlockSpec((tm, d), lambda i: (blk0 + i, 0))


def _s5_disc_kernel(are_ref, aim_ref, ldt_ref, bre_ref, bim_ref, lam_ref, bbre_ref, bbim_ref):
    a_re = are_ref[...]
    a_im = aim_ref[...]
    dt = jnp.exp(ldt_ref[...])
    mag = jnp.exp(a_re * dt)
    lr = mag * jnp.cos(a_im * dt)
    li = mag * jnp.sin(a_im * dt)
    den = a_re * a_re + a_im * a_im
    nr = lr - 1.0
    cr = (nr * a_re + li * a_im) / den
    ci = (li * a_re - nr * a_im) / den
    lam_ref[0:1, :] = lr
    lam_ref[1:2, :] = li
    b_re = bre_ref[...]
    b_im = bim_ref[...]
    bbre_ref[...] = cr * b_re - ci * b_im
    bbim_ref[...] = cr * b_im + ci * b_re


def _s5_discretise(a_re, a_im, log_dt, b_re, b_im):
    n_g, n_p = a_re.shape
    n_h = b_re.shape[-1]
    n_s = n_g * n_p
    ldt = jnp.broadcast_to(log_dt[:, None], (n_g, n_p)).reshape(1, n_s)
    b_re_t = b_re.reshape(n_s, n_h).T
    b_im_t = b_im.reshape(n_s, n_h).T
    return pl.pallas_call(
        _s5_disc_kernel,
        out_shape=(jax.ShapeDtypeStruct((2, n_s), F32),
                   jax.ShapeDtypeStruct((n_h, n_s), F32),
                   jax.ShapeDtypeStruct((n_h, n_s), F32)),
        name="s5_discretise",
    )(a_re.reshape(1, n_s), a_im.reshape(1, n_s), ldt, b_re_t, b_im_t)


def _block_diag(blocks):
    n_g, r, c = blocks.shape
    eye = jnp.eye(n_g, dtype=blocks.dtype)
    return (eye[:, None, :, None] * blocks[:, :, None, :]).reshape(n_g * r, n_g * c)


def _s5_input_blocks(bb_re, bb_im, n_grp, n_p):
    n_h = bb_re.shape[0]
    gpb = MXU_DIM // n_p
    ch = gpb * n_h
    assert MXU_DIM % n_p == 0 and LANES % ch == 0 and n_grp % gpb == 0
    n_blk = n_grp // gpb
    per_chunk = LANES // ch

    def blocks(bb):
        g = jnp.transpose(bb.reshape(n_h, n_grp, n_p), (1, 0, 2)).reshape(n_blk, gpb, n_h, n_p)
        dense = jax.vmap(_block_diag)(g)
        return jnp.stack([jnp.pad(dense[j], ((ch * (j % per_chunk), LANES - ch * (j % per_chunk + 1)), (0, 0)))
                          for j in range(n_blk)])

    return jnp.concatenate([blocks(bb_re), blocks(bb_im)], axis=2)


def _s5_output_blocks(c_re, c_im, n_p):
    n_grp, n_h, _ = c_re.shape
    gpb = MXU_DIM // n_h
    assert n_grp % gpb == 0
    n_blk = n_grp // gpb

    def blocks(cm):
        g = jnp.transpose(cm, (0, 2, 1)).reshape(n_blk, gpb, n_p, n_h)
        return jax.vmap(_block_diag)(g)

    return jnp.concatenate([blocks(c_re), blocks(-c_im)], axis=1)


def _mixer_kernel(x_ref, w1_ref, b1_ref, h0_ref, pc_ref, cc_ref, wb_ref, wc_ref, lam_ref, d_ref, wglu_ref, bglu_ref,
                  wpool_ref, pscale_ref, cw_ref, cb_ref, clg_ref, clb_ref,
                  y_ref, hT_ref, pnew_ref, cnew_ref,
                  u_ref, s_scr, h_scr, ph_scr, ch_scr, ca_scr, *, nb, tc, start_pos, d_s5, d_pool, d_conv, n_taps,
                  precise):
    c = pl.program_id(0)
    last = pl.num_programs(0) - 1
    rows = tc * nb
    n_state = lam_ref.shape[1]
    o_pool = d_s5
    o_conv = d_s5 + d_pool

    @pl.when(c == 0)
    def _():
        h_scr[...] = h0_ref[...]
        ph_scr[0:POOL_HIST * nb, :] = pc_ref[...]
        ch_scr[0:CONV_HIST * nb, :] = cc_ref[...]

    proj = _mm(x_ref[...], w1_ref[...], precise) + b1_ref[...]
    u_ref[:, 0:o_conv] = proj[:, 0:o_conv]
    u_ref[:, o_conv:] = proj[:, o_conv:o_conv + d_conv] * _sigmoid(proj[:, o_conv + d_conv:])

    ch_scr[pl.ds(CONV_HIST * nb, rows), :] = u_ref[:, o_conv:o_conv + d_conv]
    rb = min(rows, 64)
    tap0 = CONV_HIST - (n_taps - 1)
    for base in range(0, rows, rb):
        acc = [None, None]
        for k in range(n_taps):
            term = cw_ref[k:k + 1, :] * ch_scr[base + (tap0 + k) * nb:base + (tap0 + k) * nb + rb, :]
            acc[k % 2] = term if acc[k % 2] is None else acc[k % 2] + term
        ca_scr[base:base + rb, :] = acc[0] + acc[1]

    u_s5 = u_ref[:, 0:d_s5]
    n_in_blk = wb_ref.shape[0]
    per_chunk = LANES // (d_s5 // n_in_blk)
    for j in range(n_in_blk):
        k0 = (j // per_chunk) * LANES
        bu = _mm(u_s5[:, k0:k0 + LANES], wb_ref[j], precise)
        s_scr[:, j * MXU_DIM:(j + 1) * MXU_DIM] = bu[:, :MXU_DIM]
        s_scr[:, n_state + j * MXU_DIM:n_state + (j + 1) * MXU_DIM] = bu[:, MXU_DIM:]
    if tc == 1:
        lr = lam_ref[0:1, :]
        li = lam_ref[1:2, :]
        h_re = h_scr[:, 0:n_state]
        h_im = h_scr[:, n_state:]
        n_re = lr * h_re - li * h_im + s_scr[:, 0:n_state]
        n_im = lr * h_im + li * h_re + s_scr[:, n_state:]
        s_scr[:, 0:n_state] = n_re
        s_scr[:, n_state:] = n_im
        h_scr[:, 0:n_state] = n_re
        h_scr[:, n_state:] = n_im
    else:
        assert nb == 8
        n_part = 2
        half = n_state // n_part
        for hf in range(n_part):
            re0 = hf * half
            im0 = n_state + hf * half
            lr = jnp.broadcast_to(lam_ref[0:1, re0:re0 + half], (nb, half))
            li = jnp.broadcast_to(lam_ref[1:2, re0:re0 + half], (nb, half))

            def step(t, carry, re0=re0, im0=im0, lr=lr, li=li):
                h_re, h_im = carry
                at_t = pl.ds(pl.multiple_of(t * nb, nb), nb)
                n_re = lr * h_re - li * h_im + s_scr[at_t, re0:re0 + half]
                n_im = lr * h_im + li * h_re + s_scr[at_t, im0:im0 + half]
                s_scr[at_t, re0:re0 + half] = n_re
                s_scr[at_t, im0:im0 + half] = n_im
                return n_re, n_im

            h_re, h_im = lax.fori_loop(0, tc, step, (h_scr[:, re0:re0 + half], h_scr[:, im0:im0 + half]),
                                       unroll=4)
            h_scr[:, re0:re0 + half] = h_re
            h_scr[:, im0:im0 + half] = h_im

    n_out_blk = wc_ref.shape[0]
    spb = n_state // n_out_blk
    ys = []
    for j in range(n_out_blk):
        h_re_b = s_scr[:, j * spb:(j + 1) * spb]
        h_im_b = s_scr[:, n_state + j * spb:n_state + (j + 1) * spb]
        ys.append(_mm(h_re_b, wc_ref[j, 0:spb, :], precise) + _mm(h_im_b, wc_ref[j, spb:, :], precise))
    y = jnp.concatenate(ys, axis=1) + d_ref[...] * u_s5
    y = jax.nn.gelu(y)
    y = y * _sigmoid(_mm(y, wglu_ref[...], precise) + bglu_ref[...])
    y_ref[:, 0:d_s5] = y.astype(y_ref.dtype)

    @pl.when(c == last)
    def _():
        hT_ref[...] = h_scr[...]

    u_pool = u_ref[:, o_pool:o_pool + d_pool]
    ph_scr[pl.ds(POOL_HIST * nb, rows), :] = u_pool
    a = ph_scr[...]
    row_t = lax.broadcasted_iota(jnp.int32, (rows, d_pool), 0) >> (nb.bit_length() - 1)
    pos1 = start_pos + c * tc + row_t + 1
    lane = lax.broadcasted_iota(jnp.int32, (rows, d_pool), 1)
    group = d_pool // len(POOL_WINDOWS)
    w = 1
    s_sel = None
    cnt_sel = None
    for gi, win in enumerate(POOL_WINDOWS):
        while w < win:
            a = a[w * nb:, :] + a[:a.shape[0] - w * nb, :]
            w *= 2
        assert w == win
        i0 = (POOL_HIST + 1 - win) * nb
        s_w = a[i0:i0 + rows, :]
        cnt_w = jnp.minimum(pos1, win).astype(F32)
        if s_sel is None:
            s_sel, cnt_sel = s_w, cnt_w
        else:
            in_later = lane >= gi * group
            s_sel = jnp.where(in_later, s_w, s_sel)
            cnt_sel = jnp.where(in_later, cnt_w, cnt_sel)
    pooled = s_sel / cnt_sel - u_pool
    y_pool = _mm(pooled, wpool_ref[...], precise) * pscale_ref[...]
    y_ref[:, o_pool:o_pool + d_pool] = y_pool.astype(y_ref.dtype)
    ph_scr[0:POOL_HIST * nb, :] = ph_scr[pl.ds(tc * nb, POOL_HIST * nb), :]

    @pl.when(c == last)
    def _():
        pnew_ref[...] = ph_scr[nb:POOL_HIST * nb, :]

    yc = _silu(_layer_norm(ca_scr[...] + cb_ref[...], clg_ref[...], clb_ref[...]))
    y_ref[:, o_conv:o_conv + d_conv] = yc.astype(y_ref.dtype)
    ch_scr[0:CONV_HIST * nb, :] = ch_scr[pl.ds(tc * nb, CONV_HIST * nb), :]

    @pl.when(c == last)
    def _():
        cnew_ref[...] = ch_scr[tap0 * nb:CONV_HIST * nb, :]


def _mixers(x, w1, b1, h0, pool_cache, conv_cache, mw, *, nb, tc, start_pos, precise):
    n_rows, d = x.shape
    n_t = n_rows // nb
    rows = tc * nb
    d_s5 = mw["wglu"].shape[0]
    d_pool = mw["wpool"].shape[0]
    d_conv = mw["conv_w"].shape[1]
    n_taps = mw["n_taps"]
    d_y = d_s5 + d_pool + d_conv
    n_state2 = h0.shape[1]
    weights = [mw["wb"], mw["wc"], mw["lam"], mw["d"], mw["wglu"], mw["bglu"], mw["wpool"], mw["pscale"],
               mw["conv_w"], mw["conv_b"], mw["conv_ln_g"], mw["conv_ln_b"]]
    kern = functools.partial(_mixer_kernel, nb=nb, tc=tc, start_pos=start_pos, d_s5=d_s5, d_pool=d_pool,
                             d_conv=d_conv, n_taps=n_taps, precise=precise)
    return pl.pallas_call(
        kern,
        grid=(n_t // tc,),
        in_specs=[_row_spec(rows, d), _const_spec(w1.shape), _const_spec(b1.shape),
                  _const_spec(h0.shape), _const_spec(pool_cache.shape), _const_spec(conv_cache.shape)]
                 + [_const_spec(w.shape) for w in weights],
        out_specs=(_row_spec(rows, d_y),
                   pl.BlockSpec(h0.shape, lambda c: (0, 0)),
                   pl.BlockSpec(((POOL_HIST - 1) * nb, d_pool), lambda c: (0, 0)),
                   pl.BlockSpec(((n_taps - 1) * nb, d_conv), lambda c: (0, 0))),
        out_shape=(jax.ShapeDtypeStruct((n_rows, d_y), F32 if precise else BF16),
                   jax.ShapeDtypeStruct(h0.shape, F32),
                   jax.ShapeDtypeStruct(((POOL_HIST - 1) * nb, d_pool), F32),
                   jax.ShapeDtypeStruct(((n_taps - 1) * nb, d_conv), F32)),
        scratch_shapes=[pltpu.VMEM((rows, d_y), F32),
                        pltpu.VMEM((rows, n_state2), F32),
                        pltpu.VMEM((nb, n_state2), F32),
                        pltpu.VMEM(((POOL_HIST + tc) * nb, d_pool), F32),
                        pltpu.VMEM(((CONV_HIST + tc) * nb, d_conv), F32),
                        pltpu.VMEM((rows, d_conv), F32)],
        compiler_params=_params("arbitrary"),
        name="token_mixers",
    )(x, w1, b1, h0, pool_cache, conv_cache, *weights)


def _merge_kernel(*refs, alpha, d_s5, d_pool, d_conv, n_exp, aliased, precise):
    with_router = n_exp > 0
    n_in = 11 + (1 if with_router else 0) + (1 if aliased else 0)
    x_ref, y_ref, wg_ref, bg_ref, ps5_ref, ppool_ref, pconv_ref, wo_ref, bo_ref, g_ref, b_ref = refs[:11]
    o_ref = refs[n_in]
    d = x_ref.shape[-1]
    if with_router:
        rt_ref = refs[11]
        ri_ref, rw_ref = refs[n_in + 1:n_in + 3]
        rt = rt_ref[...]
        rt_hi = rt.astype(BF16).astype(F32)
        rt2 = (rt_hi + pltpu.roll(rt - rt_hi, ROUTER_LANES // 2, axis=1)).astype(BF16)

    n_sub = MERGE_SUBTILES if x_ref.shape[0] % (MERGE_SUBTILES * 16) == 0 else 1
    sub = x_ref.shape[0] // n_sub
    for s in range(n_sub):
        rs = slice(s * sub, (s + 1) * sub)
        x = x_ref[rs, :]
        xa = x if precise else x.astype(BF16)
        merged = None
        col = 0
        for j, (width, p_ref) in enumerate(((d_s5, ps5_ref), (d_pool, ppool_ref), (d_conv, pconv_ref))):
            gate = _sigmoid(_mm(xa, wg_ref[:, j * d:(j + 1) * d], precise) + bg_ref[:, j * d:(j + 1) * d])
            term = gate * _mm(y_ref[rs, col:col + width], p_ref[...], precise)
            merged = term if merged is None else merged + term
            col += width
        out = _mm(merged, wo_ref[...], precise) + bo_ref[...]
        x1 = _layer_norm(alpha * x + out, g_ref[...], b_ref[...])
        o_ref[rs, :] = x1
        if with_router:
            x_hi, x_lo = _split(x1)
            r = _dot(x_hi, rt2) + _dot(x_lo, rt2)
            logits = r + pltpu.roll(r, ROUTER_LANES // 2, axis=1)
            lane = lax.broadcasted_iota(jnp.int32, logits.shape, 1).astype(F32)
            neg = jnp.float32(-jnp.inf)
            l1 = jnp.where(lane < n_exp, logits, neg)
            m1 = jnp.max(l1, axis=-1, keepdims=True)
            i1 = jnp.min(jnp.where(l1 == m1, lane, float(ROUTER_LANES)), axis=-1, keepdims=True)
            l2 = jnp.where(lane == i1, neg, l1)
            m2 = jnp.max(l2, axis=-1, keepdims=True)
            i2 = jnp.min(jnp.where(l2 == m2, lane, float(ROUTER_LANES)), axis=-1, keepdims=True)
            e2 = jnp.exp(m2 - m1)
            w1 = 1.0 / (1.0 + e2)
            w2 = e2 * w1
            ri_ref[rs, :] = jnp.where(lane == 0.0, i1, jnp.where(lane == 1.0, i2, 0.0)).astype(jnp.int32)
            rw_ref[rs, :] = jnp.where(lane == 0.0, w1, jnp.where(lane == 1.0, w2, 0.0))


def _merge(x, y, lw, *, alpha, tm, precise, router=None, n_exp=0, shared=None, shared_rows=0, row0=0):
    n, d = x.shape
    d_y = y.shape[-1]
    d_s5, d_pool, d_conv = lw["proj_s5"].shape[0], lw["proj_pool"].shape[0], lw["proj_conv"].shape[0]
    inputs = [x, y, lw["w_gates"], lw["b_gates"], lw["proj_s5"], lw["proj_pool"], lw["proj_conv"], lw["w_out"],
              lw["b_out"], lw["ln1_g"], lw["ln1_b"]]
    in_specs = [_row_spec(tm, d), _row_spec(tm, d_y)] + [_const_spec(w.shape) for w in inputs[2:]]
    aliases = {}
    if router is None:
        out_specs = [_row_spec(tm, d)]
        out_shape = [jax.ShapeDtypeStruct((n, d), F32)]
    else:
        inputs.append(router)
        in_specs.append(_const_spec(router.shape))
        if shared is not None:
            aliases = {len(inputs): 0}
            inputs.append(shared)
            in_specs.append(pl.BlockSpec(memory_space=pl.ANY))
        out_specs = [_row_spec(tm, d, row0 // tm), _row_spec(tm, ROUTER_LANES), _row_spec(tm, ROUTER_LANES)]
        out_shape = [jax.ShapeDtypeStruct((shared_rows, d), F32),
                     jax.ShapeDtypeStruct((n, ROUTER_LANES), jnp.int32),
                     jax.ShapeDtypeStruct((n, ROUTER_LANES), F32)]
    kern = functools.partial(_merge_kernel, alpha=alpha, d_s5=d_s5, d_pool=d_pool, d_conv=d_conv,
                             n_exp=n_exp if router is not None else 0, aliased=shared is not None, precise=precise)
    return pl.pallas_call(
        kern,
        grid=(n // tm,),
        in_specs=in_specs,
        out_specs=tuple(out_specs),
        out_shape=tuple(out_shape),
        input_output_aliases=aliases,
        compiler_params=_params("parallel"),
        name="merge_outproj_ln1",
    )(*inputs)


def _swiglu_tile(x, wg_ref, wu_ref, wd_ref, ff_chunk, precise):
    d_ff = wg_ref.shape[0]
    acc = None
    for c0 in range(0, d_ff, ff_chunk):
        c1 = min(c0 + ff_chunk, d_ff)
        g = _mm(x, wg_ref[c0:c1, :], precise, w_transposed=True)
        u = _mm(x, wu_ref[c0:c1, :], precise, w_transposed=True)
        part = _mm(_silu(g) * u, wd_ref[c0:c1, :], precise)
        acc = part if acc is None else acc + part
    return acc


def _ffn_kernel(x_ref, wg_ref, wu_ref, wd_ref, g_ref, b_ref, o_ref, *, alpha, ff_chunk, precise):
    x = x_ref[...]
    f = _swiglu_tile(x if precise else x.astype(BF16), wg_ref, wu_ref, wd_ref, ff_chunk, precise)
    o_ref[...] = _layer_norm(alpha * x + f, g_ref[...], b_ref[...])


def _ffn_dense(x, wg, wu, wd, ln_g, ln_b, *, alpha, tm, ff_chunk, precise):
    n, d = x.shape
    weights = [wg, wu, wd, ln_g, ln_b]
    return pl.pallas_call(
        functools.partial(_ffn_kernel, alpha=alpha, ff_chunk=ff_chunk, precise=precise),
        grid=(n // tm,),
        in_specs=[_row_spec(tm, d)] + [_const_spec(w.shape) for w in weights],
        out_specs=_row_spec(tm, d),
        out_shape=jax.ShapeDtypeStruct((n, d), F32),
        compiler_params=_params("parallel"),
        name="ffn_dense_ln2",
    )(x, *weights)


def _moe_kernel(te_ref, nt_ref, xs_ref, wg_ref, wu_ref, wd_ref, *rest, ff_chunk, tile0):
    o_ref = rest[-1]
    i = tile0 + pl.program_id(0)

    @pl.when(i < nt_ref[0])
    def _():
        o_ref[...] = _swiglu_tile(xs_ref[...].astype(BF16), wg_ref, wu_ref, wd_ref, ff_chunk, False)

    @pl.when(i >= nt_ref[0])
    def _():
        o_ref[...] = jnp.zeros(o_ref.shape, o_ref.dtype)


def _moe_grouped(tile_expert, n_tiles_used, xs, wg, wu, wd, *, tm, ff_chunk, tile0, n_slots, y_prev=None):
    n_part, d = xs.shape
    d_ff = wd.shape[1]

    def expert_spec(shape, buffers):
        return pl.BlockSpec((None,) + shape, lambda i, te, nt: (te[tile0 + i], 0, 0),
                            pipeline_mode=pl.Buffered(buffers))

    inputs = [tile_expert, n_tiles_used, xs, wg, wu, wd]
    in_specs = [pl.BlockSpec((tm, d), lambda i, te, nt: (i, 0)),
                expert_spec((d_ff, d), 1), expert_spec((d_ff, d), 1), expert_spec((d_ff, d), 2)]
    aliases = {}
    if y_prev is not None:
        aliases = {len(inputs): 0}
        inputs.append(y_prev)
        in_specs.append(pl.BlockSpec(memory_space=pl.ANY))
    grid_spec = pltpu.PrefetchScalarGridSpec(
        num_scalar_prefetch=2,
        grid=(n_part // tm,),
        in_specs=in_specs,
        out_specs=pl.BlockSpec((tm, d), lambda i, te, nt: (tile0 + i, 0)),
    )
    return pl.pallas_call(
        functools.partial(_moe_kernel, ff_chunk=ff_chunk, tile0=tile0),
        grid_spec=grid_spec,
        out_shape=jax.ShapeDtypeStruct((n_slots, d), F32),
        input_output_aliases=aliases,
        compiler_params=_params("arbitrary"),
        name="moe_grouped_swiglu",
    )(*inputs)


def _combine_kernel(x_ref, ya_ref, yb_ref, rw_ref, g_ref, b_ref, *rest, alpha, nb):
    f = rw_ref[:, 0:1] * ya_ref[...] + rw_ref[:, 1:2] * yb_ref[...]
    y = _layer_norm(alpha * x_ref[...] + f, g_ref[...], b_ref[...])
    if nb is None:
        rest[-1][...] = y
        return
    o_ref, t_scr = rest[-2], rest[-1]
    steps = y.shape[0] // nb
    for k in range(t_scr.shape[0]):
        t_scr[k] = y[:, k * LANES:(k + 1) * LANES]
    for b in range(nb):
        for k in range(t_scr.shape[0]):
            o_ref[b, :, k * LANES:(k + 1) * LANES] = t_scr[k, pl.ds(b, steps, stride=nb), :]


def _moe_combine(x1, ya, yb, rw, ln_g, ln_b, *, alpha, tm, x_row0, rows, nb=None, n_seq_steps=None, out_prev=None):
    r0, cnt = rows
    d = x1.shape[1]
    inputs = [x1, ya, yb, rw, ln_g, ln_b]
    in_specs = [_row_spec(tm, d, (x_row0 + r0) // tm), _row_spec(tm, d), _row_spec(tm, d),
                _row_spec(tm, ROUTER_LANES, r0 // tm), _const_spec(ln_g.shape), _const_spec(ln_b.shape)]
    aliases = {}
    if out_prev is not None:
        aliases = {len(inputs): 0}
        inputs.append(out_prev)
        in_specs.append(pl.BlockSpec(memory_space=pl.ANY))
    if nb is None:
        out_specs = _row_spec(tm, d)
        out_shape = jax.ShapeDtypeStruct((cnt, d), F32)
        scratch = []
    else:
        steps = tm // nb
        out_specs = pl.BlockSpec((nb, steps, d), lambda i: (0, r0 // tm + i, 0))
        out_shape = jax.ShapeDtypeStruct((nb, n_seq_steps, d), F32)
        scratch = [pltpu.VMEM((d // LANES, tm, LANES), F32)]
    return pl.pallas_call(
        functools.partial(_combine_kernel, alpha=alpha, nb=nb),
        grid=(cnt // tm,),
        in_specs=in_specs,
        out_specs=out_specs,
        out_shape=out_shape,
        scratch_shapes=scratch,
        input_output_aliases=aliases,
        compiler_params=_params("parallel"),
        name="moe_combine_ln2",
    )(*inputs)


def _moe_route(top_i, n_experts, tm):
    n = top_i.shape[0]
    n_flat = TOP_K * n
    flat_e = top_i.reshape(-1)
    iota = jnp.arange(n_flat, dtype=jnp.int32)
    sorted_e, sorted_j = lax.sort((flat_e, iota), num_keys=1, is_stable=True)
    experts = jnp.arange(n_experts, dtype=jnp.int32)
    counts = jnp.sum((flat_e[:, None] == experts[None, :]).astype(jnp.int32), axis=0)
    start_end = jnp.cumsum(counts)
    padded = ((counts + tm - 1) // tm) * tm
    off_end = jnp.cumsum(padded)
    off = off_end - padded
    shift = off - (start_end - counts)
    n_slots = ((n_flat + n_experts * tm + tm - 1) // tm) * tm
    n_tiles = n_slots // tm
    tile_start = jnp.arange(n_tiles, dtype=jnp.int32) * tm
    tile_expert = jnp.sum((tile_start[:, None] >= off_end[None, :]).astype(jnp.int32), axis=1)
    tile_expert = jnp.minimum(tile_expert, n_experts - 1).astype(jnp.int32)
    n_used = (off_end[-1] // tm).astype(jnp.int32).reshape(1)
    slot = jnp.arange(n_slots, dtype=jnp.int32).reshape(n_tiles, tm)
    t_off, t_cnt, t_shift = off[tile_expert][:, None], counts[tile_expert][:, None], shift[tile_expert][:, None]
    valid = (slot - t_off) < t_cnt
    src = jnp.clip(slot - t_shift, 0, n_flat - 1)
    slot_tok = jnp.where(valid, sorted_j[src] // TOP_K, 0).reshape(n_slots)
    slot_of_sorted = iota + shift[sorted_e]
    _, slots = lax.sort((sorted_j, slot_of_sorted), num_keys=1)
    return slot_tok, tile_expert, n_used, slots.reshape(n, TOP_K)


def _pad_time_major(cache, steps):
    n_b, n_s, n_c = cache.shape
    tmaj = jnp.transpose(cache, (1, 0, 2))
    tmaj = jnp.pad(tmaj, ((steps - n_s, 0), (0, 0), (0, 0)))
    return tmaj.reshape(steps * n_b, n_c)


def _from_time_major(flat, n_b):
    n_c = flat.shape[1]
    return jnp.transpose(flat.reshape(-1, n_b, n_c), (1, 0, 2))


def kernel(x_prompt, x_sample, state_s5_re, state_s5_im, cache_pool, cache_conv, w_in, b_in, s5_a_re, s5_a_im, s5_log_dt, s5_b_re, s5_b_im, s5_c_re, s5_c_im, s5_d, s5_w_glu, s5_b_glu, pool_w, pool_scale, conv_w, conv_b, conv_ln_g, conv_ln_b, proj_s5, proj_pool, proj_conv, w_out, b_out, ln1_g, ln1_b, ln2_g, ln2_b, ffn_w_gate, ffn_w_up, ffn_w_down, moe_router, moe_w_gate, moe_w_up, moe_w_down):
    depth, d_model, _ = w_in.shape
    n_bp, n_tp, _ = x_prompt.shape
    n_bs, n_ts, _ = x_sample.shape
    assert n_ts == 1
    n_grp, n_p = s5_a_re.shape[1:]
    n_h = s5_b_re.shape[-1]
    n_state = n_grp * n_p
    d_s5 = n_grp * n_h
    d_pool = pool_scale.shape[1]
    d_conv = conv_w.shape[2]
    n_taps = conv_w.shape[1]
    d_lin = d_s5 + d_pool
    d_u = d_lin + 2 * d_conv
    n_exp = moe_router.shape[-1]
    alpha = (2.0 * depth) ** 0.25
    ff_chunk = MXU_DIM
    assert cache_pool.shape[2] == POOL_HIST - 1 and n_taps - 1 <= CONV_HIST
    n_p_tok = n_bp * n_tp
    n_tok = n_p_tok + n_bs
    tc_p = min(64, n_tp)
    tm_p = tc_p * n_bp

    def row(v):
        return v.reshape(1, -1).astype(F32)

    def both(w):
        w = w.astype(F32)
        return w.astype(BF16), w

    xp = jnp.transpose(x_prompt, (1, 0, 2)).reshape(n_p_tok, d_model)
    xs = x_sample.reshape(n_bs, d_model)
    zeros_p = dict(h0=jnp.zeros((n_bp, 2 * n_state), F32),
                   pool=jnp.zeros((POOL_HIST * n_bp, d_pool), F32),
                   conv=jnp.zeros((CONV_HIST * n_bp, d_conv), F32))
    outs = {k: [] for k in ("re_p", "im_p", "pool_p", "conv_p", "re_s", "im_s", "pool_s", "conv_s")}

    for l in range(depth):
        lam, bb_re, bb_im = _s5_discretise(s5_a_re[l], s5_a_im[l], s5_log_dt[l], s5_b_re[l], s5_b_im[l])
        conv_w_pad = jnp.pad(conv_w[l].astype(F32), ((0, CONV_HIST - n_taps), (0, 0)))
        shared_mw = dict(lam=lam, d=row(s5_d[l]), bglu=row(s5_b_glu[l]), pscale=row(pool_scale[l]),
                         conv_w=conv_w_pad, conv_b=row(conv_b[l]), conv_ln_g=row(conv_ln_g[l]),
                         conv_ln_b=row(conv_ln_b[l]), n_taps=n_taps)
        mats = dict(wb=both(_s5_input_blocks(bb_re, bb_im, n_grp, n_p)),
                    wc=both(_s5_output_blocks(s5_c_re[l].astype(F32), s5_c_im[l].astype(F32), n_p)),
                    wglu=both(s5_w_glu[l]), wpool=both(_block_diag(pool_w[l].astype(F32))))
        w1 = both(w_in[l, :, :d_u])
        b1 = row(b_in[l, :d_u])
        shared_lw = dict(b_gates=row(b_in[l, d_u:]), b_out=row(b_out[l]), ln1_g=row(ln1_g[l]), ln1_b=row(ln1_b[l]))
        lmats = dict(w_gates=both(w_in[l, :, d_u:]), proj_s5=both(proj_s5[l]), proj_pool=both(proj_pool[l]),
                     proj_conv=both(proj_conv[l]), w_out=both(w_out[l]))
        is_moe = l % 2 == 1
        router = None
        if is_moe:
            assert n_exp <= ROUTER_LANES // 2
            router = jnp.pad(moe_router[l // 2].astype(F32), ((0, 0), (0, ROUTER_LANES - n_exp)))

        mw = dict(shared_mw, **{k: v[0] for k, v in mats.items()})
        lw = dict(shared_lw, **{k: v[0] for k, v in lmats.items()})
        yp, hT_p, pool_p, conv_p = _mixers(xp, w1[0], b1, zeros_p["h0"], zeros_p["pool"], zeros_p["conv"], mw,
                                           nb=n_bp, tc=tc_p, start_pos=0, precise=False)
        res_p = _merge(xp, yp, lw, alpha=alpha, tm=tm_p, precise=False, router=router, n_exp=n_exp,
                       shared_rows=n_tok)
        mw = dict(shared_mw, **{k: v[1] for k, v in mats.items()})
        lw = dict(shared_lw, **{k: v[1] for k, v in lmats.items()})
        h0_s = jnp.concatenate([state_s5_re[l].reshape(n_bs, n_state), state_s5_im[l].reshape(n_bs, n_state)],
                               axis=1).astype(F32)
        ys, hT_s, pool_s, conv_s = _mixers(xs, w1[1], b1, h0_s,
                                           _pad_time_major(cache_pool[l].astype(F32), POOL_HIST),
                                           _pad_time_major(cache_conv[l].astype(F32), CONV_HIST), mw,
                                           nb=n_bs, tc=1, start_pos=PAST_LEN, precise=True)
        res_s = _merge(xs, ys, lw, alpha=alpha, tm=n_bs, precise=True, router=router, n_exp=n_exp,
                       shared=res_p[0] if is_moe else None, shared_rows=n_tok, row0=n_p_tok)

        outs["re_p"].append(hT_p[:, :n_state].reshape(n_bp, n_grp, n_p))
        outs["im_p"].append(hT_p[:, n_state:].reshape(n_bp, n_grp, n_p))
        outs["pool_p"].append(_from_time_major(pool_p, n_bp))
        outs["conv_p"].append(_from_time_major(conv_p, n_bp))
        outs["re_s"].append(hT_s[:, :n_state].reshape(n_bs, n_grp, n_p))
        outs["im_s"].append(hT_s[:, n_state:].reshape(n_bs, n_grp, n_p))
        outs["pool_s"].append(_from_time_major(pool_s, n_bs))
        outs["conv_s"].append(_from_time_major(conv_s, n_bs))

        g2, b2 = row(ln2_g[l]), row(ln2_b[l])
        i = l // 2
        if not is_moe:
            wg, wu = jnp.swapaxes(ffn_w_gate[i], -1, -2).astype(F32), jnp.swapaxes(ffn_w_up[i], -1, -2).astype(F32)
            wd = ffn_w_down[i].astype(F32)
            xp = _ffn_dense(res_p[0], wg, wu, wd, g2, b2, alpha=alpha, tm=tm_p, ff_chunk=ff_chunk,
                            precise=False)
            xs = _ffn_dense(res_s[0], wg, wu, wd, g2, b2, alpha=alpha, tm=n_bs, ff_chunk=ff_chunk,
                            precise=True)
        else:
            wg, wu = jnp.swapaxes(moe_w_gate[i], -1, -2).astype(F32), jnp.swapaxes(moe_w_up[i], -1, -2).astype(F32)
            wd = moe_w_down[i].astype(F32)
            _, ri_p, rw_p = res_p
            x1_all, ri_s, rw_s = res_s
            top_i = jnp.concatenate([ri_p[:, :TOP_K], ri_s[:, :TOP_K]], axis=0)
            tm_moe = 512
            slot_tok, tile_expert, n_used, slots = _moe_route(top_i, n_exp, tm_moe)
            n_tiles = slot_tok.shape[0] // tm_moe
            bounds = sorted({0} | {max(1, round(n_tiles * f)) for f in MOE_PART_ENDS})
            y_sorted = None
            for t0, t1 in zip(bounds[:-1], bounds[1:]):
                x_part = x1_all.at[slot_tok[t0 * tm_moe:t1 * tm_moe]].get(mode="promise_in_bounds")
                y_sorted = _moe_grouped(tile_expert, n_used, x_part, wg, wu, wd, tm=tm_moe, ff_chunk=ff_chunk,
                                        tile0=t0, n_slots=slot_tok.shape[0], y_prev=y_sorted)
            def picked(r0, cnt):
                return [y_sorted.at[slots[r0:r0 + cnt, k]].get(mode="promise_in_bounds") for k in range(TOP_K)]

            last_layer = l == depth - 1
            p_tiles = n_p_tok // tm_p
            bounds = sorted({tm_p * ((p_tiles * k) // MOE_PARTS) for k in range(MOE_PARTS + 1)})
            xp = None
            for r0, r1 in zip(bounds[:-1], bounds[1:]):
                ya, yb = picked(r0, r1 - r0)
                if last_layer:
                    xp = _moe_combine(x1_all, ya, yb, rw_p, g2, b2, alpha=alpha, tm=tm_p, x_row0=0,
                                      rows=(r0, r1 - r0), nb=n_bp, n_seq_steps=n_tp, out_prev=xp)
                else:
                    part = _moe_combine(x1_all, ya, yb, rw_p, g2, b2, alpha=alpha, tm=tm_p, x_row0=0,
                                        rows=(r0, r1 - r0))
                    xp = part if xp is None else jnp.concatenate([xp, part], axis=0)
            ya, yb = picked(n_p_tok, n_bs)
            xs = _moe_combine(x1_all, ya, yb, rw_s, g2, b2, alpha=alpha, tm=n_bs, x_row0=n_p_tok, rows=(0, n_bs))

    y_prompt = xp if xp.ndim == 3 else _from_time_major(xp, n_bp)
    st = lambda k: jnp.stack(outs[k])
    return (y_prompt, xs.reshape(n_bs, 1, d_model), st("re_p"), st("im_p"), st("pool_p"), st("conv_p"),
            st("re_s"), st("im_s"), st("pool_s"), st("conv_s"))
```

```python
import functools

import jax
import jax.numpy as jnp
from jax import lax
from jax.experimental import pallas as pl
from jax.experimental.pallas import tpu as pltpu

F32 = jnp.float32
BF16 = jnp.bfloat16

LN_EPS = 1e-5
POOL_WINDOWS = (2, 4, 8, 16)
PAST_LEN = 16384
TOP_K = 2
POOL_HIST = 16
CONV_HIST = 32
ROUTER_LANES = 128
LANES = 128
SUBLANES = 8
MXU_DIM = 256
MIXER_CHUNK_STEPS = 64
MOE_TILE_ROWS = 512
MERGE_SUBTILES = 2
MOE_PARTS = 4
MOE_PART_ENDS = (0.12, 0.4, 0.7, 1.0)
VMEM_LIMIT = 58 * 1024 * 1024


def _sigmoid(x):
    return 0.5 * jnp.tanh(0.5 * x) + 0.5


def _silu(x):
    return x * _sigmoid(x)


def _layer_norm(x, g, b):
    mu = jnp.mean(x, axis=-1, keepdims=True)
    xc = x - mu
    var = jnp.mean(xc * xc, axis=-1, keepdims=True)
    return xc * lax.rsqrt(var + LN_EPS) * g + b


def _dot(a, b, w_transposed=False):
    dims = (((1,), (1 if w_transposed else 0,)), ((), ()))
    return lax.dot_general(a, b, dims, preferred_element_type=F32)


def _split(v):
    hi = v.astype(BF16)
    return hi, (v - hi.astype(F32)).astype(BF16)


def _mm(a, w, precise, w_transposed=False):
    if not precise:
        return _dot(a.astype(BF16), w.astype(BF16), w_transposed)
    a_hi, a_lo = _split(a)
    w_hi, w_lo = _split(w)
    m = a.shape[0]
    r = _dot(jnp.concatenate([a_hi, a_lo], axis=0), w_hi, w_transposed)
    return r[:m] + r[m:] + _dot(a_hi, w_lo, w_transposed)


def _params(*sem):
    return pltpu.CompilerParams(dimension_semantics=sem, vmem_limit_bytes=VMEM_LIMIT)


def _const_spec(shape):
    return pl.BlockSpec(shape, lambda *_: (0,) * len(shape), pipeline_mode=pl.Buffered(1))


def _row_spec(tm, d, blk0=0):
    return pl.BlockSpec((tm, d), lambda i: (blk0 + i, 0))


def _s5_disc_kernel(are_ref, aim_ref, ldt_ref, bre_ref, bim_ref, lam_ref, bbre_ref, bbim_ref):
    a_re = are_ref[...]
    a_im = aim_ref[...]
    dt = jnp.exp(ldt_ref[...])
    mag = jnp.exp(a_re * dt)
    lr = mag * jnp.cos(a_im * dt)
    li = mag * jnp.sin(a_im * dt)
    den = a_re * a_re + a_im * a_im
    nr = lr - 1.0
    cr = (nr * a_re + li * a_im) / den
    ci = (li * a_re - nr * a_im) / den
    lam_ref[0:1, :] = lr
    lam_ref[1:2, :] = li
    b_re = bre_ref[...]
    b_im = bim_ref[...]
    bbre_ref[...] = cr * b_re - ci * b_im
    bbim_ref[...] = cr * b_im + ci * b_re


def _s5_discretise(a_re, a_im, log_dt, b_re, b_im):
    n_g, n_p = a_re.shape
    n_h = b_re.shape[-1]
    n_s = n_g * n_p
    ldt = jnp.broadcast_to(log_dt[:, None], (n_g, n_p)).reshape(1, n_s)
    b_re_t = b_re.reshape(n_s, n_h).T
    b_im_t = b_im.reshape(n_s, n_h).T
    return pl.pallas_call(
        _s5_disc_kernel,
        out_shape=(jax.ShapeDtypeStruct((2, n_s), F32),
                   jax.ShapeDtypeStruct((n_h, n_s), F32),
                   jax.ShapeDtypeStruct((n_h, n_s), F32)),
        name="s5_discretise",
    )(a_re.reshape(1, n_s), a_im.reshape(1, n_s), ldt, b_re_t, b_im_t)


def _block_diag(blocks):
    n_g, r, c = blocks.shape
    eye = jnp.eye(n_g, dtype=blocks.dtype)
    return (eye[:, None, :, None] * blocks[:, :, None, :]).reshape(n_g * r, n_g * c)


def _s5_input_blocks(bb_re, bb_im, n_grp, n_p):
    n_h = bb_re.shape[0]
    gpb = MXU_DIM // n_p
    ch = gpb * n_h
    assert MXU_DIM % n_p == 0 and LANES % ch == 0 and n_grp % gpb == 0
    n_blk = n_grp // gpb
    per_chunk = LANES // ch

    def blocks(bb):
        g = jnp.transpose(bb.reshape(n_h, n_grp, n_p), (1, 0, 2)).reshape(n_blk, gpb, n_h, n_p)
        dense = jax.vmap(_block_diag)(g)
        return jnp.stack([jnp.pad(dense[j], ((ch * (j % per_chunk), LANES - ch * (j % per_chunk + 1)), (0, 0)))
                          for j in range(n_blk)])

    return jnp.concatenate([blocks(bb_re), blocks(bb_im)], axis=2)


def _s5_output_blocks(c_re, c_im, n_p):
    n_grp, n_h, _ = c_re.shape
    gpb = MXU_DIM // n_h
    assert n_grp % gpb == 0
    n_blk = n_grp // gpb

    def blocks(cm):
        g = jnp.transpose(cm, (0, 2, 1)).reshape(n_blk, gpb, n_p, n_h)
        return jax.vmap(_block_diag)(g)

    return jnp.concatenate([blocks(c_re), blocks(-c_im)], axis=1)


def _mixer_kernel(x_ref, w1_ref, b1_ref, h0_ref, pc_ref, cc_ref, wb_ref, wc_ref, lam_ref, d_ref, wglu_ref, bglu_ref,
                  wpool_ref, pscale_ref, cw_ref, cb_ref, clg_ref, clb_ref,
                  y_ref, hT_ref, pnew_ref, cnew_ref,
                  u_ref, s_scr, h_scr, ph_scr, ch_scr, ca_scr, *, nb, tc, start_pos, d_s5, d_pool, d_conv, n_taps,
                  precise):
    c = pl.program_id(0)
    last = pl.num_programs(0) - 1
    rows = tc * nb
    n_state = lam_ref.shape[1]
    o_pool = d_s5
    o_conv = d_s5 + d_pool

    @pl.when(c == 0)
    def _():
        h_scr[...] = h0_ref[...]
        ph_scr[0:POOL_HIST * nb, :] = pc_ref[...]
        ch_scr[0:CONV_HIST * nb, :] = cc_ref[...]

    x_in = x_ref[...] if precise else x_ref[...].astype(BF16)

    def proj(c0, c1):
        return _mm(x_in, w1_ref[:, c0:c1], precise) + b1_ref[:, c0:c1]

    for c0 in range(0, o_conv, MXU_DIM):
        u_ref[:, c0:c0 + MXU_DIM] = proj(c0, c0 + MXU_DIM)
    for c0 in range(0, d_conv, MXU_DIM):
        a0 = o_conv + c0
        u_ref[:, a0:a0 + MXU_DIM] = proj(a0, a0 + MXU_DIM) * _sigmoid(proj(a0 + d_conv, a0 + d_conv + MXU_DIM))

    ch_scr[pl.ds(CONV_HIST * nb, rows), :] = u_ref[:, o_conv:o_conv + d_conv]
    rb = min(rows, 64)
    tap0 = CONV_HIST - (n_taps - 1)
    for base in range(0, rows, rb):
        acc = [None, None]
        for k in range(n_taps):
            w_k = jnp.concatenate([cw_ref[SUBLANES * k:SUBLANES * (k + 1), :]] * (rb // SUBLANES), axis=0)
            term = w_k * ch_scr[base + (tap0 + k) * nb:base + (tap0 + k) * nb + rb, :]
            acc[k % 2] = term if acc[k % 2] is None else acc[k % 2] + term
        ca_scr[base:base + rb, :] = acc[0] + acc[1]

    u_s5 = u_ref[:, 0:d_s5]
    n_in_blk = wb_ref.shape[0]
    per_chunk = LANES // (d_s5 // n_in_blk)
    for j in range(n_in_blk):
        k0 = (j // per_chunk) * LANES
        bu = _mm(u_s5[:, k0:k0 + LANES], wb_ref[j], precise)
        s_scr[:, j * MXU_DIM:(j + 1) * MXU_DIM] = bu[:, :MXU_DIM]
        s_scr[:, n_state + j * MXU_DIM:n_state + (j + 1) * MXU_DIM] = bu[:, MXU_DIM:]
    if tc == 1:
        lr = lam_ref[0:1, :]
        li = lam_ref[1:2, :]
        h_re = h_scr[:, 0:n_state]
        h_im = h_scr[:, n_state:]
        n_re = lr * h_re - li * h_im + s_scr[:, 0:n_state]
        n_im = lr * h_im + li * h_re + s_scr[:, n_state:]
        s_scr[:, 0:n_state] = n_re
        s_scr[:, n_state:] = n_im
        h_scr[:, 0:n_state] = n_re
        h_scr[:, n_state:] = n_im
    else:
        assert nb == 8
        n_part = 2
        half = n_state // n_part
        for hf in range(n_part):
            re0 = hf * half
            im0 = n_state + hf * half
            lr = jnp.broadcast_to(lam_ref[0:1, re0:re0 + half], (nb, half))
            li = jnp.broadcast_to(lam_ref[1:2, re0:re0 + half], (nb, half))

            def step(t, carry, re0=re0, im0=im0, lr=lr, li=li):
                h_re, h_im = carry
                at_t = pl.ds(pl.multiple_of(t * nb, nb), nb)
                n_re = lr * h_re - li * h_im + s_scr[at_t, re0:re0 + half]
                n_im = lr * h_im + li * h_re + s_scr[at_t, im0:im0 + half]
                s_scr[at_t, re0:re0 + half] = n_re
                s_scr[at_t, im0:im0 + half] = n_im
                return n_re, n_im

            h_re, h_im = lax.fori_loop(0, tc, step, (h_scr[:, re0:re0 + half], h_scr[:, im0:im0 + half]),
                                       unroll=4)
            h_scr[:, re0:re0 + half] = h_re
            h_scr[:, im0:im0 + half] = h_im

    n_out_blk = wc_ref.shape[0]
    spb = n_state // n_out_blk
    ys = []
    for j in range(n_out_blk):
        h_re_b = s_scr[:, j * spb:(j + 1) * spb]
        h_im_b = s_scr[:, n_state + j * spb:n_state + (j + 1) * spb]
        ys.append(_mm(h_re_b, wc_ref[j, 0:spb, :], precise) + _mm(h_im_b, wc_ref[j, spb:, :], precise))
    y = jnp.concatenate(ys, axis=1) + d_ref[...] * u_s5
    y = jax.nn.gelu(y)
    y = y * _sigmoid(_mm(y, wglu_ref[...], precise) + bglu_ref[...])
    y_ref[:, 0:d_s5] = y.astype(y_ref.dtype)

    @pl.when(c == last)
    def _():
        hT_ref[...] = h_scr[...]

    u_pool = u_ref[:, o_pool:o_pool + d_pool]
    ph_scr[pl.ds(POOL_HIST * nb, rows), :] = u_pool
    a = ph_scr[...]
    row_t = lax.broadcasted_iota(jnp.int32, (rows, d_pool), 0) >> (nb.bit_length() - 1)
    pos1 = start_pos + c * tc + row_t + 1
    lane = lax.broadcasted_iota(jnp.int32, (rows, d_pool), 1)
    group = d_pool // len(POOL_WINDOWS)
    w = 1
    s_sel = None
    cnt_sel = None
    for gi, win in enumerate(POOL_WINDOWS):
        while w < win:
            a = a[w * nb:, :] + a[:a.shape[0] - w * nb, :]
            w *= 2
        assert w == win
        i0 = (POOL_HIST + 1 - win) * nb
        s_w = a[i0:i0 + rows, :]
        cnt_w = jnp.minimum(pos1, win).astype(F32)
        if s_sel is None:
            s_sel, cnt_sel = s_w, cnt_w
        else:
            in_later = lane >= gi * group
            s_sel = jnp.where(in_later, s_w, s_sel)
            cnt_sel = jnp.where(in_later, cnt_w, cnt_sel)
    pooled = s_sel / cnt_sel - u_pool
    y_pool = _mm(pooled, wpool_ref[...], precise) * pscale_ref[...]
    y_ref[:, o_pool:o_pool + d_pool] = y_pool.astype(y_ref.dtype)
    ph_scr[0:POOL_HIST * nb, :] = ph_scr[pl.ds(tc * nb, POOL_HIST * nb), :]

    @pl.when(c == last)
    def _():
        pnew_ref[...] = ph_scr[nb:POOL_HIST * nb, :]

    yc = _silu(_layer_norm(ca_scr[...] + cb_ref[...], clg_ref[...], clb_ref[...]))
    y_ref[:, o_conv:o_conv + d_conv] = yc.astype(y_ref.dtype)
    ch_scr[0:CONV_HIST * nb, :] = ch_scr[pl.ds(tc * nb, CONV_HIST * nb), :]

    @pl.when(c == last)
    def _():
        cnew_ref[...] = ch_scr[tap0 * nb:CONV_HIST * nb, :]


def _mixers(x, w1, b1, h0, pool_cache, conv_cache, mw, *, nb, tc, start_pos, precise):
    n_rows, d = x.shape
    n_t = n_rows // nb
    rows = tc * nb
    d_s5 = mw["wglu"].shape[0]
    d_pool = mw["wpool"].shape[0]
    d_conv = mw["conv_w"].shape[1]
    n_taps = mw["n_taps"]
    d_y = d_s5 + d_pool + d_conv
    n_state2 = h0.shape[1]
    weights = [mw["wb"], mw["wc"], mw["lam"], mw["d"], mw["wglu"], mw["bglu"], mw["wpool"], mw["pscale"],
               mw["conv_w"], mw["conv_b"], mw["conv_ln_g"], mw["conv_ln_b"]]
    kern = functools.partial(_mixer_kernel, nb=nb, tc=tc, start_pos=start_pos, d_s5=d_s5, d_pool=d_pool,
                             d_conv=d_conv, n_taps=n_taps, precise=precise)
    return pl.pallas_call(
        kern,
        grid=(n_t // tc,),
        in_specs=[_row_spec(rows, d), _const_spec(w1.shape), _const_spec(b1.shape),
                  _const_spec(h0.shape), _const_spec(pool_cache.shape), _const_spec(conv_cache.shape)]
                 + [_const_spec(w.shape) for w in weights],
        out_specs=(_row_spec(rows, d_y),
                   pl.BlockSpec(h0.shape, lambda c: (0, 0)),
                   pl.BlockSpec(((POOL_HIST - 1) * nb, d_pool), lambda c: (0, 0)),
                   pl.BlockSpec(((n_taps - 1) * nb, d_conv), lambda c: (0, 0))),
        out_shape=(jax.ShapeDtypeStruct((n_rows, d_y), F32 if precise else BF16),
                   jax.ShapeDtypeStruct(h0.shape, F32),
                   jax.ShapeDtypeStruct(((POOL_HIST - 1) * nb, d_pool), F32),
                   jax.ShapeDtypeStruct(((n_taps - 1) * nb, d_conv), F32)),
        scratch_shapes=[pltpu.VMEM((rows, d_y), F32),
                        pltpu.VMEM((rows, n_state2), F32),
                        pltpu.VMEM((nb, n_state2), F32),
                        pltpu.VMEM(((POOL_HIST + tc) * nb, d_pool), F32),
                        pltpu.VMEM(((CONV_HIST + tc) * nb, d_conv), F32),
                        pltpu.VMEM((rows, d_conv), F32)],
        compiler_params=_params("arbitrary"),
        name="token_mixers",
    )(x, w1, b1, h0, pool_cache, conv_cache, *weights)


def _merge_kernel(*refs, alpha, d_s5, d_pool, d_conv, n_exp, aliased, precise):
    with_router = n_exp > 0
    n_in = 11 + (1 if with_router else 0) + (1 if aliased else 0)
    x_ref, y_ref, wg_ref, bg_ref, ps5_ref, ppool_ref, pconv_ref, wo_ref, bo_ref, g_ref, b_ref = refs[:11]
    o_ref = refs[n_in]
    d = x_ref.shape[-1]
    if with_router:
        rt_ref = refs[11]
        ri_ref, rw_ref = refs[n_in + 1:n_in + 3]
        rt = rt_ref[...]
        rt_hi = rt.astype(BF16).astype(F32)
        rt2 = (rt_hi + pltpu.roll(rt - rt_hi, ROUTER_LANES // 2, axis=1)).astype(BF16)

    n_sub = MERGE_SUBTILES if x_ref.shape[0] % (MERGE_SUBTILES * 16) == 0 else 1
    sub = x_ref.shape[0] // n_sub
    for s in range(n_sub):
        rs = slice(s * sub, (s + 1) * sub)
        x = x_ref[rs, :]
        xa = x if precise else x.astype(BF16)
        merged = None
        col = 0
        for j, (width, p_ref) in enumerate(((d_s5, ps5_ref), (d_pool, ppool_ref), (d_conv, pconv_ref))):
            gate = _sigmoid(_mm(xa, wg_ref[:, j * d:(j + 1) * d], precise) + bg_ref[:, j * d:(j + 1) * d])
            term = gate * _mm(y_ref[rs, col:col + width], p_ref[...], precise)
            merged = term if merged is None else merged + term
            col += width
        out = _mm(merged, wo_ref[...], precise) + bo_ref[...]
        x1 = _layer_norm(alpha * x + out, g_ref[...], b_ref[...])
        o_ref[rs, :] = x1
        if with_router:
            x_hi, x_lo = _split(x1)
            r = _dot(x_hi, rt2) + _dot(x_lo, rt2)
            logits = r + pltpu.roll(r, ROUTER_LANES // 2, axis=1)
            lane = lax.broadcasted_iota(jnp.int32, logits.shape, 1).astype(F32)
            neg = jnp.float32(-jnp.inf)
            l1 = jnp.where(lane < n_exp, logits, neg)
            m1 = jnp.max(l1, axis=-1, keepdims=True)
            i1 = jnp.min(jnp.where(l1 == m1, lane, float(ROUTER_LANES)), axis=-1, keepdims=True)
            l2 = jnp.where(lane == i1, neg, l1)
            m2 = jnp.max(l2, axis=-1, keepdims=True)
            i2 = jnp.min(jnp.where(l2 == m2, lane, float(ROUTER_LANES)), axis=-1, keepdims=True)
            e2 = jnp.exp(m2 - m1)
            w1 = 1.0 / (1.0 + e2)
            w2 = e2 * w1
            ri_ref[rs, :] = jnp.where(lane == 0.0, i1, jnp.where(lane == 1.0, i2, 0.0)).astype(jnp.int32)
            rw_ref[rs, :] = jnp.where(lane == 0.0, w1, jnp.where(lane == 1.0, w2, 0.0))


def _merge(x, y, lw, *, alpha, tm, precise, router=None, n_exp=0, shared=None, shared_rows=0, row0=0):
    n, d = x.shape
    d_y = y.shape[-1]
    d_s5, d_pool, d_conv = lw["proj_s5"].shape[0], lw["proj_pool"].shape[0], lw["proj_conv"].shape[0]
    inputs = [x, y, lw["w_gates"], lw["b_gates"], lw["proj_s5"], lw["proj_pool"], lw["proj_conv"], lw["w_out"],
              lw["b_out"], lw["ln1_g"], lw["ln1_b"]]
    in_specs = [_row_spec(tm, d), _row_spec(tm, d_y)] + [_const_spec(w.shape) for w in inputs[2:]]
    aliases = {}
    if router is None:
        out_specs = [_row_spec(tm, d)]
        out_shape = [jax.ShapeDtypeStruct((n, d), F32)]
    else:
        inputs.append(router)
        in_specs.append(_const_spec(router.shape))
        if shared is not None:
            aliases = {len(inputs): 0}
            inputs.append(shared)
            in_specs.append(pl.BlockSpec(memory_space=pl.ANY))
        out_specs = [_row_spec(tm, d, row0 // tm), _row_spec(tm, ROUTER_LANES), _row_spec(tm, ROUTER_LANES)]
        out_shape = [jax.ShapeDtypeStruct((shared_rows, d), F32),
                     jax.ShapeDtypeStruct((n, ROUTER_LANES), jnp.int32),
                     jax.ShapeDtypeStruct((n, ROUTER_LANES), F32)]
    kern = functools.partial(_merge_kernel, alpha=alpha, d_s5=d_s5, d_pool=d_pool, d_conv=d_conv,
                             n_exp=n_exp if router is not None else 0, aliased=shared is not None, precise=precise)
    return pl.pallas_call(
        kern,
        grid=(n // tm,),
        in_specs=in_specs,
        out_specs=tuple(out_specs),
        out_shape=tuple(out_shape),
        input_output_aliases=aliases,
        compiler_params=_params("parallel"),
        name="merge_outproj_ln1",
    )(*inputs)


def _swiglu_tile(x, wg_ref, wu_ref, wd_ref, ff_chunk, precise):
    d_ff = wg_ref.shape[0]
    acc = None
    for c0 in range(0, d_ff, ff_chunk):
        c1 = min(c0 + ff_chunk, d_ff)
        g = _mm(x, wg_ref[c0:c1, :], precise, w_transposed=True)
        u = _mm(x, wu_ref[c0:c1, :], precise, w_transposed=True)
        part = _mm(_silu(g) * u, wd_ref[c0:c1, :], precise)
        acc = part if acc is None else acc + part
    return acc


def _ffn_kernel(x_ref, wg_ref, wu_ref, wd_ref, g_ref, b_ref, o_ref, *, alpha, ff_chunk, precise):
    x = x_ref[...]
    f = _swiglu_tile(x if precise else x.astype(BF16), wg_ref, wu_ref, wd_ref, ff_chunk, precise)
    o_ref[...] = _layer_norm(alpha * x + f, g_ref[...], b_ref[...])


def _ffn_dense(x, wg, wu, wd, ln_g, ln_b, *, alpha, tm, ff_chunk, precise):
    n, d = x.shape
    weights = [wg, wu, wd, ln_g, ln_b]
    return pl.pallas_call(
        functools.partial(_ffn_kernel, alpha=alpha, ff_chunk=ff_chunk, precise=precise),
        grid=(n // tm,),
        in_specs=[_row_spec(tm, d)] + [_const_spec(w.shape) for w in weights],
        out_specs=_row_spec(tm, d),
        out_shape=jax.ShapeDtypeStruct((n, d), F32),
        compiler_params=_params("parallel"),
        name="ffn_dense_ln2",
    )(x, *weights)


def _moe_kernel(te_ref, nt_ref, xs_ref, wg_ref, wu_ref, wd_ref, *rest, ff_chunk, tile0):
    o_ref = rest[-1]
    i = tile0 + pl.program_id(0)

    @pl.when(i < nt_ref[0])
    def _():
        o_ref[...] = _swiglu_tile(xs_ref[...].astype(BF16), wg_ref, wu_ref, wd_ref, ff_chunk, False)

    @pl.when(i >= nt_ref[0])
    def _():
        o_ref[...] = jnp.zeros(o_ref.shape, o_ref.dtype)


def _moe_grouped(tile_expert, n_tiles_used, xs, wg, wu, wd, *, tm, ff_chunk, tile0, n_slots, y_prev=None):
    n_part, d = xs.shape
    d_ff = wd.shape[1]

    def expert_spec(shape):
        return pl.BlockSpec((None,) + shape, lambda i, te, nt: (te[tile0 + i], 0, 0))

    inputs = [tile_expert, n_tiles_used, xs, wg, wu, wd]
    in_specs = [pl.BlockSpec((tm, d), lambda i, te, nt: (i, 0)),
                expert_spec((d_ff, d)), expert_spec((d_ff, d)), expert_spec((d_ff, d))]
    aliases = {}
    if y_prev is not None:
        aliases = {len(inputs): 0}
        inputs.append(y_prev)
        in_specs.append(pl.BlockSpec(memory_space=pl.ANY))
    grid_spec = pltpu.PrefetchScalarGridSpec(
        num_scalar_prefetch=2,
        grid=(n_part // tm,),
        in_specs=in_specs,
        out_specs=pl.BlockSpec((tm, d), lambda i, te, nt: (tile0 + i, 0)),
    )
    return pl.pallas_call(
        functools.partial(_moe_kernel, ff_chunk=ff_chunk, tile0=tile0),
        grid_spec=grid_spec,
        out_shape=jax.ShapeDtypeStruct((n_slots, d), F32),
        input_output_aliases=aliases,
        compiler_params=_params("arbitrary"),
        name="moe_grouped_swiglu",
    )(*inputs)


def _combine_kernel(x_ref, ya_ref, yb_ref, rw_ref, g_ref, b_ref, *rest, alpha, nb):
    f = rw_ref[:, 0:1] * ya_ref[...] + rw_ref[:, 1:2] * yb_ref[...]
    y = _layer_norm(alpha * x_ref[...] + f, g_ref[...], b_ref[...])
    if nb is None:
        rest[-1][...] = y
        return
    o_ref, t_scr = rest[-2], rest[-1]
    steps = y.shape[0] // nb
    for k in range(t_scr.shape[0]):
        t_scr[k] = y[:, k * LANES:(k + 1) * LANES]
    for b in range(nb):
        for k in range(t_scr.shape[0]):
            o_ref[b, :, k * LANES:(k + 1) * LANES] = t_scr[k, pl.ds(b, steps, stride=nb), :]


def _moe_combine(x1, ya, yb, rw, ln_g, ln_b, *, alpha, tm, x_row0, rows, nb=None, n_seq_steps=None, out_prev=None):
    r0, cnt = rows
    d = x1.shape[1]
    inputs = [x1, ya, yb, rw, ln_g, ln_b]
    in_specs = [_row_spec(tm, d, (x_row0 + r0) // tm), _row_spec(tm, d), _row_spec(tm, d),
                _row_spec(tm, ROUTER_LANES, r0 // tm), _const_spec(ln_g.shape), _const_spec(ln_b.shape)]
    aliases = {}
    if out_prev is not None:
        aliases = {len(inputs): 0}
        inputs.append(out_prev)
        in_specs.append(pl.BlockSpec(memory_space=pl.ANY))
    if nb is None:
        out_specs = _row_spec(tm, d)
        out_shape = jax.ShapeDtypeStruct((cnt, d), F32)
        scratch = []
    else:
        steps = tm // nb
        out_specs = pl.BlockSpec((nb, steps, d), lambda i: (0, r0 // tm + i, 0))
        out_shape = jax.ShapeDtypeStruct((nb, n_seq_steps, d), F32)
        scratch = [pltpu.VMEM((d // LANES, tm, LANES), F32)]
    return pl.pallas_call(
        functools.partial(_combine_kernel, alpha=alpha, nb=nb),
        grid=(cnt // tm,),
        in_specs=in_specs,
        out_specs=out_specs,
        out_shape=out_shape,
        scratch_shapes=scratch,
        input_output_aliases=aliases,
        compiler_params=_params("parallel"),
        name="moe_combine_ln2",
    )(*inputs)


def _moe_route(top_i, n_experts, tm):
    n = top_i.shape[0]
    n_flat = TOP_K * n
    flat_e = top_i.reshape(-1)
    iota = jnp.arange(n_flat, dtype=jnp.int32)
    sorted_e, sorted_j = lax.sort((flat_e, iota), num_keys=1, is_stable=True)
    experts = jnp.arange(n_experts, dtype=jnp.int32)
    counts = jnp.sum((flat_e[:, None] == experts[None, :]).astype(jnp.int32), axis=0)
    start_end = jnp.cumsum(counts)
    padded = ((counts + tm - 1) // tm) * tm
    off_end = jnp.cumsum(padded)
    off = off_end - padded
    shift = off - (start_end - counts)
    n_slots = ((n_flat + n_experts * tm + tm - 1) // tm) * tm
    n_tiles = n_slots // tm
    tile_start = jnp.arange(n_tiles, dtype=jnp.int32) * tm
    tile_expert = jnp.sum((tile_start[:, None] >= off_end[None, :]).astype(jnp.int32), axis=1)
    tile_expert = jnp.minimum(tile_expert, n_experts - 1).astype(jnp.int32)
    n_used = (off_end[-1] // tm).astype(jnp.int32).reshape(1)
    slot = jnp.arange(n_slots, dtype=jnp.int32).reshape(n_tiles, tm)
    t_off, t_cnt, t_shift = off[tile_expert][:, None], counts[tile_expert][:, None], shift[tile_expert][:, None]
    valid = (slot - t_off) < t_cnt
    src = jnp.clip(slot - t_shift, 0, n_flat - 1)
    slot_tok = jnp.where(valid, sorted_j[src] // TOP_K, 0).reshape(n_slots)
    slot_of_sorted = iota + shift[sorted_e]
    _, slots = lax.sort((sorted_j, slot_of_sorted), num_keys=1)
    return slot_tok, tile_expert, n_used, slots.reshape(n, TOP_K)


def _pad_time_major(cache, steps):
    n_b, n_s, n_c = cache.shape
    tmaj = jnp.transpose(cache, (1, 0, 2))
    tmaj = jnp.pad(tmaj, ((steps - n_s, 0), (0, 0), (0, 0)))
    return tmaj.reshape(steps * n_b, n_c)


def _from_time_major(flat, n_b):
    n_c = flat.shape[1]
    return jnp.transpose(flat.reshape(-1, n_b, n_c), (1, 0, 2))


def kernel(x_prompt, x_sample, state_s5_re, state_s5_im, cache_pool, cache_conv, w_in, b_in, s5_a_re, s5_a_im, s5_log_dt, s5_b_re, s5_b_im, s5_c_re, s5_c_im, s5_d, s5_w_glu, s5_b_glu, pool_w, pool_scale, conv_w, conv_b, conv_ln_g, conv_ln_b, proj_s5, proj_pool, proj_conv, w_out, b_out, ln1_g, ln1_b, ln2_g, ln2_b, ffn_w_gate, ffn_w_up, ffn_w_down, moe_router, moe_w_gate, moe_w_up, moe_w_down):
    depth, d_model, _ = w_in.shape
    n_bp, n_tp, _ = x_prompt.shape
    n_bs, n_ts, _ = x_sample.shape
    assert n_ts == 1
    n_grp, n_p = s5_a_re.shape[1:]
    n_h = s5_b_re.shape[-1]
    n_state = n_grp * n_p
    d_s5 = n_grp * n_h
    d_pool = pool_scale.shape[1]
    d_conv = conv_w.shape[2]
    n_taps = conv_w.shape[1]
    d_lin = d_s5 + d_pool
    d_u = d_lin + 2 * d_conv
    n_exp = moe_router.shape[-1]
    alpha = (2.0 * depth) ** 0.25
    ff_chunk = MXU_DIM
    assert cache_pool.shape[2] == POOL_HIST - 1 and n_taps - 1 <= CONV_HIST
    n_p_tok = n_bp * n_tp
    n_tok = n_p_tok + n_bs
    tc_p = min(MIXER_CHUNK_STEPS, n_tp)
    tm_p = tc_p * n_bp

    def row(v):
        return v.reshape(1, -1).astype(F32)

    def both(w):
        w = w.astype(F32)
        return w.astype(BF16), w

    xp = jnp.transpose(x_prompt, (1, 0, 2)).reshape(n_p_tok, d_model)
    xs = x_sample.reshape(n_bs, d_model)
    zeros_p = dict(h0=jnp.zeros((n_bp, 2 * n_state), F32),
                   pool=jnp.zeros((POOL_HIST * n_bp, d_pool), F32),
                   conv=jnp.zeros((CONV_HIST * n_bp, d_conv), F32))
    outs = {k: [] for k in ("re_p", "im_p", "pool_p", "conv_p", "re_s", "im_s", "pool_s", "conv_s")}

    for l in range(depth):
        lam, bb_re, bb_im = _s5_discretise(s5_a_re[l], s5_a_im[l], s5_log_dt[l], s5_b_re[l], s5_b_im[l])
        conv_w_pad = jnp.repeat(conv_w[l].astype(F32), SUBLANES, axis=0)
        shared_mw = dict(lam=lam, d=row(s5_d[l]), bglu=row(s5_b_glu[l]), pscale=row(pool_scale[l]),
                         conv_w=conv_w_pad, conv_b=row(conv_b[l]), conv_ln_g=row(conv_ln_g[l]),
                         conv_ln_b=row(conv_ln_b[l]), n_taps=n_taps)
        mats = dict(wb=both(_s5_input_blocks(bb_re, bb_im, n_grp, n_p)),
                    wc=both(_s5_output_blocks(s5_c_re[l].astype(F32), s5_c_im[l].astype(F32), n_p)),
                    wglu=both(s5_w_glu[l]), wpool=both(_block_diag(pool_w[l].astype(F32))))
        w1 = both(w_in[l, :, :d_u])
        b1 = row(b_in[l, :d_u])
        shared_lw = dict(b_gates=row(b_in[l, d_u:]), b_out=row(b_out[l]), ln1_g=row(ln1_g[l]), ln1_b=row(ln1_b[l]))
        lmats = dict(w_gates=both(w_in[l, :, d_u:]), proj_s5=both(proj_s5[l]), proj_pool=both(proj_pool[l]),
                     proj_conv=both(proj_conv[l]), w_out=both(w_out[l]))
        is_moe = l % 2 == 1
        router = None
        if is_moe:
            assert n_exp <= ROUTER_LANES // 2
            router = jnp.pad(moe_router[l // 2].astype(F32), ((0, 0), (0, ROUTER_LANES - n_exp)))

        mw = dict(shared_mw, **{k: v[0] for k, v in mats.items()})
        lw = dict(shared_lw, **{k: v[0] for k, v in lmats.items()})
        yp, hT_p, pool_p, conv_p = _mixers(xp, w1[0], b1, zeros_p["h0"], zeros_p["pool"], zeros_p["conv"], mw,
                                           nb=n_bp, tc=tc_p, start_pos=0, precise=False)
        res_p = _merge(xp, yp, lw, alpha=alpha, tm=tm_p, precise=False, router=router, n_exp=n_exp,
                       shared_rows=n_tok)
        mw = dict(shared_mw, **{k: v[1] for k, v in mats.items()})
        lw = dict(shared_lw, **{k: v[1] for k, v in lmats.items()})
        h0_s = jnp.concatenate([state_s5_re[l].reshape(n_bs, n_state), state_s5_im[l].reshape(n_bs, n_state)],
                               axis=1).astype(F32)
        ys, hT_s, pool_s, conv_s = _mixers(xs, w1[1], b1, h0_s,
                                           _pad_time_major(cache_pool[l].astype(F32), POOL_HIST),
                                           _pad_time_major(cache_conv[l].astype(F32), CONV_HIST), mw,
                                           nb=n_bs, tc=1, start_pos=PAST_LEN, precise=True)
        res_s = _merge(xs, ys, lw, alpha=alpha, tm=n_bs, precise=True, router=router, n_exp=n_exp,
                       shared=res_p[0] if is_moe else None, shared_rows=n_tok, row0=n_p_tok)

        outs["re_p"].append(hT_p[:, :n_state].reshape(n_bp, n_grp, n_p))
        outs["im_p"].append(hT_p[:, n_state:].reshape(n_bp, n_grp, n_p))
        outs["pool_p"].append(_from_time_major(pool_p, n_bp))
        outs["conv_p"].append(_from_time_major(conv_p, n_bp))
        outs["re_s"].append(hT_s[:, :n_state].reshape(n_bs, n_grp, n_p))
        outs["im_s"].append(hT_s[:, n_state:].reshape(n_bs, n_grp, n_p))
        outs["pool_s"].append(_from_time_major(pool_s, n_bs))
        outs["conv_s"].append(_from_time_major(conv_s, n_bs))

        g2, b2 = row(ln2_g[l]), row(ln2_b[l])
        i = l // 2
        if not is_moe:
            wg, wu = jnp.swapaxes(ffn_w_gate[i], -1, -2).astype(F32), jnp.swapaxes(ffn_w_up[i], -1, -2).astype(F32)
            wd = ffn_w_down[i].astype(F32)
            xp = _ffn_dense(res_p[0], wg, wu, wd, g2, b2, alpha=alpha, tm=tm_p, ff_chunk=ff_chunk,
                            precise=False)
            xs = _ffn_dense(res_s[0], wg, wu, wd, g2, b2, alpha=alpha, tm=n_bs, ff_chunk=ff_chunk,
                            precise=True)
        else:
            wg, wu = jnp.swapaxes(moe_w_gate[i], -1, -2).astype(BF16), jnp.swapaxes(moe_w_up[i], -1, -2).astype(BF16)
            wd = moe_w_down[i].astype(BF16)
            _, ri_p, rw_p = res_p
            x1_all, ri_s, rw_s = res_s
            top_i = jnp.concatenate([ri_p[:, :TOP_K], ri_s[:, :TOP_K]], axis=0)
            tm_moe = MOE_TILE_ROWS
            slot_tok, tile_expert, n_used, slots = _moe_route(top_i, n_exp, tm_moe)
            n_tiles = slot_tok.shape[0] // tm_moe
            bounds = sorted({0} | {max(1, round(n_tiles * f)) for f in MOE_PART_ENDS})
            y_sorted = None
            for t0, t1 in zip(bounds[:-1], bounds[1:]):
                x_part = x1_all.at[slot_tok[t0 * tm_moe:t1 * tm_moe]].get(mode="promise_in_bounds")
                y_sorted = _moe_grouped(tile_expert, n_used, x_part, wg, wu, wd, tm=tm_moe, ff_chunk=ff_chunk,
                                        tile0=t0, n_slots=slot_tok.shape[0], y_prev=y_sorted)
            def picked(r0, cnt):
                return [y_sorted.at[slots[r0:r0 + cnt, k]].get(mode="promise_in_bounds") for k in range(TOP_K)]

            last_layer = l == depth - 1
            p_tiles = n_p_tok // tm_p
            bounds = sorted({tm_p * ((p_tiles * k) // MOE_PARTS) for k in range(MOE_PARTS + 1)})
            xp = None
            for r0, r1 in zip(bounds[:-1], bounds[1:]):
                ya, yb = picked(r0, r1 - r0)
                if last_layer:
                    xp = _moe_combine(x1_all, ya, yb, rw_p, g2, b2, alpha=alpha, tm=tm_p, x_row0=0,
                                      rows=(r0, r1 - r0), nb=n_bp, n_seq_steps=n_tp, out_prev=xp)
                else:
                    part = _moe_combine(x1_all, ya, yb, rw_p, g2, b2, alpha=alpha, tm=tm_p, x_row0=0,
                                        rows=(r0, r1 - r0))
                    xp = part if xp is None else jnp.concatenate([xp, part], axis=0)
            ya, yb = picked(n_p_tok, n_bs)
            xs = _moe_combine(x1_all, ya, yb, rw_s, g2, b2, alpha=alpha, tm=n_bs, x_row0=n_p_tok, rows=(0, n_bs))

    y_prompt = xp if xp.ndim == 3 else _from_time_major(xp, n_bp)
    st = lambda k: jnp.stack(outs[k])
    return (y_prompt, xs.reshape(n_bs, 1, d_model), st("re_p"), st("im_p"), st("pool_p"), st("conv_p"),
            st("re_s"), st("im_s"), st("pool_s"), st("conv_s"))
```
